```python
import math
import jax
import jax.numpy as jnp
from jax import lax
import numpy as np

D_MODEL = 1024
BATCH = 16
SEQ = 4096
DEPTH = 4

CTX_LEN = 256
GRID_W = 64
N_MIXERS = 4
N_OCC = tuple((DEPTH - m + N_MIXERS - 1) // N_MIXERS for m in range(N_MIXERS))

HEAD_DIM = 64
ROPE_BASE = 10000.0
NORM_EPS = 1e-6
NEG_INF = -1e30

WIN_HEADS = D_MODEL // HEAD_DIM
WIN_KV_HEADS = WIN_HEADS // 4
WINDOW = 128
WIN_BLOCK = 128
WIN_QKV = (WIN_HEADS + 2 * WIN_KV_HEADS) * HEAD_DIM

DIFF_HEADS = D_MODEL // (2 * HEAD_DIM)
DIFF_Q_BLOCK = 128

POOL_WINDOWS = (2, 4, 8, 16)
POOL_GROUPS = len(POOL_WINDOWS)
POOL_GROUP_DIM = D_MODEL // POOL_GROUPS

RWKV_HEAD = 64
RWKV_HEADS = D_MODEL // RWKV_HEAD
DECAY_LORA = 64
AAA_LORA = 64
GATE_LORA = 128
GN_EPS = 64e-5

N_GROUPS = 4
EXPERTS_PER_GROUP = 8
N_EXPERTS = N_GROUPS * EXPERTS_PER_GROUP
TOP_K_IN_GROUP = 2
EXPERT_FF = D_MODEL // 2
MOE_BLOCK = 256

kernel_name = "hybrid_dit_interleaved_moe"


def rmsnorm(x, g):
    xf = x.astype(jnp.float32)
    y = xf * lax.rsqrt(jnp.mean(xf * xf, axis=-1, keepdims=True) + NORM_EPS)
    return (y * g.astype(jnp.float32)).astype(x.dtype)


def axial_rope_tables(rows, dim):
    row = jnp.repeat(jnp.arange(rows), GRID_W).astype(jnp.float32)
    col = jnp.tile(jnp.arange(GRID_W), rows).astype(jnp.float32)
    nf = dim // 4
    inv = ROPE_BASE ** (-jnp.arange(nf, dtype=jnp.float32) / nf)
    ar, ac = row[:, None] * inv, col[:, None] * inv
    ang = jnp.concatenate([ar, ar, ac, ac], axis=-1)
    return jnp.cos(ang), jnp.sin(ang)


def apply_rope(x, cos, sin):
    shp = x.shape
    x4 = x.reshape(shp[:-1] + (2, 2, shp[-1] // 4))
    rot = jnp.concatenate([-x4[..., 1:, :], x4[..., :1, :]], axis=-2).reshape(shp)
    bshape = (shp[1],) + (1,) * (x.ndim - 3) + (shp[-1],)
    return x * cos.reshape(bshape).astype(x.dtype) + rot * sin.reshape(bshape).astype(x.dtype)


def window_attention(h, hc, cos, sin, w_qkv, sink, w_o, ctx_out):
    B, S, D = h.shape
    G = WIN_HEADS // WIN_KV_HEADS
    nq, nkv = WIN_HEADS * HEAD_DIM, WIN_KV_HEADS * HEAD_DIM

    def proj(t):
        qkv = t @ w_qkv
        lead = t.shape[:2]
        q = qkv[..., :nq].reshape(lead + (WIN_KV_HEADS, G, HEAD_DIM)) * HEAD_DIM ** -0.5
        k = qkv[..., nq:nq + nkv].reshape(lead + (WIN_KV_HEADS, HEAD_DIM))
        v = qkv[..., nq + nkv:].reshape(lead + (WIN_KV_HEADS, HEAD_DIM))
        return q, k, v

    q, k, v = proj(h)
    qc, kc, vc = proj(hc)
    q, k = apply_rope(q, cos, sin), apply_rope(k, cos, sin)
    sink_f = sink.astype(jnp.float32).reshape(1, WIN_KV_HEADS, G, 1, 1)

    def sink_softmax(parts):
        s = jnp.concatenate(parts + [jnp.broadcast_to(sink_f, parts[0].shape[:-1] + (1,))], axis=-1)
        return jax.nn.softmax(s, axis=-1)

    pad = ((0, 0), (WIN_BLOCK, WIN_BLOCK), (0, 0), (0, 0))
    kp, vp = jnp.pad(k, pad), jnp.pad(v, pad)
    nb = S // WIN_BLOCK
    span = 3 * WIN_BLOCK
    rel = jnp.arange(span)[None, :] - WIN_BLOCK - jnp.arange(WIN_BLOCK)[:, None]
    in_window = jnp.abs(rel) <= WINDOW
    qb = jnp.moveaxis(q.reshape((B, nb, WIN_BLOCK) + q.shape[2:]), 1, 0)

    def block(args):
        n, qblk = args
        start = n * WIN_BLOCK
        kblk = lax.dynamic_slice_in_dim(kp, start, span, axis=1)
        vblk = lax.dynamic_slice_in_dim(vp, start, span, axis=1)
        key_pos = start - WIN_BLOCK + jnp.arange(span)
        mask = in_window & ((key_pos >= 0) & (key_pos < S))[None, :]
        s_loc = jnp.einsum('bqhgd,bkhd->bhgqk', qblk, kblk, preferred_element_type=jnp.float32)
        s_loc = jnp.where(mask, s_loc, NEG_INF)
        s_ctx = jnp.einsum('bqhgd,bchd->bhgqc', qblk, kc, preferred_element_type=jnp.float32)
        p = sink_softmax([s_loc, s_ctx]).astype(v.dtype)
        return (jnp.einsum('bhgqk,bkhd->bqhgd', p[..., :span], vblk)
                + jnp.einsum('bhgqc,bchd->bqhgd', p[..., span:-1], vc))

    o = lax.map(block, (jnp.arange(nb), qb))
    y = jnp.moveaxis(o, 0, 1).reshape(B, S, D) @ w_o
    yc = None
    if ctx_out:
        s = jnp.einsum('bqhgd,bkhd->bhgqk', qc, kc, preferred_element_type=jnp.float32)
        pc = sink_softmax([s]).astype(v.dtype)
        yc = jnp.einsum('bhgqk,bkhd->bqhgd', pc[..., :-1], vc).reshape(hc.shape) @ w_o
    return y, yc


def diff_attention(h, hc, cos, sin, w_qkv, lam, subln_g, w_o, lambda_init, ctx_out):
    B, S, D = h.shape
    dv = 2 * HEAD_DIM

    def proj(t):
        qkv = t @ w_qkv
        lead = t.shape[:2]
        q = qkv[..., :D].reshape(lead + (DIFF_HEADS, 2, HEAD_DIM)) * HEAD_DIM ** -0.5
        k = qkv[..., D:2 * D].reshape(lead + (DIFF_HEADS, 2, HEAD_DIM))
        v = qkv[..., 2 * D:].reshape(lead + (DIFF_HEADS, dv))
        return q, k, v

    q, k, v = proj(h)
    qc, kc, vc = proj(hc)
    q, k = apply_rope(q, cos, sin), apply_rope(k, cos, sin)
    lf = lam.astype(jnp.float32)
    lam_full = jnp.exp(jnp.sum(lf[0] * lf[1])) - jnp.exp(jnp.sum(lf[2] * lf[3])) + lambda_init

    def attend(qblk, keys, vals):
        s = jnp.einsum('bqhmd,bkhmd->bhmqk', qblk, keys, preferred_element_type=jnp.float32)
        p = jax.nn.softmax(s, axis=-1)
        a = (p[:, :, 0] - lam_full * p[:, :, 1]).astype(vals.dtype)
        return jnp.einsum('bhqk,bkhd->bqhd', a, vals)

    def post(o):
        o = rmsnorm(o, subln_g) * (1.0 - lambda_init)
        return o.reshape(o.shape[:2] + (D,)) @ w_o

    k_all = jnp.concatenate([k, kc], axis=1)
    v_all = jnp.concatenate([v, vc], axis=1)
    nb = S // DIFF_Q_BLOCK
    qb = jnp.moveaxis(q.reshape((B, nb, DIFF_Q_BLOCK) + q.shape[2:]), 1, 0)
    o = lax.map(lambda qblk: attend(qblk, k_all, v_all), qb)
    y = post(jnp.moveaxis(o, 0, 1).reshape(B, S, DIFF_HEADS, dv))
    yc = post(attend(qc, kc, vc)) if ctx_out else None
    return y, yc


def multiscale_pool(t, w_group, b_group, layer_scale):
    B, L, D = t.shape
    tf = t.astype(jnp.float32)
    cs = jnp.concatenate([jnp.zeros((B, 1, D), jnp.float32), jnp.cumsum(tf, axis=1)], axis=1)
    pos = jnp.arange(L)
    means = []
    for gi, w in enumerate(POOL_WINDOWS):
        lo = jnp.clip(pos - w // 2, 0, L)
        hi = jnp.clip(pos + w - w // 2, 0, L)
        csg = cs[..., gi * POOL_GROUP_DIM:(gi + 1) * POOL_GROUP_DIM]
        means.append((csg[:, hi] - csg[:, lo]) / (hi - lo).astype(jnp.float32)[None, :, None])
    y = (jnp.concatenate(means, axis=-1) - tf).astype(t.dtype).reshape(B, L, POOL_GROUPS, POOL_GROUP_DIM)
    y = jnp.einsum('blgc,gcd->blgd', y, w_group).reshape(B, L, D) + b_group
    return y * layer_scale


def rwkv_features(t, p, with_out):
    mu, w_rkv, w0, w_a1, w_a2, a0, a_a1, a_a2, g1, g2, k_k, k_a = p
    B, L, D = t.shape
    prev = jnp.pad(t, ((0, 0), (1, 0), (0, 0)))[:, :-1]
    nxt = jnp.pad(t, ((0, 0), (0, 1), (0, 0)))[:, 1:]
    dp, dn = prev - t, nxt - t

    def mix(i):
        return t + dp * mu[0, i] + dn * mu[1, i]

    def heads(u):
        return u.astype(jnp.float32).reshape(B, L, RWKV_HEADS, RWKV_HEAD)

    k = (mix(2) @ w_rkv[1]).astype(jnp.float32)
    v = heads(mix(3) @ w_rkv[2])
    kk = heads(k * k_k.astype(jnp.float32))
    kk = kk / jnp.maximum(jnp.sqrt(jnp.sum(kk * kk, axis=-1, keepdims=True)), 1e-12)
    xw, xa = mix(1), mix(4)
    dirs = []
    for d in range(2):
        w = -jax.nn.softplus(-(w0[d] + jnp.tanh(xw @ w_a1[d]) @ w_a2[d]).astype(jnp.float32)) - 0.5
        a = jax.nn.sigmoid((a0[d] + (xa @ a_a1[d]) @ a_a2[d]).astype(jnp.float32))
        kd = k * (1.0 + (a - 1.0) * k_a.astype(jnp.float32))
        dirs.append((heads(jnp.exp(-jnp.exp(w))), heads(a), heads(kd)))
    r = heads(mix(0) @ w_rkv[0]) if with_out else None
    g = (jax.nn.sigmoid(mix(5) @ g1) @ g2).astype(jnp.float32) if with_out else None
    return r, v, kk, g, dirs


def rwkv_scan(state0, decay, k, v, kk, a, r, reverse):
    emit = r is not None
    seq = (decay, k, v, kk, a) + ((r,) if emit else ())
    xs = tuple(jnp.moveaxis(u, 1, 0) for u in seq)

    def step(S, inp):
        w_t, k_t, v_t, kk_t, a_t = inp[:5]
        sa = jnp.einsum('bhvk,bhk->bhv', S, kk_t)
        S = (S * w_t[:, :, None, :] - sa[..., None] * (kk_t * a_t)[:, :, None, :]
             + v_t[..., None] * k_t[:, :, None, :])
        return S, (jnp.einsum('bhvk,bhk->bhv', S, inp[5]) if emit else None)

    S_fin, out = lax.scan(step, state0, xs, reverse=reverse)
    return S_fin, (jnp.moveaxis(out, 0, 1) if emit else None)


def rwkv_output(o, r, v, kds, g, r_k, ln_w, ln_b, w_o, dtype):
    B, L = o.shape[:2]
    mean = jnp.mean(o, axis=-1, keepdims=True)
    var = jnp.mean(jnp.square(o - mean), axis=-1, keepdims=True)
    on = ((o - mean) * lax.rsqrt(var + GN_EPS)).reshape(B, L, D_MODEL) * ln_w.astype(jnp.float32) + ln_b.astype(jnp.float32)
    rk = r_k.astype(jnp.float32)
    bonus = (jnp.sum(r * kds[0] * rk, axis=-1, keepdims=True) * v
             + jnp.sum(r * kds[1] * rk, axis=-1, keepdims=True) * v).reshape(B, L, D_MODEL)
    return ((on + bonus) * g).astype(dtype) @ w_o


def rwkv_mix(h, hc, feat_p, out_p, ctx_out):
    r_k, ln_w, ln_b, w_o = out_p
    r, v, kk, g, dirs = rwkv_features(h, feat_p, True)
    rc, vc, kkc, gc, dirs_c = rwkv_features(hc, feat_p, ctx_out)
    B = h.shape[0]
    outs, outs_c = [], []
    for d, rev in enumerate((False, True)):
        S0 = jnp.zeros((B, RWKV_HEADS, RWKV_HEAD, RWKV_HEAD), jnp.float32)
        dec_c, a_c, k_c = dirs_c[d]
        S_c, o_c = rwkv_scan(S0, dec_c, k_c, vc, kkc, a_c, rc, rev)
        dec, a, kd = dirs[d]
        _, o_l = rwkv_scan(S_c, dec, kd, v, kk, a, r, rev)
        outs.append(o_l)
        outs_c.append(o_c)
    y = rwkv_output(outs[0] + outs[1], r, v, [dd[2] for dd in dirs], g, r_k, ln_w, ln_b, w_o, h.dtype)
    yc = None
    if ctx_out:
        yc = rwkv_output(outs_c[0] + outs_c[1], rc, vc, [dd[2] for dd in dirs_c], gc, r_k, ln_w, ln_b, w_o, hc.dtype)
    return y, yc


def expert_dispatch(t, eid, wts, w_up, w_down):
    T, D = t.shape
    A = T * TOP_K_IN_GROUP
    flat_e = eid.reshape(A)
    flat_tok = jnp.arange(A, dtype=jnp.int32) // TOP_K_IN_GROUP
    flat_w = wts.reshape(A)
    order = jnp.argsort(flat_e)
    se, stok, sw = flat_e[order], flat_tok[order], flat_w[order]
    counts = jnp.zeros((N_EXPERTS,), jnp.int32).at[flat_e].add(1)
    start = jnp.cumsum(counts) - counts
    nblk = (counts + MOE_BLOCK - 1) // MOE_BLOCK
    blk_end = jnp.cumsum(nblk)
    blk_start = blk_end - nblk
    dest = blk_start[se] * MOE_BLOCK + (jnp.arange(A, dtype=jnp.int32) - start[se])
    NB = -(-A // MOE_BLOCK) + N_EXPERTS
    src = jnp.full((NB * MOE_BLOCK,), T, jnp.int32).at[dest].set(stok)
    xb = jnp.concatenate([t, jnp.zeros((1, D), t.dtype)], axis=0)[src].reshape(NB, MOE_BLOCK, D)
    blk_expert = jnp.minimum(jnp.searchsorted(blk_end, jnp.arange(NB), side='right'), N_EXPERTS - 1)

    def run(args):
        xblk, e = args
        u = xblk @ w_up[e]
        return (jax.nn.silu(u[:, :EXPERT_FF]) * u[:, EXPERT_FF:]) @ w_down[e]

    yb = lax.map(run, (xb, blk_expert)).reshape(NB * MOE_BLOCK, D)
    return jnp.zeros((T, D), t.dtype).at[stok].add(yb[dest] * sw[:, None].astype(t.dtype))


def hier_moe(t, wg, bg, we, be, w_up, w_down):
    T, D = t.shape
    lg = (t @ wg).astype(jnp.float32) + bg.astype(jnp.float32)
    gsel = jnp.argmax(lg, axis=-1)
    p_grp = jnp.max(jax.nn.softmax(lg, axis=-1), axis=-1)
    le = ((t @ we).astype(jnp.float32) + be.astype(jnp.float32)).reshape(T, N_GROUPS, EXPERTS_PER_GROUP)
    le_sel = jnp.einsum('tge,tg->te', le, jax.nn.one_hot(gsel, N_GROUPS, dtype=jnp.float32))
    top_p, top_i = lax.top_k(jax.nn.softmax(le_sel, axis=-1), TOP_K_IN_GROUP)
    wts = p_grp[:, None] * top_p / jnp.sum(top_p, axis=-1, keepdims=True)
    eid = (gsel[:, None] * EXPERTS_PER_GROUP + top_i).astype(jnp.int32)
    return expert_dispatch(t, eid, wts, w_up, w_down)


def setup_inputs(seed: int = 0) -> dict:
    key = jax.random.key(seed)
    keys = iter(jax.random.split(key, 64))
    f32 = jnp.float32

    def nrm(shape, scale):
        return jax.random.normal(next(keys), shape, f32) * scale

    def near_one(shape):
        return 1.0 + nrm(shape, 0.02)

    NA, NBD, NC, ND = N_OCC
    D = D_MODEL
    return {
        "x": nrm((BATCH, SEQ, D), 1.0),
        "c": nrm((BATCH, D), 1.0),
        "ctx": nrm((BATCH, CTX_LEN, D), 1.0),
        "c_ctx": nrm((D,), 1.0),
        "mod_w": nrm((DEPTH, D, 6 * D), 0.5 * D ** -0.5),
        "mod_b": nrm((DEPTH, 6 * D), 0.02),
        "norm_g": near_one((DEPTH, 2, D)),
        "final_g": near_one((D,)),
        "win_w_qkv": nrm((NA, D, WIN_QKV), D ** -0.5),
        "win_sink": nrm((NA, WIN_HEADS), 0.5),
        "win_w_o": nrm((NA, D, D), D ** -0.5),
        "diff_w_qkv": nrm((NBD, D, 3 * D), D ** -0.5),
        "diff_lambda": nrm((NBD, 4, HEAD_DIM), 0.1),
        "diff_subln_g": near_one((NBD, 2 * HEAD_DIM)),
        "diff_w_o": nrm((NBD, D, D), D ** -0.5),
        "pool_w_group": nrm((NC, POOL_GROUPS, POOL_GROUP_DIM, POOL_GROUP_DIM), POOL_GROUP_DIM ** -0.5),
        "pool_b_group": nrm((NC, D), 0.02),
        "pool_scale": near_one((NC, D)),
        "rwkv_mu": jax.random.uniform(next(keys), (ND, 2, 6, D), f32, 0.0, 0.5),
        "rwkv_w_rkv": nrm((ND, 3, D, D), D ** -0.5),
        "rwkv_w0": jax.random.uniform(next(keys), (ND, 2, D), f32, -5.0, -0.5),
        "rwkv_w_a1": nrm((ND, 2, D, DECAY_LORA), D ** -0.5),
        "rwkv_w_a2": nrm((ND, 2, DECAY_LORA, D), 0.1 * DECAY_LORA ** -0.5),
        "rwkv_a0": nrm((ND, 2, D), 0.1),
        "rwkv_a_a1": nrm((ND, 2, D, AAA_LORA), D ** -0.5),
        "rwkv_a_a2": nrm((ND, 2, AAA_LORA, D), 0.1 * AAA_LORA ** -0.5),
        "rwkv_g1": nrm((ND, D, GATE_LORA), D ** -0.5),
        "rwkv_g2": nrm((ND, GATE_LORA, D), GATE_LORA ** -0.5),
        "rwkv_k_k": 0.85 + nrm((ND, D), 0.02),
        "rwkv_k_a": near_one((ND, D)),
        "rwkv_r_k": nrm((ND, RWKV_HEADS, RWKV_HEAD), 0.1),
        "rwkv_ln_w": near_one((ND, D)),
        "rwkv_ln_b": nrm((ND, D), 0.02),
        "rwkv_w_o": nrm((ND, D, D), D ** -0.5),
        "moe_wg": nrm((DEPTH, D, N_GROUPS), D ** -0.5),
        "moe_bg": nrm((DEPTH, N_GROUPS), 0.01),
        "moe_we": nrm((DEPTH, D, N_EXPERTS), D ** -0.5),
        "moe_be": nrm((DEPTH, N_EXPERTS), 0.01),
        "moe_w_up": nrm((DEPTH, N_EXPERTS, D, 2 * EXPERT_FF), D ** -0.5),
        "moe_w_down": nrm((DEPTH, N_EXPERTS, EXPERT_FF, D), EXPERT_FF ** -0.5),
    }


def reference(x, c, ctx, c_ctx, mod_w, mod_b, norm_g, final_g,
              win_w_qkv, win_sink, win_w_o,
              diff_w_qkv, diff_lambda, diff_subln_g, diff_w_o,
              pool_w_group, pool_b_group, pool_scale,
              rwkv_mu, rwkv_w_rkv, rwkv_w0, rwkv_w_a1, rwkv_w_a2, rwkv_a0, rwkv_a_a1, rwkv_a_a2,
              rwkv_g1, rwkv_g2, rwkv_k_k, rwkv_k_a, rwkv_r_k, rwkv_ln_w, rwkv_ln_b, rwkv_w_o,
              moe_wg, moe_bg, moe_we, moe_be, moe_w_up, moe_w_down):
    B, S, D = x.shape
    C = ctx.shape[1]
    ROWS = S // GRID_W
    cos, sin = axial_rope_tables(ROWS, HEAD_DIM)
    xc = ctx
    for i in range(DEPTH):
        m, occ = i % N_MIXERS, i // N_MIXERS
        last = i == DEPTH - 1
        mod = (jax.nn.silu(c) @ mod_w[i] + mod_b[i]).reshape(B, 6, 1, D)
        modc = (jax.nn.silu(c_ctx) @ mod_w[i] + mod_b[i]).reshape(6, D)
        h = rmsnorm(x, norm_g[i, 0]) * (1 + mod[:, 1]) + mod[:, 0]
        need_ctx = (not last) or m != 2
        hc = rmsnorm(xc, norm_g[i, 0]) * (1 + modc[1]) + modc[0] if need_ctx else None
        if m == 0:
            y, yc = window_attention(h, hc, cos, sin, win_w_qkv[occ], win_sink[occ], win_w_o[occ], not last)
        elif m == 1:
            lambda_init = 0.8 - 0.6 * math.exp(-0.3 * i)
            y, yc = diff_attention(h, hc, cos, sin, diff_w_qkv[occ], diff_lambda[occ], diff_subln_g[occ],
                                   diff_w_o[occ], lambda_init, not last)
        elif m == 2:
            y = multiscale_pool(h, pool_w_group[occ], pool_b_group[occ], pool_scale[occ])
            yc = None if last else multiscale_pool(hc, pool_w_group[occ], pool_b_group[occ], pool_scale[occ])
        else:
            feat_p = (rwkv_mu[occ], rwkv_w_rkv[occ], rwkv_w0[occ], rwkv_w_a1[occ], rwkv_w_a2[occ],
                      rwkv_a0[occ], rwkv_a_a1[occ], rwkv_a_a2[occ], rwkv_g1[occ], rwkv_g2[occ],
                      rwkv_k_k[occ], rwkv_k_a[occ])
            out_p = (rwkv_r_k[occ], rwkv_ln_w[occ], rwkv_ln_b[occ], rwkv_w_o[occ])
            y, yc = rwkv_mix(h, hc, feat_p, out_p, not last)
        x = x + mod[:, 2] * y
        h = rmsnorm(x, norm_g[i, 1]) * (1 + mod[:, 4]) + mod[:, 3]
        moe_p = (moe_wg[i], moe_bg[i], moe_we[i], moe_be[i], moe_w_up[i], moe_w_down[i])
        if last:
            x = x + mod[:, 5] * hier_moe(h.reshape(B * S, D), *moe_p).reshape(B, S, D)
        else:
            xc = xc + modc[2] * yc
            hc = rmsnorm(xc, norm_g[i, 1]) * (1 + modc[4]) + modc[3]
            out = hier_moe(jnp.concatenate([h.reshape(B * S, D), hc.reshape(B * C, D)], axis=0), *moe_p)
            x = x + mod[:, 5] * out[:B * S].reshape(B, S, D)
            xc = xc + modc[5] * out[B * S:].reshape(B, C, D)
    return rmsnorm(x, final_g)
```

```python
import functools
import math

import jax
import jax.numpy as jnp
from jax import lax
from jax.experimental import pallas as pl
from jax.experimental.pallas import tpu as pltpu

F32 = jnp.float32
BF16 = jnp.bfloat16

HEAD_DIM = 64
GRID_W = 64
ROPE_BASE = 10000.0
NORM_EPS = 1e-6
NEG_INF = -1e30
WIN_KV_HEADS = 4
WIN_BLOCK = 128
POOL_WINDOWS = (2, 4, 8, 16)
POOL_HALO = 8
GN_EPS = 64e-5
N_GROUPS = 4
EXPERTS_PER_GROUP = 8
N_EXPERTS = N_GROUPS * EXPERTS_PER_GROUP
MOE_BLOCK = 256
RWKV_CHUNK = 64
RWKV_HEADS_PER_STEP = 4
LANES = 128
V7X_VMEM_LIMIT = 48 * 1024 * 1024
HI = lax.Precision.HIGHEST


def _cp(sem, vmem=V7X_VMEM_LIMIT):
    return pltpu.CompilerParams(dimension_semantics=sem, vmem_limit_bytes=vmem)


def _bf(x):
    return x.astype(BF16)


def _dot(a, b, precision=None):
    return jnp.dot(a, b, preferred_element_type=F32, precision=precision)


def _dot_t(a, b, precision=None):
    return lax.dot_general(a, b, (((1,), (1,)), ((), ())), preferred_element_type=F32, precision=precision)


def _dot_l(a, b, precision=None):
    return lax.dot_general(a, b, (((0,), (0,)), ((), ())), preferred_element_type=F32, precision=precision)


def _norm_mod(x, g, scale, shift):
    ms = jnp.mean(x * x, axis=-1, keepdims=True)
    y = x * lax.rsqrt(ms + NORM_EPS) * g
    return y * (1.0 + scale) + shift


def _sigmoid(x):
    return 1.0 / (1.0 + jnp.exp(-x))


def _row_tile(n, pref):
    t = min(pref, n)
    assert n % t == 0
    return t


def _mod_body(c_ref, w_ref, b_ref, o_ref):
    c = c_ref[...]
    s = c * _sigmoid(c)
    o_ref[0] = _dot(_bf(s), _bf(w_ref[0])) + b_ref[0]


def _mod_all(c_all, mod_w, mod_b):
    depth, d, n = mod_w.shape
    r = c_all.shape[0]
    tn = 1536
    return pl.pallas_call(
        _mod_body,
        grid=(depth, n // tn),
        in_specs=[
            pl.BlockSpec((r, d), lambda i, j: (0, 0)),
            pl.BlockSpec((1, d, tn), lambda i, j: (i, 0, j)),
            pl.BlockSpec((1, 1, tn), lambda i, j: (i, 0, j)),
        ],
        out_specs=pl.BlockSpec((1, r, tn), lambda i, j: (i, 0, j)),
        out_shape=jax.ShapeDtypeStruct((depth, r, n), F32),
        compiler_params=_cp(("parallel", "parallel")),
        name="adaln_mod",
    )(c_all, mod_w, mod_b.reshape(depth, 1, n))


def _mod_spec(mod):
    if mod.shape[0] == 1:
        return pl.BlockSpec((1,) + mod.shape[1:], lambda b, i: (0, 0, 0))
    return pl.BlockSpec((1,) + mod.shape[1:], lambda b, i: (b, 0, 0))


def _rope_tile(y, cos, sin, first_half):
    fwd = pltpu.roll(y, LANES - 16, 1)
    bwd = pltpu.roll(y, 16, 1)
    rot = jnp.where(first_half, -fwd, bwd)
    return y * cos + rot * sin


def _proj_body(*refs, n_rope, n_q, scale_i, shift_i):
    if n_rope:
        x_ref, g_ref, m_ref, w_ref, cos_ref, sin_ref, o_ref = refs
    else:
        x_ref, g_ref, m_ref, w_ref, o_ref = refs
    m = m_ref[0]
    h = _norm_mod(x_ref[0], g_ref[...], m[scale_i:scale_i + 1], m[shift_i:shift_i + 1])
    y = _dot(_bf(h), w_ref[...])
    n = y.shape[1]
    if n_rope:
        cos, sin = cos_ref[...], sin_ref[...]
        first_half = (lax.broadcasted_iota(jnp.int32, cos.shape, 1) & 31) < 16
    for j in range(n // LANES):
        blk = y[:, j * LANES:(j + 1) * LANES]
        if j * LANES < n_q:
            blk = blk * (HEAD_DIM ** -0.5)
        if j * LANES < n_rope:
            blk = _rope_tile(blk, cos, sin, first_half)
        o_ref[0, :, j * LANES:(j + 1) * LANES] = _bf(blk)


def _proj(x, g, mod, w, cos, sin, *, n_rope, n_q, scale_i, shift_i, name):
    b, l, d = x.shape
    n = w.shape[1]
    tm = _row_tile(l, 256)
    in_specs = [
        pl.BlockSpec((1, tm, d), lambda bi, i: (bi, i, 0)),
        pl.BlockSpec((1, d), lambda bi, i: (0, 0)),
        _mod_spec(mod),
        pl.BlockSpec((d, n), lambda bi, i: (0, 0)),
    ]
    args = [x, g.reshape(1, d), mod, w]
    if n_rope:
        in_specs += [pl.BlockSpec((tm, LANES), lambda bi, i: (i, 0))] * 2
        args += [cos, sin]
    return pl.pallas_call(
        functools.partial(_proj_body, n_rope=n_rope, n_q=n_q, scale_i=scale_i, shift_i=shift_i),
        grid=(b, l // tm),
        in_specs=in_specs,
        out_specs=pl.BlockSpec((1, tm, n), lambda bi, i: (bi, i, 0)),
        out_shape=jax.ShapeDtypeStruct((b, l, n), BF16),
        compiler_params=_cp(("parallel", "parallel")),
        name=name,
    )(*args)


def _res_body(a_ref, w_ref, x_ref, m_ref, o_ref, *, gate_i):
    y = _dot(a_ref[0], w_ref[...])
    o_ref[0] = x_ref[0] + m_ref[0][gate_i:gate_i + 1] * y


def _res_proj(a, w, x, mod, *, gate_i, name):
    b, l, d = x.shape
    k = a.shape[-1]
    tm = _row_tile(l, 512)
    in_specs = [
        pl.BlockSpec((1, tm, k), lambda bi, i: (bi, i, 0)),
        pl.BlockSpec((k, d), lambda bi, i: (0, 0)),
        pl.BlockSpec((1, tm, d), lambda bi, i: (bi, i, 0)),
        _mod_spec(mod),
    ]
    args = [a, w, x, mod]
    return pl.pallas_call(
        functools.partial(_res_body, gate_i=gate_i),
        grid=(b, l // tm),
        in_specs=in_specs,
        out_specs=pl.BlockSpec((1, tm, d), lambda bi, i: (bi, i, 0)),
        out_shape=jax.ShapeDtypeStruct((b, l, d), F32),
        input_output_aliases={2: 0},
        compiler_params=_cp(("parallel", "parallel")),
        name=name,
    )(*args)


def _win_body(*refs, seq, local, n_heads):
    if local:
        q_ref, kp_ref, kc_ref, kn_ref, vp_ref, vc_ref, vn_ref, kx_ref, vx_ref, sink_ref, o_ref = refs
    else:
        q_ref, kx_ref, vx_ref, sink_ref, o_ref = refs
    tq = q_ref.shape[1]
    grp = n_heads // WIN_KV_HEADS
    n = pl.program_id(1)
    if local:
        span = 3 * tq
        row = lax.broadcasted_iota(jnp.int32, (grp * tq, span), 0) & (tq - 1)
        col = lax.broadcasted_iota(jnp.int32, (grp * tq, span), 1)
        rel = col - tq - row
        key_pos = n * tq - tq + col
        mask = (jnp.abs(rel) <= tq) & (key_pos >= 0) & (key_pos < seq)
    for hk in range(WIN_KV_HEADS):
        ks = slice(hk * HEAD_DIM, (hk + 1) * HEAD_DIM)
        qh = jnp.concatenate(
            [q_ref[0, :, (hk * grp + g) * HEAD_DIM:(hk * grp + g + 1) * HEAD_DIM] for g in range(grp)], axis=0)
        sink = jnp.concatenate(
            [jnp.full((tq, 1), sink_ref[hk * grp + g], F32) for g in range(grp)], axis=0)
        kx = kx_ref[0, :, ks]
        vx = vx_ref[0, :, ks]
        s_ctx = _dot_t(qh, kx)
        m = jnp.maximum(jnp.max(s_ctx, axis=1, keepdims=True), sink)
        if local:
            k_loc = jnp.concatenate([kp_ref[0, :, ks], kc_ref[0, :, ks], kn_ref[0, :, ks]], axis=0)
            v_loc = jnp.concatenate([vp_ref[0, :, ks], vc_ref[0, :, ks], vn_ref[0, :, ks]], axis=0)
            s_loc = jnp.where(mask, _dot_t(qh, k_loc), NEG_INF)
            m = jnp.maximum(m, jnp.max(s_loc, axis=1, keepdims=True))
        p_ctx = jnp.exp(s_ctx - m)
        den = jnp.sum(p_ctx, axis=1, keepdims=True) + jnp.exp(sink - m)
        o = _dot(_bf(p_ctx), vx)
        if local:
            p_loc = jnp.exp(s_loc - m)
            den = den + jnp.sum(p_loc, axis=1, keepdims=True)
            o = o + _dot(_bf(p_loc), v_loc)
        o = o / den
        for g in range(grp):
            hq = hk * grp + g
            o_ref[0, :, hq * HEAD_DIM:(hq + 1) * HEAD_DIM] = _bf(o[g * tq:(g + 1) * tq])


def _win_attn(qkv, qkvc, sink, *, local, name):
    src = qkv if local else qkvc
    b, l, _ = src.shape
    c = qkvc.shape[1]
    n_heads = sink.shape[0]
    d = n_heads * HEAD_DIM
    kvw = WIN_KV_HEADS * HEAD_DIM
    kcol = d // kvw
    tq = WIN_BLOCK
    nb = l // tq
    in_specs = [pl.BlockSpec((1, tq, d), lambda bi, i: (bi, i, 0))]
    args = [src]
    if local:
        for colb in (kcol, kcol + 1):
            in_specs += [
                pl.BlockSpec((1, tq, kvw), lambda bi, i, colb=colb: (bi, jnp.maximum(i - 1, 0), colb)),
                pl.BlockSpec((1, tq, kvw), lambda bi, i, colb=colb: (bi, i, colb)),
                pl.BlockSpec((1, tq, kvw), lambda bi, i, colb=colb: (bi, jnp.minimum(i + 1, nb - 1), colb)),
            ]
            args += [qkv, qkv, qkv]
    in_specs += [
        pl.BlockSpec((1, c, kvw), lambda bi, i: (bi, 0, kcol)),
        pl.BlockSpec((1, c, kvw), lambda bi, i: (bi, 0, kcol + 1)),
        pl.BlockSpec(memory_space=pltpu.SMEM),
    ]
    args += [qkvc, qkvc, sink]
    return pl.pallas_call(
        functools.partial(_win_body, seq=l, local=local, n_heads=n_heads),
        grid=(b, nb),
        in_specs=in_specs,
        out_specs=pl.BlockSpec((1, tq, d), lambda bi, i: (bi, i, 0)),
        out_shape=jax.ShapeDtypeStruct((b, l, d), BF16),
        compiler_params=_cp(("parallel", "parallel")),
        name=name,
    )(*args)


def _diff_body(*refs, n_lat, tk, lambda_init):
    if n_lat:
        lam_ref, g_ref, q_ref, kl_ref, vl_ref, kx_ref, vx_ref, o_ref = refs
    else:
        lam_ref, g_ref, q_ref, kx_ref, vx_ref, o_ref = refs
    q = q_ref[0]
    tq = q.shape[0]
    lane = lax.broadcasted_iota(jnp.int32, q.shape, 1)
    zero = jnp.zeros_like(q)
    qq = jnp.concatenate([jnp.where(lane < HEAD_DIM, q, zero), jnp.where(lane >= HEAD_DIM, q, zero)], axis=0)

    def step(k, v, carry):
        m, l, acc = carry
        s = _dot_t(qq, k)
        m_new = jnp.maximum(m, jnp.max(s, axis=1, keepdims=True))
        alpha = jnp.exp(m - m_new)
        p = jnp.exp(s - m_new)
        l = alpha * l + jnp.sum(p, axis=1, keepdims=True)
        acc = alpha * acc + _dot(_bf(p), v)
        return m_new, l, acc

    carry = (jnp.full((2 * tq, 1), NEG_INF, F32), jnp.zeros((2 * tq, 1), F32),
             jnp.zeros((2 * tq, 2 * HEAD_DIM), F32))
    if n_lat:
        def lat_step(i, carry):
            off = pl.multiple_of(i * tk, tk)
            return step(kl_ref[0, pl.ds(off, tk), :], vl_ref[0, pl.ds(off, tk), :], carry)
        carry = lax.fori_loop(0, n_lat, lat_step, carry)
    _, l, acc = step(kx_ref[0], vx_ref[0], carry)
    lam = lam_ref[...]
    lam_full = (jnp.exp(jnp.sum(lam[0:1] * lam[1:2], axis=1, keepdims=True))
                - jnp.exp(jnp.sum(lam[2:3] * lam[3:4], axis=1, keepdims=True)) + lambda_init)
    a = acc / l
    o = a[:tq] - lam_full * a[tq:]
    ms = jnp.mean(o * o, axis=-1, keepdims=True)
    o = o * lax.rsqrt(ms + NORM_EPS) * g_ref[...] * (1.0 - lambda_init)
    o_ref[0] = _bf(o)


def _diff_attn(qkv, qkvc, lam, subln_g, *, local, lambda_init, name):
    src = qkv if local else qkvc
    b, l, n3 = src.shape
    d = n3 // 3
    c = qkvc.shape[1]
    hw = 2 * HEAD_DIM
    nh = d // hw
    tq = _row_tile(l, 256)
    tk = 256
    in_specs = [
        pl.BlockSpec((4, HEAD_DIM), lambda bi, h, i: (0, 0)),
        pl.BlockSpec((1, hw), lambda bi, h, i: (0, 0)),
        pl.BlockSpec((1, tq, hw), lambda bi, h, i: (bi, i, h)),
    ]
    args = [lam, subln_g.reshape(1, hw), src]
    n_lat = 0
    if local:
        s = qkv.shape[1]
        n_lat = s // tk
        in_specs += [
            pl.BlockSpec((1, s, hw), lambda bi, h, i: (bi, 0, nh + h)),
            pl.BlockSpec((1, s, hw), lambda bi, h, i: (bi, 0, 2 * nh + h)),
        ]
        args += [qkv, qkv]
    in_specs += [
        pl.BlockSpec((1, c, hw), lambda bi, h, i: (bi, 0, nh + h)),
        pl.BlockSpec((1, c, hw), lambda bi, h, i: (bi, 0, 2 * nh + h)),
    ]
    args += [qkvc, qkvc]
    return pl.pallas_call(
        functools.partial(_diff_body, n_lat=n_lat, tk=tk, lambda_init=lambda_init),
        grid=(b, nh, l // tq),
        in_specs=in_specs,
        out_specs=pl.BlockSpec((1, tq, hw), lambda bi, h, i: (bi, i, h)),
        out_shape=jax.ShapeDtypeStruct((b, l, d), BF16),
        compiler_params=_cp(("parallel", "parallel", "parallel")),
        name=name,
    )(*args)


def _halo_specs(l, tm, d):
    nh = l // POOL_HALO
    per = tm // POOL_HALO
    return [
        pl.BlockSpec((1, POOL_HALO, d), lambda bi, i: (bi, jnp.maximum(i * per - 1, 0), 0)),
        pl.BlockSpec((1, tm, d), lambda bi, i: (bi, i, 0)),
        pl.BlockSpec((1, POOL_HALO, d), lambda bi, i: (bi, jnp.minimum((i + 1) * per, nh - 1), 0)),
    ]


def _fill_normed(h_ref, xp_ref, x_ref, xn_ref, g, scale, shift):
    tm = x_ref.shape[1]
    h_ref[0:POOL_HALO, :] = _norm_mod(xp_ref[0], g, scale, shift)
    h_ref[POOL_HALO:POOL_HALO + tm, :] = _norm_mod(x_ref[0], g, scale, shift)
    h_ref[POOL_HALO + tm:2 * POOL_HALO + tm, :] = _norm_mod(xn_ref[0], g, scale, shift)


def _pool_body(xp_ref, x_ref, xn_ref, g_ref, m_ref, w_ref, b_ref, s_ref, o_ref, h_ref, *, seq):
    tm = x_ref.shape[1]
    d = x_ref.shape[2]
    gd = d // len(POOL_WINDOWS)
    m = m_ref[0]
    _fill_normed(h_ref, xp_ref, x_ref, xn_ref, g_ref[...], m[1:2], m[0:1])
    pos = pl.program_id(1) * tm + lax.broadcasted_iota(jnp.int32, (tm, 1), 0)
    x = x_ref[0]
    for gi, w in enumerate(POOL_WINDOWS):
        cols = slice(gi * gd, (gi + 1) * gd)
        acc = jnp.zeros((tm, gd), F32)
        for off in range(-(w // 2), w - w // 2):
            valid = (pos + off >= 0) & (pos + off < seq)
            acc = acc + jnp.where(valid, h_ref[POOL_HALO + off:POOL_HALO + off + tm, cols], 0.0)
        lo = jnp.maximum(pos - w // 2, 0)
        hi = jnp.minimum(pos + w - w // 2, seq)
        y = acc / (hi - lo).astype(F32) - h_ref[POOL_HALO:POOL_HALO + tm, cols]
        y = (_dot(_bf(y), w_ref[gi]) + b_ref[:, cols]) * s_ref[:, cols]
        o_ref[0, :, cols] = x[:, cols] + m[2:3, cols] * y


def _pool_mix(x, g, mod, w_group, b_group, layer_scale, *, name):
    b, l, d = x.shape
    tm = _row_tile(l, 256)
    gd = d // len(POOL_WINDOWS)
    return pl.pallas_call(
        functools.partial(_pool_body, seq=l),
        grid=(b, l // tm),
        in_specs=_halo_specs(l, tm, d) + [
            pl.BlockSpec((1, d), lambda bi, i: (0, 0)),
            _mod_spec(mod),
            pl.BlockSpec((len(POOL_WINDOWS), gd, gd), lambda bi, i: (0, 0, 0)),
            pl.BlockSpec((1, d), lambda bi, i: (0, 0)),
            pl.BlockSpec((1, d), lambda bi, i: (0, 0)),
        ],
        out_specs=pl.BlockSpec((1, tm, d), lambda bi, i: (bi, i, 0)),
        out_shape=jax.ShapeDtypeStruct((b, l, d), F32),
        scratch_shapes=[pltpu.VMEM((tm + 2 * POOL_HALO, d), F32)],
        compiler_params=_cp(("parallel", "parallel")),
        name=name,
    )(x, x, x, g.reshape(1, d), mod, _bf(w_group), b_group.reshape(1, d), layer_scale.reshape(1, d))


def _softplus(z):
    return jnp.maximum(z, 0.0) + jnp.log1p(jnp.exp(-jnp.abs(z)))


def _rwkv_feat_body(*refs, seq, with_out):
    (xp_ref, x_ref, xn_ref, g_ref, m_ref, mu_ref, wrkv_ref, w0_ref, wa1_ref, wa2_ref, a0_ref, aa1_ref,
     aa2_ref, g1_ref, g2_ref, kk_ref, ka_ref) = refs[:17]
    if with_out:
        r_out, v_out, kk_out, lw_out, b_out, kd_out, g_out, h_ref = refs[17:]
    else:
        v_out, kk_out, lw_out, b_out, kd_out, h_ref = refs[17:]
    tm = x_ref.shape[1]
    d = x_ref.shape[2]
    nh = d // HEAD_DIM
    m = m_ref[0]
    _fill_normed(h_ref, xp_ref, x_ref, xn_ref, g_ref[...], m[1:2], m[0:1])
    pos = pl.program_id(1) * tm + lax.broadcasted_iota(jnp.int32, (tm, 1), 0)
    t = h_ref[POOL_HALO:POOL_HALO + tm, :]
    dp = jnp.where(pos >= 1, h_ref[POOL_HALO - 1:POOL_HALO - 1 + tm, :], 0.0) - t
    dn = jnp.where(pos < seq - 1, h_ref[POOL_HALO + 1:POOL_HALO + 1 + tm, :], 0.0) - t

    def mix(i):
        return _bf(t + dp * mu_ref[0, i:i + 1, :] + dn * mu_ref[1, i:i + 1, :])

    def put(ref, val, lead=()):
        for h in range(nh):
            ref[lead + (0, h)] = val[:, h * HEAD_DIM:(h + 1) * HEAD_DIM]

    k = _dot(mix(2), wrkv_ref[1])
    put(v_out, _dot(mix(3), wrkv_ref[2]))
    kk = k * kk_ref[...]
    kk_heads = []
    for h in range(nh):
        kh = kk[:, h * HEAD_DIM:(h + 1) * HEAD_DIM]
        nrm = jnp.sqrt(jnp.sum(kh * kh, axis=-1, keepdims=True))
        kk_heads.append(kh / jnp.maximum(nrm, 1e-12))
        kk_out[0, h] = kk_heads[h]
    xw, xa = mix(1), mix(4)
    for di in range(2):
        z = w0_ref[di:di + 1, :] + _dot(_bf(jnp.tanh(_dot(xw, wa1_ref[di]))), wa2_ref[di])
        w = -_softplus(-z) - 0.5
        put(lw_out, -jnp.exp(w), (di,))
        a = _sigmoid(a0_ref[di:di + 1, :] + _dot(_bf(_dot(xa, aa1_ref[di])), aa2_ref[di]))
        put(kd_out, k * (1.0 + (a - 1.0) * ka_ref[...]), (di,))
        for h in range(nh):
            cols = slice(h * HEAD_DIM, (h + 1) * HEAD_DIM)
            b_out[di, 0, h] = kk_heads[h] * a[:, cols]
    if with_out:
        put(r_out, _dot(mix(0), wrkv_ref[0]))
        g_out[0] = _dot(_bf(_sigmoid(_dot(mix(5), g1_ref[...]))), g2_ref[...])


def _rwkv_features(x, g, mod, p, *, with_out, name):
    b, l, d = x.shape
    nh = d // HEAD_DIM
    tm = _row_tile(l, 128)
    const = lambda a: pl.BlockSpec(a.shape, lambda bi, i, nd=a.ndim: (0,) * nd)
    weights = [p["mu"], _bf(p["w_rkv"]), p["w0"], _bf(p["w_a1"]), _bf(p["w_a2"]), p["a0"], _bf(p["a_a1"]),
               _bf(p["a_a2"]), _bf(p["g1"]), _bf(p["g2"]), p["k_k"].reshape(1, d), p["k_a"].reshape(1, d)]
    head = pl.BlockSpec((1, nh, tm, HEAD_DIM), lambda bi, i: (bi, 0, i, 0))
    head2 = pl.BlockSpec((2, 1, nh, tm, HEAD_DIM), lambda bi, i: (0, bi, 0, i, 0))
    hs = jax.ShapeDtypeStruct((b, nh, l, HEAD_DIM), F32)
    hs2 = jax.ShapeDtypeStruct((2, b, nh, l, HEAD_DIM), F32)
    out_specs = [head, head, head2, head2, head2]
    out_shape = [hs, hs, hs2, hs2, hs2]
    if with_out:
        out_specs = [head] + out_specs + [pl.BlockSpec((1, tm, d), lambda bi, i: (bi, i, 0))]
        out_shape = [hs] + out_shape + [jax.ShapeDtypeStruct((b, l, d), F32)]
    return pl.pallas_call(
        functools.partial(_rwkv_feat_body, seq=l, with_out=with_out),
        grid=(b, l // tm),
        in_specs=_halo_specs(l, tm, d) + [pl.BlockSpec((1, d), lambda bi, i: (0, 0)), _mod_spec(mod)]
        + [const(a) for a in weights],
        out_specs=out_specs,
        out_shape=out_shape,
        scratch_shapes=[pltpu.VMEM((tm + 2 * POOL_HALO, d), F32)],
        compiler_params=_cp(("parallel", "parallel")),
        name=name,
    )(x, x, x, g.reshape(1, d), mod, *weights)


def _scan_body(*refs, emit, n_chunks):
    if emit:
        kk_ref, v_ref, r_ref, lw_ref, b_ref, kd_ref, s0_ref, o_ref, sfin_ref, s_ref = refs
    else:
        kk_ref, v_ref, lw_ref, b_ref, kd_ref, s0_ref, sfin_ref, s_ref = refs
    hb = kk_ref.shape[1]
    L = kk_ref.shape[2]
    rev = pl.program_id(0) == 1
    c = pl.program_id(3)

    @pl.when(c == 0)
    def _():
        s_ref[...] = s0_ref[0, 0]

    row = lax.broadcasted_iota(jnp.int32, (L, L), 0)
    col = lax.broadcasted_iota(jnp.int32, (L, L), 1)
    ahead = (row - col) * (1 - 2 * rev.astype(jnp.int32))
    incl = ahead >= 0
    strict = ahead > 0
    ones_incl = incl.astype(F32)
    kr = lax.broadcasted_iota(jnp.int32, (HEAD_DIM, HEAD_DIM), 0)
    kc = lax.broadcasted_iota(jnp.int32, (HEAD_DIM, HEAD_DIM), 1)
    eye = kr == kc

    for h in range(hb):
        lw = lw_ref[0, 0, h]
        kk = kk_ref[0, h]
        v = v_ref[0, h]
        b = b_ref[0, 0, h]
        kd = kd_ref[0, 0, h]
        s0 = s_ref[h]
        cl = _dot(ones_incl, lw, HI)
        cl_tot = jnp.sum(lw, axis=0, keepdims=True)
        w_inv = jnp.exp(-cl)
        kh = kk * jnp.exp(cl - lw)
        kt = kd * w_inv
        bt = b * w_inv
        if emit:
            rh = r_ref[0, h] * jnp.exp(cl)
            lhs = jnp.concatenate([kh, rh], axis=0)
        else:
            lhs = kh
        big = _dot_t(lhs, jnp.concatenate([bt, kt], axis=0), HI)
        m_b = jnp.where(strict, big[:L, :L], 0.0)
        m_k = jnp.where(strict, big[:L, L:], 0.0)
        y = jnp.concatenate([kh, _dot(m_k, v, HI)], axis=1)
        nk = -m_b
        y = y + _dot(nk, y, HI)
        steps = int(math.log2(L)) - 1
        for _ in range(steps):
            nk = _dot(nk, nk, HI)
            y = y + _dot(nk, y, HI)
        w_last = jnp.exp(cl_tot - cl)
        kt_l = kd * w_last
        bt_l = b * w_last
        gh = _dot_l(y, bt_l, HI)
        g_mat = jnp.where(eye, jnp.exp(cl_tot), 0.0) - gh[:HEAD_DIM]
        h_mat = _dot_l(v, kt_l, HI) - gh[HEAD_DIM:]
        if emit:
            a_b = jnp.where(incl, big[L:, :L], 0.0)
            a_k = jnp.where(incl, big[L:, L:], 0.0)
            ab_y = _dot(a_b, y, HI)
            r_bar = rh - ab_y[:, :HEAD_DIM]
            o_loc = _dot(a_k, v, HI) - ab_y[:, HEAD_DIM:]
            o_ref[0, 0, h] = _dot_t(r_bar, s0, HI) + o_loc
        s_ref[h] = _dot(s0, g_mat, HI) + h_mat

    @pl.when(c == n_chunks - 1)
    def _():
        sfin_ref[0, 0] = s_ref[...]


def _rwkv_scan(kk, v, r, lw, b, kd, s0, *, name):
    bsz, nh, l, dh = kk.shape
    L = RWKV_CHUNK
    hb = RWKV_HEADS_PER_STEP
    nch = l // L
    emit = r is not None

    def chunk(d, c):
        return c + d * (nch - 1 - 2 * c)

    one = pl.BlockSpec((1, hb, L, dh), lambda d, bi, hg, c: (bi, hg, chunk(d, c), 0))
    two = pl.BlockSpec((1, 1, hb, L, dh), lambda d, bi, hg, c: (d, bi, hg, chunk(d, c), 0))
    st = pl.BlockSpec((1, 1, hb, dh, dh), lambda d, bi, hg, c: (d, bi, hg, 0, 0))
    in_specs = [one, one] + ([one] if emit else []) + [two, two, two, st]
    args = [kk, v] + ([r] if emit else []) + [lw, b, kd, s0]
    s_shape = jax.ShapeDtypeStruct((2, bsz, nh, dh, dh), F32)
    if emit:
        out_specs = [two, st]
        out_shape = [jax.ShapeDtypeStruct((2, bsz, nh, l, dh), F32), s_shape]
    else:
        out_specs = [st]
        out_shape = [s_shape]
    return pl.pallas_call(
        functools.partial(_scan_body, emit=emit, n_chunks=nch),
        grid=(2, bsz, nh // hb, nch),
        in_specs=in_specs,
        out_specs=out_specs,
        out_shape=out_shape,
        scratch_shapes=[pltpu.VMEM((hb, dh, dh), F32)],
        compiler_params=_cp(("parallel", "parallel", "parallel", "arbitrary")),
        name=name,
    )(*args)


def _rwkv_out_body(o_ref, r_ref, v_ref, kd_ref, g_ref, rk_ref, lnw_ref, lnb_ref, w_ref, x_ref, m_ref, out_ref):
    nh = r_ref.shape[1]
    tm = r_ref.shape[2]
    acc = jnp.zeros((tm, x_ref.shape[2]), F32)
    for h in range(nh):
        cols = slice(h * HEAD_DIM, (h + 1) * HEAD_DIM)
        o = o_ref[0, 0, h] + o_ref[1, 0, h]
        mean = jnp.mean(o, axis=-1, keepdims=True)
        var = jnp.mean(jnp.square(o - mean), axis=-1, keepdims=True)
        on = (o - mean) * lax.rsqrt(var + GN_EPS) * lnw_ref[h:h + 1, :] + lnb_ref[h:h + 1, :]
        r = r_ref[0, h]
        rk = rk_ref[h:h + 1, :]
        bonus = (jnp.sum(r * kd_ref[0, 0, h] * rk, axis=-1, keepdims=True) * v_ref[0, h]
                 + jnp.sum(r * kd_ref[1, 0, h] * rk, axis=-1, keepdims=True) * v_ref[0, h])
        a = (on + bonus) * g_ref[0, :, cols]
        acc = acc + _dot(_bf(a), w_ref[cols, :])
    out_ref[0] = x_ref[0] + m_ref[0][2:3] * acc


def _rwkv_output(o, r, v, kd, g, r_k, ln_w, ln_b, w_o, x, mod, *, name):
    b, l, d = x.shape
    nh = d // HEAD_DIM
    tm = _row_tile(l, 256)
    head = pl.BlockSpec((1, nh, tm, HEAD_DIM), lambda bi, i: (bi, 0, i, 0))
    head2 = pl.BlockSpec((2, 1, nh, tm, HEAD_DIM), lambda bi, i: (0, bi, 0, i, 0))
    row = pl.BlockSpec((1, tm, d), lambda bi, i: (bi, i, 0))
    small = pl.BlockSpec((nh, HEAD_DIM), lambda bi, i: (0, 0))
    return pl.pallas_call(
        _rwkv_out_body,
        grid=(b, l // tm),
        in_specs=[head2, head, head, head2, row, small, small, small,
                  pl.BlockSpec((d, d), lambda bi, i: (0, 0)), row, _mod_spec(mod)],
        out_specs=row,
        out_shape=jax.ShapeDtypeStruct((b, l, d), F32),
        input_output_aliases={9: 0},
        compiler_params=_cp(("parallel", "parallel")),
        name=name,
    )(o, r, v, kd, g, r_k, ln_w.reshape(nh, HEAD_DIM), ln_b.reshape(nh, HEAD_DIM), _bf(w_o), x, mod)


def _router_body(x_ref, g_ref, m_ref, w_ref, b_ref, h_ref, sel_ref):
    m = m_ref[0]
    h = _norm_mod(x_ref[0], g_ref[...], m[4:5], m[3:4])
    h_ref[0] = _bf(h)
    lg = _dot(h, w_ref[...], HI) + b_ref[...]
    lane = lax.broadcasted_iota(jnp.int32, lg.shape, 1)
    big = jnp.int32(1 << 20)
    g_logit = jnp.where(lane < N_GROUPS, lg, -jnp.inf)
    g_max = jnp.max(g_logit, axis=1, keepdims=True)
    gsel = jnp.min(jnp.where(g_logit == g_max, lane, big), axis=1, keepdims=True)
    p_grp = 1.0 / jnp.sum(jnp.exp(g_logit - g_max), axis=1, keepdims=True)
    lo = N_GROUPS + gsel * EXPERTS_PER_GROUP
    e_logit = jnp.where((lane >= lo) & (lane < lo + EXPERTS_PER_GROUP), lg, -jnp.inf)
    e_max = jnp.max(e_logit, axis=1, keepdims=True)
    i1 = jnp.min(jnp.where(e_logit == e_max, lane, big), axis=1, keepdims=True)
    rest = jnp.where(lane == i1, -jnp.inf, e_logit)
    e2 = jnp.max(rest, axis=1, keepdims=True)
    i2 = jnp.min(jnp.where(rest == e2, lane, big), axis=1, keepdims=True)
    q2 = jnp.exp(e2 - e_max)
    w1 = p_grp / (1.0 + q2)
    w2 = p_grp * q2 / (1.0 + q2)
    sel = jnp.where(lane == 0, (i1 - N_GROUPS).astype(F32),
                    jnp.where(lane == 1, (i2 - N_GROUPS).astype(F32),
                              jnp.where(lane == 2, w1, jnp.where(lane == 3, w2, 0.0))))
    sel_ref[0] = sel


def _router(x, g, mod, w_r, b_r, *, name):
    b, l, d = x.shape
    tm = _row_tile(l, 256)
    return pl.pallas_call(
        _router_body,
        grid=(b, l // tm),
        in_specs=[
            pl.BlockSpec((1, tm, d), lambda bi, i: (bi, i, 0)),
            pl.BlockSpec((1, d), lambda bi, i: (0, 0)),
            _mod_spec(mod),
            pl.BlockSpec((d, LANES), lambda bi, i: (0, 0)),
            pl.BlockSpec((1, LANES), lambda bi, i: (0, 0)),
        ],
        out_specs=[pl.BlockSpec((1, tm, d), lambda bi, i: (bi, i, 0)),
                   pl.BlockSpec((1, tm, LANES), lambda bi, i: (bi, i, 0))],
        out_shape=[jax.ShapeDtypeStruct((b, l, d), BF16), jax.ShapeDtypeStruct((b, l, LANES), F32)],
        compiler_params=_cp(("parallel", "parallel")),
        name=name,
    )(x, g.reshape(1, d), mod, w_r, b_r)


def _expert_body(be_ref, nu_ref, x_ref, wu_ref, wd_ref, o_ref):
    i = pl.program_id(0)
    ff = wd_ref.shape[1]

    @pl.when(i < nu_ref[0])
    def _():
        u = _dot(x_ref[...], wu_ref[0])
        gate = u[:, :ff]
        act = gate * _sigmoid(gate) * u[:, ff:]
        o_ref[...] = _dot(_bf(act), wd_ref[0])

    @pl.when(i >= nu_ref[0])
    def _():
        o_ref[...] = jnp.zeros_like(o_ref)


def _experts(xb, blk_expert, n_used, w_up, w_down):
    rows, d = xb.shape
    nb = rows // MOE_BLOCK
    ff2 = w_up.shape[2]
    ff = w_down.shape[1]
    return pl.pallas_call(
        _expert_body,
        grid_spec=pltpu.PrefetchScalarGridSpec(
            num_scalar_prefetch=2,
            grid=(nb,),
            in_specs=[
                pl.BlockSpec((MOE_BLOCK, d), lambda i, be, nu: (i, 0)),
                pl.BlockSpec((1, d, ff2), lambda i, be, nu: (be[i], 0, 0)),
                pl.BlockSpec((1, ff, d), lambda i, be, nu: (be[i], 0, 0)),
            ],
            out_specs=pl.BlockSpec((MOE_BLOCK, d), lambda i, be, nu: (i, 0)),
        ),
        out_shape=jax.ShapeDtypeStruct((rows, d), F32),
        compiler_params=_cp(("arbitrary",)),
        name="moe_experts",
    )(blk_expert, n_used, xb, w_up, w_down)


def _moe_res_body(x_ref, y0_ref, y1_ref, sel_ref, m_ref, o_ref):
    sel = sel_ref[0]
    y = y0_ref[0] * sel[:, 2:3] + y1_ref[0] * sel[:, 3:4]
    o_ref[0] = x_ref[0] + m_ref[0][5:6] * y


def _moe_res(x, y0, y1, sel, mod, *, name):
    b, l, d = x.shape
    tm = _row_tile(l, 512)
    row = pl.BlockSpec((1, tm, d), lambda bi, i: (bi, i, 0))
    return pl.pallas_call(
        _moe_res_body,
        grid=(b, l // tm),
        in_specs=[row, row, row, pl.BlockSpec((1, tm, LANES), lambda bi, i: (bi, i, 0)), _mod_spec(mod)],
        out_specs=row,
        out_shape=jax.ShapeDtypeStruct((b, l, d), F32),
        input_output_aliases={0: 0},
        compiler_params=_cp(("parallel", "parallel")),
        name=name,
    )(x, y0, y1, sel, mod)


def _dispatch_tables(eid, n_tok):
    a = n_tok * 2
    flat_e = eid.reshape(a)
    order = jnp.argsort(flat_e)
    se = flat_e[order]
    counts = jnp.sum((flat_e[:, None] == jnp.arange(N_EXPERTS, dtype=jnp.int32)[None, :]).astype(jnp.int32), axis=0)
    start = jnp.cumsum(counts) - counts
    nblk = (counts + MOE_BLOCK - 1) // MOE_BLOCK
    blk_end = jnp.cumsum(nblk)
    blk_start = blk_end - nblk
    dest = blk_start[se] * MOE_BLOCK + (jnp.arange(a, dtype=jnp.int32) - start[se])
    nb = -(-a // MOE_BLOCK) + N_EXPERTS
    blk_expert = jnp.minimum(jnp.searchsorted(blk_end, jnp.arange(nb, dtype=jnp.int32), side="right"),
                             N_EXPERTS - 1).astype(jnp.int32)
    rows = jnp.arange(nb * MOE_BLOCK, dtype=jnp.int32)
    e_row = blk_expert[rows // MOE_BLOCK]
    local = rows - blk_start[e_row] * MOE_BLOCK
    sorted_idx = jnp.clip(start[e_row] + local, 0, a - 1)
    src = jnp.where(local < counts[e_row], order[sorted_idx] // 2, 0)
    pos = jnp.zeros((a,), jnp.int32).at[order].set(dest).reshape(n_tok, 2)
    return src, pos, blk_expert, blk_end[-1:].astype(jnp.int32)


def _hier_moe(x, xc, mod, modc, g2, wg, bg, we, be, w_up, w_down, li):
    b, s, d = x.shape
    w_r = jnp.zeros((d, LANES), F32).at[:, :N_GROUPS].set(wg).at[:, N_GROUPS:N_GROUPS + N_EXPERTS].set(we)
    b_r = jnp.zeros((1, LANES), F32).at[0, :N_GROUPS].set(bg).at[0, N_GROUPS:N_GROUPS + N_EXPERTS].set(be)
    h, sel = _router(x, g2, mod, w_r, b_r, name=f"router{li}")
    h = h.reshape(b * s, d)
    sel_flat = sel.reshape(b * s, LANES)
    if xc is not None:
        c = xc.shape[1]
        hc, selc = _router(xc, g2, modc, w_r, b_r, name=f"router_ctx{li}")
        h = jnp.concatenate([h, hc.reshape(b * c, d)], axis=0)
        sel_flat = jnp.concatenate([sel_flat, selc.reshape(b * c, LANES)], axis=0)
    n_tok = h.shape[0]
    eid = sel_flat[:, 0:2].astype(jnp.int32)
    src, pos, blk_expert, n_used = _dispatch_tables(eid, n_tok)
    yb = _experts(h[src], blk_expert, n_used, _bf(w_up), _bf(w_down))
    y0 = yb[pos[:, 0]]
    y1 = yb[pos[:, 1]]
    n_lat = b * s
    x = _moe_res(x, y0[:n_lat].reshape(b, s, d), y1[:n_lat].reshape(b, s, d), sel, mod, name=f"moe_res{li}")
    if xc is not None:
        xc = _moe_res(xc, y0[n_lat:].reshape(b, c, d), y1[n_lat:].reshape(b, c, d), selc, modc,
                      name=f"moe_res_ctx{li}")
    return x, xc


def _final_body(x_ref, g_ref, o_ref):
    x = x_ref[0]
    ms = jnp.mean(x * x, axis=-1, keepdims=True)
    o_ref[0] = x * lax.rsqrt(ms + NORM_EPS) * g_ref[...]


def _final_norm(x, g):
    b, l, d = x.shape
    tm = _row_tile(l, 512)
    row = pl.BlockSpec((1, tm, d), lambda bi, i: (bi, i, 0))
    return pl.pallas_call(
        _final_body,
        grid=(b, l // tm),
        in_specs=[row, pl.BlockSpec((1, d), lambda bi, i: (0, 0))],
        out_specs=row,
        out_shape=jax.ShapeDtypeStruct((b, l, d), F32),
        compiler_params=_cp(("parallel", "parallel")),
        name="final_norm",
    )(x, g.reshape(1, d))


def _rope_tables(seq):
    rows = seq // GRID_W
    row = jnp.repeat(jnp.arange(rows), GRID_W).astype(F32)
    col = jnp.tile(jnp.arange(GRID_W), rows).astype(F32)
    nf = HEAD_DIM // 4
    inv = ROPE_BASE ** (-jnp.arange(nf, dtype=F32) / nf)
    ar, ac = row[:, None] * inv, col[:, None] * inv
    ang = jnp.concatenate([ar, ar, ac, ac] * (LANES // HEAD_DIM), axis=-1)
    return jnp.cos(ang), jnp.sin(ang)


def kernel(x, c, ctx, c_ctx, mod_w, mod_b, norm_g, final_g, win_w_qkv, win_sink, win_w_o, diff_w_qkv, diff_lambda, diff_subln_g, diff_w_o, pool_w_group, pool_b_group, pool_scale, rwkv_mu, rwkv_w_rkv, rwkv_w0, rwkv_w_a1, rwkv_w_a2, rwkv_a0, rwkv_a_a1, rwkv_a_a2, rwkv_g1, rwkv_g2, rwkv_k_k, rwkv_k_a, rwkv_r_k, rwkv_ln_w, rwkv_ln_b, rwkv_w_o, moe_wg, moe_bg, moe_we, moe_be, moe_w_up, moe_w_down):
    b, s, d = x.shape
    depth = mod_w.shape[0]
    n_mixers = 4
    cos, sin = _rope_tables(s)
    rows = -(-(b + 1) // 8) * 8
    c_all = jnp.zeros((rows, d), F32).at[:b].set(c).at[b].set(c_ctx)
    mods = _mod_all(c_all, mod_w, mod_b).reshape(depth, rows, 6, d)
    xc = ctx
    for i in range(depth):
        m, occ = i % n_mixers, i // n_mixers
        last = i == depth - 1
        mod = mods[i, :b]
        modc = mods[i, b:b + 1]
        g1 = norm_g[i, 0]
        if m == 0:
            nq = win_sink.shape[1] * HEAD_DIM
            nk = WIN_KV_HEADS * HEAD_DIM
            w = _bf(win_w_qkv[occ])
            qkv = _proj(x, g1, mod, w, cos, sin, n_rope=nq + nk, n_q=nq, scale_i=1, shift_i=0, name="win_qkv")
            qkvc = _proj(xc, g1, modc, w, None, None, n_rope=0, n_q=nq, scale_i=1, shift_i=0, name="win_qkv_ctx")
            o = _win_attn(qkv, qkvc, win_sink[occ], local=True, name="win_attn")
            wo = _bf(win_w_o[occ])
            x = _res_proj(o, wo, x, mod, gate_i=2, name="win_out")
            if not last:
                oc = _win_attn(None, qkvc, win_sink[occ], local=False, name="win_attn_ctx")
                xc = _res_proj(oc, wo, xc, modc, gate_i=2, name="win_out_ctx")
        elif m == 1:
            lambda_init = 0.8 - 0.6 * math.exp(-0.3 * i)
            w = _bf(diff_w_qkv[occ])
            qkv = _proj(x, g1, mod, w, cos, sin, n_rope=2 * d, n_q=d, scale_i=1, shift_i=0, name="diff_qkv")
            qkvc = _proj(xc, g1, modc, w, None, None, n_rope=0, n_q=d, scale_i=1, shift_i=0, name="diff_qkv_ctx")
            o = _diff_attn(qkv, qkvc, diff_lambda[occ], diff_subln_g[occ], local=True,
                           lambda_init=lambda_init, name="diff_attn")
            wo = _bf(diff_w_o[occ])
            x = _res_proj(o, wo, x, mod, gate_i=2, name="diff_out")
            if not last:
                oc = _diff_attn(None, qkvc, diff_lambda[occ], diff_subln_g[occ], local=False,
                                lambda_init=lambda_init, name="diff_attn_ctx")
                xc = _res_proj(oc, wo, xc, modc, gate_i=2, name="diff_out_ctx")
        elif m == 2:
            x = _pool_mix(x, g1, mod, pool_w_group[occ], pool_b_group[occ], pool_scale[occ], name="pool")
            if not last:
                xc = _pool_mix(xc, g1, modc, pool_w_group[occ], pool_b_group[occ], pool_scale[occ], name="pool_ctx")
        else:
            p = dict(mu=rwkv_mu[occ], w_rkv=rwkv_w_rkv[occ], w0=rwkv_w0[occ], w_a1=rwkv_w_a1[occ],
                     w_a2=rwkv_w_a2[occ], a0=rwkv_a0[occ], a_a1=rwkv_a_a1[occ], a_a2=rwkv_a_a2[occ],
                     g1=rwkv_g1[occ], g2=rwkv_g2[occ], k_k=rwkv_k_k[occ], k_a=rwkv_k_a[occ])
            assert last, "the context stream's RWKV output path is only needed for non-final layers"
            r, v, kk, lw, bb, kd, g = _rwkv_features(x, g1, mod, p, with_out=True, name="rwkv_feat")
            vc, kkc, lwc, bc, kdc = _rwkv_features(xc, g1, modc, p, with_out=False, name="rwkv_feat_ctx")
            nh = d // HEAD_DIM
            s0 = jnp.zeros((2, b, nh, HEAD_DIM, HEAD_DIM), F32)
            (s_ctx,) = _rwkv_scan(kkc, vc, None, lwc, bc, kdc, s0, name="rwkv_scan_ctx")
            o, _ = _rwkv_scan(kk, v, r, lw, bb, kd, s_ctx, name="rwkv_scan")
            x = _rwkv_output(o, r, v, kd, g, rwkv_r_k[occ], rwkv_ln_w[occ], rwkv_ln_b[occ], rwkv_w_o[occ],
                             x, mod, name="rwkv_out")
        x, xc = _hier_moe(x, None if last else xc, mod, modc, norm_g[i, 1], moe_wg[i], moe_bg[i], moe_we[i],
                          moe_be[i], moe_w_up[i], moe_w_down[i], i)
    return _final_norm(x, final_g)
```

```python
import functools
import math

import jax
import jax.numpy as jnp
from jax import lax
from jax.experimental import pallas as pl
from jax.experimental.pallas import tpu as pltpu

F32 = jnp.float32
BF16 = jnp.bfloat16

HEAD_DIM = 64
GRID_W = 64
ROPE_BASE = 10000.0
NORM_EPS = 1e-6
NEG_INF = -1e30
WIN_KV_HEADS = 4
WIN_BLOCK = 128
POOL_WINDOWS = (2, 4, 8, 16)
POOL_HALO = 8
GN_EPS = 64e-5
N_GROUPS = 4
EXPERTS_PER_GROUP = 8
N_EXPERTS = N_GROUPS * EXPERTS_PER_GROUP
MOE_BLOCK = 256
DIFF_Q_ROWS = 1024
DIFF_Q_SUB = 256
RWKV_CHUNK = 64
RWKV_HEADS_PER_STEP = 16
LANES = 128
V7X_VMEM_LIMIT = 48 * 1024 * 1024
HI = lax.Precision.HIGHEST


def _cp(sem, vmem=V7X_VMEM_LIMIT):
    return pltpu.CompilerParams(dimension_semantics=sem, vmem_limit_bytes=vmem)


def _bf(x):
    return x.astype(BF16)


def _dot(a, b, precision=None):
    return jnp.dot(a, b, preferred_element_type=F32, precision=precision)


def _dot_t(a, b, precision=None):
    return lax.dot_general(a, b, (((1,), (1,)), ((), ())), preferred_element_type=F32, precision=precision)


def _dot_l(a, b, precision=None):
    return lax.dot_general(a, b, (((0,), (0,)), ((), ())), preferred_element_type=F32, precision=precision)


def _mm(a, b):
    return _dot(_bf(a), _bf(b))


def _mm_t(a, b):
    return _dot_t(_bf(a), _bf(b))


def _mm_l(a, b):
    return _dot_l(_bf(a), _bf(b))


def _cumulate(tri, x):
    hi = _bf(x)
    r1 = x - hi.astype(F32)
    mid = _bf(r1)
    lo = _bf(r1 - mid.astype(F32))
    return _dot(tri, hi) + _dot(tri, mid) + _dot(tri, lo)


def _norm_mod(x, g, scale, shift):
    ms = jnp.mean(x * x, axis=-1, keepdims=True)
    y = x * lax.rsqrt(ms + NORM_EPS) * g
    return y * (1.0 + scale) + shift


def _sigmoid(x):
    return 1.0 / (1.0 + jnp.exp(-x))


def _row_tile(n, pref):
    t = min(pref, n)
    assert n % t == 0
    return t


def _mod_body(c_ref, w_ref, b_ref, o_ref):
    c = c_ref[...]
    s = c * _sigmoid(c)
    o_ref[0] = _dot(_bf(s), _bf(w_ref[0])) + b_ref[0]


def _mod_all(c_all, mod_w, mod_b):
    depth, d, n = mod_w.shape
    r = c_all.shape[0]
    tn = 1536
    return pl.pallas_call(
        _mod_body,
        grid=(depth, n // tn),
        in_specs=[
            pl.BlockSpec((r, d), lambda i, j: (0, 0)),
            pl.BlockSpec((1, d, tn), lambda i, j: (i, 0, j)),
            pl.BlockSpec((1, 1, tn), lambda i, j: (i, 0, j)),
        ],
        out_specs=pl.BlockSpec((1, r, tn), lambda i, j: (i, 0, j)),
        out_shape=jax.ShapeDtypeStruct((depth, r, n), F32),
        compiler_params=_cp(("parallel", "parallel")),
        name="adaln_mod",
    )(c_all, mod_w, mod_b.reshape(depth, 1, n))


def _mod_spec(mod):
    if mod.shape[0] == 1:
        return pl.BlockSpec((1,) + mod.shape[1:], lambda b, i: (0, 0, 0))
    return pl.BlockSpec((1,) + mod.shape[1:], lambda b, i: (b, 0, 0))


def _rope_tile(y, cos, sin, first_half):
    fwd = pltpu.roll(y, LANES - 16, 1)
    bwd = pltpu.roll(y, 16, 1)
    rot = jnp.where(first_half, -fwd, bwd)
    return y * cos + rot * sin


def _proj_body(*refs, n_rope, n_q, scale_i, shift_i):
    if n_rope:
        x_ref, g_ref, m_ref, w_ref, cos_ref, sin_ref, o_ref = refs
    else:
        x_ref, g_ref, m_ref, w_ref, o_ref = refs
    m = m_ref[0]
    h = _norm_mod(x_ref[0], g_ref[...], m[scale_i:scale_i + 1], m[shift_i:shift_i + 1])
    y = _dot(_bf(h), w_ref[...])
    n = y.shape[1]
    if n_rope:
        cos, sin = cos_ref[...], sin_ref[...]
        first_half = (lax.broadcasted_iota(jnp.int32, cos.shape, 1) & 31) < 16
    for j in range(n // LANES):
        blk = y[:, j * LANES:(j + 1) * LANES]
        if j * LANES < n_q:
            blk = blk * (HEAD_DIM ** -0.5)
        if j * LANES < n_rope:
            blk = _rope_tile(blk, cos, sin, first_half)
        o_ref[0, :, j * LANES:(j + 1) * LANES] = _bf(blk)


def _proj(x, g, mod, w, cos, sin, *, n_rope, n_q, scale_i, shift_i, name):
    b, l, d = x.shape
    n = w.shape[1]
    tm = _row_tile(l, 256)
    in_specs = [
        pl.BlockSpec((1, tm, d), lambda bi, i: (bi, i, 0)),
        pl.BlockSpec((1, d), lambda bi, i: (0, 0)),
        _mod_spec(mod),
        pl.BlockSpec((d, n), lambda bi, i: (0, 0)),
    ]
    args = [x, g.reshape(1, d), mod, w]
    if n_rope:
        in_specs += [pl.BlockSpec((tm, LANES), lambda bi, i: (i, 0))] * 2
        args += [cos, sin]
    return pl.pallas_call(
        functools.partial(_proj_body, n_rope=n_rope, n_q=n_q, scale_i=scale_i, shift_i=shift_i),
        grid=(b, l // tm),
        in_specs=in_specs,
        out_specs=pl.BlockSpec((1, tm, n), lambda bi, i: (bi, i, 0)),
        out_shape=jax.ShapeDtypeStruct((b, l, n), BF16),
        compiler_params=_cp(("parallel", "parallel")),
        name=name,
    )(*args)


def _res_body(a_ref, w_ref, x_ref, m_ref, o_ref, *, gate_i):
    y = _dot(a_ref[0], w_ref[...])
    o_ref[0] = x_ref[0] + m_ref[0][gate_i:gate_i + 1] * y


def _res_proj(a, w, x, mod, *, gate_i, name):
    b, l, d = x.shape
    k = a.shape[-1]
    tm = _row_tile(l, 512)
    in_specs = [
        pl.BlockSpec((1, tm, k), lambda bi, i: (bi, i, 0)),
        pl.BlockSpec((k, d), lambda bi, i: (0, 0)),
        pl.BlockSpec((1, tm, d), lambda bi, i: (bi, i, 0)),
        _mod_spec(mod),
    ]
    args = [a, w, x, mod]
    return pl.pallas_call(
        functools.partial(_res_body, gate_i=gate_i),
        grid=(b, l // tm),
        in_specs=in_specs,
        out_specs=pl.BlockSpec((1, tm, d), lambda bi, i: (bi, i, 0)),
        out_shape=jax.ShapeDtypeStruct((b, l, d), F32),
        input_output_aliases={2: 0},
        compiler_params=_cp(("parallel", "parallel")),
        name=name,
    )(*args)


def _win_body(*refs, seq, local, n_heads):
    if local:
        q_ref, kp_ref, kc_ref, kn_ref, vp_ref, vc_ref, vn_ref, kx_ref, vx_ref, sink_ref, o_ref = refs
    else:
        q_ref, kx_ref, vx_ref, sink_ref, o_ref = refs
    tq = q_ref.shape[1]
    grp = n_heads // WIN_KV_HEADS
    n = pl.program_id(1)
    if local:
        span = 3 * tq
        row = lax.broadcasted_iota(jnp.int32, (grp * tq, span), 0) & (tq - 1)
        col = lax.broadcasted_iota(jnp.int32, (grp * tq, span), 1)
        rel = col - tq - row
        key_pos = n * tq - tq + col
        mask = (jnp.abs(rel) <= tq) & (key_pos >= 0) & (key_pos < seq)
    hks = range(WIN_KV_HEADS)
    ks = [slice(hk * HEAD_DIM, (hk + 1) * HEAD_DIM) for hk in hks]
    qh = [jnp.concatenate(
        [q_ref[0, :, (hk * grp + g) * HEAD_DIM:(hk * grp + g + 1) * HEAD_DIM] for g in range(grp)], axis=0)
        for hk in hks]
    sink = [jnp.concatenate([jnp.full((tq, 1), sink_ref[hk * grp + g], F32) for g in range(grp)], axis=0)
            for hk in hks]
    s_ctx = [_dot_t(qh[hk], kx_ref[0, :, ks[hk]]) for hk in hks]
    m = [jnp.maximum(jnp.max(s_ctx[hk], axis=1, keepdims=True), sink[hk]) for hk in hks]
    if local:
        s_loc = [jnp.where(mask, _dot_t(qh[hk], jnp.concatenate(
            [kp_ref[0, :, ks[hk]], kc_ref[0, :, ks[hk]], kn_ref[0, :, ks[hk]]], axis=0)), NEG_INF) for hk in hks]
        m = [jnp.maximum(m[hk], jnp.max(s_loc[hk], axis=1, keepdims=True)) for hk in hks]
    p_ctx = [jnp.exp(s_ctx[hk] - m[hk]) for hk in hks]
    den = [jnp.sum(p_ctx[hk], axis=1, keepdims=True) + jnp.exp(sink[hk] - m[hk]) for hk in hks]
    o = [_dot(_bf(p_ctx[hk]), vx_ref[0, :, ks[hk]]) for hk in hks]
    if local:
        p_loc = [jnp.exp(s_loc[hk] - m[hk]) for hk in hks]
        den = [den[hk] + jnp.sum(p_loc[hk], axis=1, keepdims=True) for hk in hks]
        o = [o[hk] + _dot(_bf(p_loc[hk]), jnp.concatenate(
            [vp_ref[0, :, ks[hk]], vc_ref[0, :, ks[hk]], vn_ref[0, :, ks[hk]]], axis=0)) for hk in hks]
    for hk in hks:
        oh = o[hk] / den[hk]
        for g in range(grp):
            hq = hk * grp + g
            o_ref[0, :, hq * HEAD_DIM:(hq + 1) * HEAD_DIM] = _bf(oh[g * tq:(g + 1) * tq])


def _win_attn(qkv, qkvc, sink, *, local, name):
    src = qkv if local else qkvc
    b, l, _ = src.shape
    c = qkvc.shape[1]
    n_heads = sink.shape[0]
    d = n_heads * HEAD_DIM
    kvw = WIN_KV_HEADS * HEAD_DIM
    kcol = d // kvw
    tq = WIN_BLOCK
    nb = l // tq
    in_specs = [pl.BlockSpec((1, tq, d), lambda bi, i: (bi, i, 0))]
    args = [src]
    if local:
        for colb in (kcol, kcol + 1):
            in_specs += [
                pl.BlockSpec((1, tq, kvw), lambda bi, i, colb=colb: (bi, jnp.maximum(i - 1, 0), colb)),
                pl.BlockSpec((1, tq, kvw), lambda bi, i, colb=colb: (bi, i, colb)),
                pl.BlockSpec((1, tq, kvw), lambda bi, i, colb=colb: (bi, jnp.minimum(i + 1, nb - 1), colb)),
            ]
            args += [qkv, qkv, qkv]
    in_specs += [
        pl.BlockSpec((1, c, kvw), lambda bi, i: (bi, 0, kcol)),
        pl.BlockSpec((1, c, kvw), lambda bi, i: (bi, 0, kcol + 1)),
        pl.BlockSpec(memory_space=pltpu.SMEM),
    ]
    args += [qkvc, qkvc, sink]
    return pl.pallas_call(
        functools.partial(_win_body, seq=l, local=local, n_heads=n_heads),
        grid=(b, nb),
        in_specs=in_specs,
        out_specs=pl.BlockSpec((1, tq, d), lambda bi, i: (bi, i, 0)),
        out_shape=jax.ShapeDtypeStruct((b, l, d), BF16),
        compiler_params=_cp(("parallel", "parallel")),
        name=name,
    )(*args)


def _diff_body(*refs, n_lat, tk, sub, lambda_init):
    if n_lat:
        lam_ref, g_ref, q_ref, kl_ref, vl_ref, kx_ref, vx_ref, o_ref, s_ref = refs
    else:
        lam_ref, g_ref, q_ref, kx_ref, vx_ref, o_ref, s_ref = refs
    hw = 2 * HEAD_DIM
    n_sub = q_ref.shape[1] // sub
    lane = lax.broadcasted_iota(jnp.int32, (sub, hw), 1)
    chunks = [(kl_ref, vl_ref, i * tk, tk, i * tk) for i in range(n_lat)] if n_lat else []
    chunks.append((kx_ref, vx_ref, 0, kx_ref.shape[1], n_lat * tk))
    lam = lam_ref[...]
    lam_full = (jnp.exp(jnp.sum(lam[0:1] * lam[1:2], axis=1, keepdims=True))
                - jnp.exp(jnp.sum(lam[2:3] * lam[3:4], axis=1, keepdims=True)) + lambda_init)

    def stacked_q(j):
        q = q_ref[0, j * sub:(j + 1) * sub, :]
        zero = jnp.zeros_like(q)
        return jnp.concatenate([jnp.where(lane < HEAD_DIM, q, zero), jnp.where(lane >= HEAD_DIM, q, zero)], axis=0)

    def score_chunk(j, qq, ch, mx):
        k_ref, _, row, n, col = ch
        s = _dot_t(qq, k_ref[0, row:row + n, :])
        s_ref[j % 2, :, col:col + n] = s
        for t in range(n // LANES):
            mx = jnp.maximum(mx, s[:, t * LANES:(t + 1) * LANES])
        return mx

    def value_chunk(j, m, ch, acc):
        _, v_ref, row, n, col = ch
        p = jnp.concatenate(
            [jnp.exp(s_ref[j % 2, :, col + t * LANES:col + (t + 1) * LANES] - m) for t in range(n // LANES)], axis=1)
        v = v_ref[0, row:row + n, :]
        return acc + _dot(_bf(p), jnp.concatenate([v, jnp.ones_like(v)], axis=1))

    def finish(j, acc):
        a = acc[:, :hw] / acc[:, hw:]
        o = a[:sub] - lam_full * a[sub:]
        ms = jnp.mean(o * o, axis=-1, keepdims=True)
        o = o * lax.rsqrt(ms + NORM_EPS) * g_ref[...] * (1.0 - lambda_init)
        o_ref[0, j * sub:(j + 1) * sub, :] = _bf(o)

    m_prev = None
    for j in range(n_sub + 1):
        if j < n_sub:
            qq = stacked_q(j)
            mx = jnp.full((2 * sub, LANES), NEG_INF, F32)
        if j > 0:
            acc = jnp.zeros((2 * sub, 2 * hw), F32)
        for ch in chunks:
            if j > 0:
                acc = value_chunk(j - 1, m_prev, ch, acc)
            if j < n_sub:
                mx = score_chunk(j, qq, ch, mx)
        if j > 0:
            finish(j - 1, acc)
        if j < n_sub:
            m_prev = jnp.broadcast_to(jnp.max(mx, axis=1, keepdims=True), (2 * sub, LANES))


def _diff_attn(qkv, qkvc, lam, subln_g, *, local, lambda_init, name):
    src = qkv if local else qkvc
    b, l, n3 = src.shape
    d = n3 // 3
    c = qkvc.shape[1]
    hw = 2 * HEAD_DIM
    nh = d // hw
    tq = _row_tile(l, DIFF_Q_ROWS)
    sub = min(DIFF_Q_SUB, tq)
    tk = 512
    in_specs = [
        pl.BlockSpec((4, HEAD_DIM), lambda bi, h, i: (0, 0)),
        pl.BlockSpec((1, hw), lambda bi, h, i: (0, 0)),
        pl.BlockSpec((1, tq, hw), lambda bi, h, i: (bi, i, h)),
    ]
    args = [lam, subln_g.reshape(1, hw), src]
    n_lat = 0
    if local:
        s = qkv.shape[1]
        n_lat = s // tk
        in_specs += [
            pl.BlockSpec((1, s, hw), lambda bi, h, i: (bi, 0, nh + h)),
            pl.BlockSpec((1, s, hw), lambda bi, h, i: (bi, 0, 2 * nh + h)),
        ]
        args += [qkv, qkv]
    in_specs += [
        pl.BlockSpec((1, c, hw), lambda bi, h, i: (bi, 0, nh + h)),
        pl.BlockSpec((1, c, hw), lambda bi, h, i: (bi, 0, 2 * nh + h)),
    ]
    args += [qkvc, qkvc]
    return pl.pallas_call(
        functools.partial(_diff_body, n_lat=n_lat, tk=tk, sub=sub, lambda_init=lambda_init),
        grid=(b, nh, l // tq),
        in_specs=in_specs,
        out_specs=pl.BlockSpec((1, tq, hw), lambda bi, h, i: (bi, i, h)),
        out_shape=jax.ShapeDtypeStruct((b, l, d), BF16),
        scratch_shapes=[pltpu.VMEM((2, 2 * sub, n_lat * tk + c), F32)],
        compiler_params=_cp(("parallel", "parallel", "parallel")),
        name=name,
    )(*args)


def _halo_specs(l, tm, d):
    nh = l // POOL_HALO
    per = tm // POOL_HALO
    return [
        pl.BlockSpec((1, POOL_HALO, d), lambda bi, i: (bi, jnp.maximum(i * per - 1, 0), 0)),
        pl.BlockSpec((1, tm, d), lambda bi, i: (bi, i, 0)),
        pl.BlockSpec((1, POOL_HALO, d), lambda bi, i: (bi, jnp.minimum((i + 1) * per, nh - 1), 0)),
    ]


def _fill_normed(h_ref, xp_ref, x_ref, xn_ref, g, scale, shift):
    tm = x_ref.shape[1]
    h_ref[0:POOL_HALO, :] = _norm_mod(xp_ref[0], g, scale, shift)
    h_ref[POOL_HALO:POOL_HALO + tm, :] = _norm_mod(x_ref[0], g, scale, shift)
    h_ref[POOL_HALO + tm:2 * POOL_HALO + tm, :] = _norm_mod(xn_ref[0], g, scale, shift)


def _pool_body(xp_ref, x_ref, xn_ref, g_ref, m_ref, w_ref, b_ref, s_ref, o_ref, h_ref, *, seq):
    tm = x_ref.shape[1]
    d = x_ref.shape[2]
    gd = d // len(POOL_WINDOWS)
    m = m_ref[0]
    _fill_normed(h_ref, xp_ref, x_ref, xn_ref, g_ref[...], m[1:2], m[0:1])
    pos = pl.program_id(1) * tm + lax.broadcasted_iota(jnp.int32, (tm, 1), 0)
    x = x_ref[0]
    for gi, w in enumerate(POOL_WINDOWS):
        cols = slice(gi * gd, (gi + 1) * gd)
        acc = jnp.zeros((tm, gd), F32)
        for off in range(-(w // 2), w - w // 2):
            valid = (pos + off >= 0) & (pos + off < seq)
            acc = acc + jnp.where(valid, h_ref[POOL_HALO + off:POOL_HALO + off + tm, cols], 0.0)
        lo = jnp.maximum(pos - w // 2, 0)
        hi = jnp.minimum(pos + w - w // 2, seq)
        y = acc / (hi - lo).astype(F32) - h_ref[POOL_HALO:POOL_HALO + tm, cols]
        y = (_dot(_bf(y), w_ref[gi]) + b_ref[:, cols]) * s_ref[:, cols]
        o_ref[0, :, cols] = x[:, cols] + m[2:3, cols] * y


def _pool_mix(x, g, mod, w_group, b_group, layer_scale, *, name):
    b, l, d = x.shape
    tm = _row_tile(l, 256)
    gd = d // len(POOL_WINDOWS)
    return pl.pallas_call(
        functools.partial(_pool_body, seq=l),
        grid=(b, l // tm),
        in_specs=_halo_specs(l, tm, d) + [
            pl.BlockSpec((1, d), lambda bi, i: (0, 0)),
            _mod_spec(mod),
            pl.BlockSpec((len(POOL_WINDOWS), gd, gd), lambda bi, i: (0, 0, 0)),
            pl.BlockSpec((1, d), lambda bi, i: (0, 0)),
            pl.BlockSpec((1, d), lambda bi, i: (0, 0)),
        ],
        out_specs=pl.BlockSpec((1, tm, d), lambda bi, i: (bi, i, 0)),
        out_shape=jax.ShapeDtypeStruct((b, l, d), F32),
        scratch_shapes=[pltpu.VMEM((tm + 2 * POOL_HALO, d), F32)],
        compiler_params=_cp(("parallel", "parallel")),
        name=name,
    )(x, x, x, g.reshape(1, d), mod, _bf(w_group), b_group.reshape(1, d), layer_scale.reshape(1, d))


def _softplus(z):
    return jnp.maximum(z, 0.0) + jnp.log1p(jnp.exp(-jnp.abs(z)))


def _rwkv_feat_body(*refs, seq, with_out):
    (xp_ref, x_ref, xn_ref, g_ref, m_ref, mu_ref, wrkv_ref, w0_ref, wa1_ref, wa2_ref, a0_ref, aa1_ref,
     aa2_ref, g1_ref, g2_ref, kk_ref, ka_ref) = refs[:17]
    if with_out:
        r_out, v_out, kk_out, lw_out, b_out, kd_out, g_out, h_ref = refs[17:]
    else:
        v_out, kk_out, lw_out, b_out, kd_out, h_ref = refs[17:]
    tm = x_ref.shape[1]
    d = x_ref.shape[2]
    nh = d // HEAD_DIM
    m = m_ref[0]
    _fill_normed(h_ref, xp_ref, x_ref, xn_ref, g_ref[...], m[1:2], m[0:1])
    pos = pl.program_id(1) * tm + lax.broadcasted_iota(jnp.int32, (tm, 1), 0)
    t = h_ref[POOL_HALO:POOL_HALO + tm, :]
    dp = jnp.where(pos >= 1, h_ref[POOL_HALO - 1:POOL_HALO - 1 + tm, :], 0.0) - t
    dn = jnp.where(pos < seq - 1, h_ref[POOL_HALO + 1:POOL_HALO + 1 + tm, :], 0.0) - t

    def mix(i):
        return _bf(t + dp * mu_ref[0, i:i + 1, :] + dn * mu_ref[1, i:i + 1, :])

    def put(ref, val, lead=()):
        for h in range(nh):
            ref[lead + (0, h)] = val[:, h * HEAD_DIM:(h + 1) * HEAD_DIM]

    k = _dot(mix(2), wrkv_ref[1])
    put(v_out, _dot(mix(3), wrkv_ref[2]))
    kk = k * kk_ref[...]
    kk_heads = []
    for h in range(nh):
        kh = kk[:, h * HEAD_DIM:(h + 1) * HEAD_DIM]
        nrm = jnp.sqrt(jnp.sum(kh * kh, axis=-1, keepdims=True))
        kk_heads.append(kh / jnp.maximum(nrm, 1e-12))
        kk_out[0, h] = kk_heads[h]
    xw, xa = mix(1), mix(4)
    for di in range(2):
        z = w0_ref[di:di + 1, :] + _dot(_bf(jnp.tanh(_dot(xw, wa1_ref[di]))), wa2_ref[di])
        w = -_softplus(-z) - 0.5
        put(lw_out, -jnp.exp(w), (di,))
        a = _sigmoid(a0_ref[di:di + 1, :] + _dot(_bf(_dot(xa, aa1_ref[di])), aa2_ref[di]))
        put(kd_out, k * (1.0 + (a - 1.0) * ka_ref[...]), (di,))
        for h in range(nh):
            cols = slice(h * HEAD_DIM, (h + 1) * HEAD_DIM)
            b_out[di, 0, h] = kk_heads[h] * a[:, cols]
    if with_out:
        put(r_out, _dot(mix(0), wrkv_ref[0]))
        g_out[0] = _dot(_bf(_sigmoid(_dot(mix(5), g1_ref[...]))), g2_ref[...])


def _rwkv_features(x, g, mod, p, *, with_out, name):
    b, l, d = x.shape
    nh = d // HEAD_DIM
    tm = _row_tile(l, 128)
    const = lambda a: pl.BlockSpec(a.shape, lambda bi, i, nd=a.ndim: (0,) * nd)
    weights = [p["mu"], _bf(p["w_rkv"]), p["w0"], _bf(p["w_a1"]), _bf(p["w_a2"]), p["a0"], _bf(p["a_a1"]),
               _bf(p["a_a2"]), _bf(p["g1"]), _bf(p["g2"]), p["k_k"].reshape(1, d), p["k_a"].reshape(1, d)]
    head = pl.BlockSpec((1, nh, tm, HEAD_DIM), lambda bi, i: (bi, 0, i, 0))
    head2 = pl.BlockSpec((2, 1, nh, tm, HEAD_DIM), lambda bi, i: (0, bi, 0, i, 0))
    hs = jax.ShapeDtypeStruct((b, nh, l, HEAD_DIM), F32)
    hs2 = jax.ShapeDtypeStruct((2, b, nh, l, HEAD_DIM), F32)
    out_specs = [head, head, head2, head2, head2]
    out_shape = [hs, hs, hs2, hs2, hs2]
    if with_out:
        out_specs = [head] + out_specs + [pl.BlockSpec((1, tm, d), lambda bi, i: (bi, i, 0))]
        out_shape = [hs] + out_shape + [jax.ShapeDtypeStruct((b, l, d), F32)]
    return pl.pallas_call(
        functools.partial(_rwkv_feat_body, seq=l, with_out=with_out),
        grid=(b, l // tm),
        in_specs=_halo_specs(l, tm, d) + [pl.BlockSpec((1, d), lambda bi, i: (0, 0)), _mod_spec(mod)]
        + [const(a) for a in weights],
        out_specs=out_specs,
        out_shape=out_shape,
        scratch_shapes=[pltpu.VMEM((tm + 2 * POOL_HALO, d), F32)],
        compiler_params=_cp(("parallel", "parallel")),
        name=name,
    )(x, x, x, g.reshape(1, d), mod, *weights)


def _scan_body(*refs, emit, n_chunks):
    if emit:
        kk_ref, v_ref, r_ref, lw_ref, b_ref, kd_ref, s0_ref, o_ref, sfin_ref, s_ref = refs
    else:
        kk_ref, v_ref, lw_ref, b_ref, kd_ref, s0_ref, sfin_ref, s_ref = refs
    hb = kk_ref.shape[1]
    L = kk_ref.shape[2]
    rev = pl.program_id(0) == 1
    c = pl.program_id(3)

    @pl.when(c == 0)
    def _():
        s_ref[...] = s0_ref[0, 0]

    row = lax.broadcasted_iota(jnp.int32, (L, L), 0)
    col = lax.broadcasted_iota(jnp.int32, (L, L), 1)
    flip = rev.astype(jnp.int32)
    p_row = row + flip * (L - 1 - 2 * row)
    p_col = col + flip * (L - 1 - 2 * col)
    incl = p_col <= p_row
    strict = p_col < p_row
    tri = _bf(incl.astype(F32))
    eye = row == col
    assert L == HEAD_DIM
    levels = []
    m = 1
    while m < L:
        same = (p_row // (2 * m)) == (p_col // (2 * m))
        levels.append(same & ((p_row & (2 * m - 1)) >= m) & ((p_col & (2 * m - 1)) < m))
        m *= 2

    hs = range(hb)
    lw = [lw_ref[0, 0, h] for h in hs]
    v = [v_ref[0, h] for h in hs]
    b = [b_ref[0, 0, h] for h in hs]
    kd = [kd_ref[0, 0, h] for h in hs]
    s0 = [s_ref[h] for h in hs]
    cl = [_cumulate(tri, lw[h]) for h in hs]
    cl_tot = [jnp.sum(lw[h], axis=0, keepdims=True) for h in hs]
    w_inv = [jnp.exp(-cl[h]) for h in hs]
    kh = [kk_ref[0, h] * jnp.exp(cl[h] - lw[h]) for h in hs]
    rhs = [jnp.concatenate([b[h] * w_inv[h], kd[h] * w_inv[h]], axis=0) for h in hs]
    if emit:
        rh = [r_ref[0, h] * jnp.exp(cl[h]) for h in hs]
        lhs = [jnp.concatenate([kh[h], rh[h]], axis=0) for h in hs]
    else:
        lhs = kh
    big = [_mm_t(lhs[h], rhs[h]) for h in hs]
    m_b = [big[h][:L, :L] for h in hs]
    mkv = [_mm(jnp.where(strict, big[h][:L, L:], 0.0), v[h]) for h in hs]
    t = [jnp.where(eye, 1.0, 0.0) - jnp.where(levels[0], m_b[h], 0.0) for h in hs]
    for lm in levels[1:]:
        tl = [_mm(t[h], jnp.where(lm, m_b[h], 0.0)) for h in hs]
        t = [t[h] - _mm(tl[h], t[h]) for h in hs]
    y = [_mm(t[h], jnp.concatenate([kh[h], mkv[h]], axis=1)) for h in hs]
    w_last = [jnp.exp(cl_tot[h] - cl[h]) for h in hs]
    gh = [_mm_l(y[h], b[h] * w_last[h]) for h in hs]
    vk = [_mm_l(v[h], kd[h] * w_last[h]) for h in hs]
    g_mat = [jnp.where(eye, jnp.exp(cl_tot[h]), 0.0) - gh[h][:HEAD_DIM] for h in hs]
    if emit:
        ab_y = [_mm(jnp.where(incl, big[h][L:, :L], 0.0), y[h]) for h in hs]
        akv = [_mm(jnp.where(incl, big[h][L:, L:], 0.0), v[h]) for h in hs]
        rs = [_mm_t(rh[h] - ab_y[h][:, :HEAD_DIM], s0[h]) for h in hs]
        for h in hs:
            o_ref[0, 0, h] = rs[h] + akv[h] - ab_y[h][:, HEAD_DIM:]
    sg = [_mm(s0[h], g_mat[h]) for h in hs]
    for h in hs:
        s_ref[h] = sg[h] + vk[h] - gh[h][HEAD_DIM:]

    @pl.when(c == n_chunks - 1)
    def _():
        sfin_ref[0, 0] = s_ref[...]


def _rwkv_scan(kk, v, r, lw, b, kd, s0, *, name):
    bsz, nh, l, dh = kk.shape
    L = RWKV_CHUNK
    hb = RWKV_HEADS_PER_STEP
    nch = l // L
    emit = r is not None

    def chunk(d, c):
        return c + d * (nch - 1 - 2 * c)

    one = pl.BlockSpec((1, hb, L, dh), lambda d, bi, hg, c: (bi, hg, chunk(d, c), 0))
    two = pl.BlockSpec((1, 1, hb, L, dh), lambda d, bi, hg, c: (d, bi, hg, chunk(d, c), 0))
    st = pl.BlockSpec((1, 1, hb, dh, dh), lambda d, bi, hg, c: (d, bi, hg, 0, 0))
    in_specs = [one, one] + ([one] if emit else []) + [two, two, two, st]
    args = [kk, v] + ([r] if emit else []) + [lw, b, kd, s0]
    s_shape = jax.ShapeDtypeStruct((2, bsz, nh, dh, dh), F32)
    if emit:
        out_specs = [two, st]
        out_shape = [jax.ShapeDtypeStruct((2, bsz, nh, l, dh), F32), s_shape]
    else:
        out_specs = [st]
        out_shape = [s_shape]
    return pl.pallas_call(
        functools.partial(_scan_body, emit=emit, n_chunks=nch),
        grid=(2, bsz, nh // hb, nch),
        in_specs=in_specs,
        out_specs=out_specs,
        out_shape=out_shape,
        scratch_shapes=[pltpu.VMEM((hb, dh, dh), F32)],
        compiler_params=_cp(("parallel", "parallel", "parallel", "arbitrary")),
        name=name,
    )(*args)


def _rwkv_out_body(o_ref, r_ref, v_ref, kd_ref, g_ref, rk_ref, lnw_ref, lnb_ref, w_ref, x_ref, m_ref, out_ref):
    nh = r_ref.shape[1]
    tm = r_ref.shape[2]
    acc = jnp.zeros((tm, x_ref.shape[2]), F32)
    for h in range(nh):
        cols = slice(h * HEAD_DIM, (h + 1) * HEAD_DIM)
        o = o_ref[0, 0, h] + o_ref[1, 0, h]
        mean = jnp.mean(o, axis=-1, keepdims=True)
        var = jnp.mean(jnp.square(o - mean), axis=-1, keepdims=True)
        on = (o - mean) * lax.rsqrt(var + GN_EPS) * lnw_ref[h:h + 1, :] + lnb_ref[h:h + 1, :]
        r = r_ref[0, h]
        rk = rk_ref[h:h + 1, :]
        bonus = (jnp.sum(r * kd_ref[0, 0, h] * rk, axis=-1, keepdims=True) * v_ref[0, h]
                 + jnp.sum(r * kd_ref[1, 0, h] * rk, axis=-1, keepdims=True) * v_ref[0, h])
        a = (on + bonus) * g_ref[0, :, cols]
        acc = acc + _dot(_bf(a), w_ref[cols, :])
    out_ref[0] = x_ref[0] + m_ref[0][2:3] * acc


def _rwkv_output(o, r, v, kd, g, r_k, ln_w, ln_b, w_o, x, mod, *, name):
    b, l, d = x.shape
    nh = d // HEAD_DIM
    tm = _row_tile(l, 256)
    head = pl.BlockSpec((1, nh, tm, HEAD_DIM), lambda bi, i: (bi, 0, i, 0))
    head2 = pl.BlockSpec((2, 1, nh, tm, HEAD_DIM), lambda bi, i: (0, bi, 0, i, 0))
    row = pl.BlockSpec((1, tm, d), lambda bi, i: (bi, i, 0))
    small = pl.BlockSpec((nh, HEAD_DIM), lambda bi, i: (0, 0))
    return pl.pallas_call(
        _rwkv_out_body,
        grid=(b, l // tm),
        in_specs=[head2, head, head, head2, row, small, small, small,
                  pl.BlockSpec((d, d), lambda bi, i: (0, 0)), row, _mod_spec(mod)],
        out_specs=row,
        out_shape=jax.ShapeDtypeStruct((b, l, d), F32),
        input_output_aliases={9: 0},
        compiler_params=_cp(("parallel", "parallel")),
        name=name,
    )(o, r, v, kd, g, r_k, ln_w.reshape(nh, HEAD_DIM), ln_b.reshape(nh, HEAD_DIM), _bf(w_o), x, mod)


def _router_body(x_ref, g_ref, m_ref, w_ref, b_ref, h_ref, sel_ref):
    m = m_ref[0]
    h = _norm_mod(x_ref[0], g_ref[...], m[4:5], m[3:4])
    h_ref[0] = _bf(h)
    lg = _dot(h, w_ref[...], HI) + b_ref[...]
    lane = lax.broadcasted_iota(jnp.int32, lg.shape, 1)
    big = jnp.int32(1 << 20)
    g_logit = jnp.where(lane < N_GROUPS, lg, -jnp.inf)
    g_max = jnp.max(g_logit, axis=1, keepdims=True)
    gsel = jnp.min(jnp.where(g_logit == g_max, lane, big), axis=1, keepdims=True)
    p_grp = 1.0 / jnp.sum(jnp.exp(g_logit - g_max), axis=1, keepdims=True)
    lo = N_GROUPS + gsel * EXPERTS_PER_GROUP
    e_logit = jnp.where((lane >= lo) & (lane < lo + EXPERTS_PER_GROUP), lg, -jnp.inf)
    e_max = jnp.max(e_logit, axis=1, keepdims=True)
    i1 = jnp.min(jnp.where(e_logit == e_max, lane, big), axis=1, keepdims=True)
    rest = jnp.where(lane == i1, -jnp.inf, e_logit)
    e2 = jnp.max(rest, axis=1, keepdims=True)
    i2 = jnp.min(jnp.where(rest == e2, lane, big), axis=1, keepdims=True)
    q2 = jnp.exp(e2 - e_max)
    w1 = p_grp / (1.0 + q2)
    w2 = p_grp * q2 / (1.0 + q2)
    sel = jnp.where(lane == 0, (i1 - N_GROUPS).astype(F32),
                    jnp.where(lane == 1, (i2 - N_GROUPS).astype(F32),
                              jnp.where(lane == 2, w1, jnp.where(lane == 3, w2, 0.0))))
    sel_ref[0] = sel


def _router(x, g, mod, w_r, b_r, *, name):
    b, l, d = x.shape
    tm = _row_tile(l, 256)
    return pl.pallas_call(
        _router_body,
        grid=(b, l // tm),
        in_specs=[
            pl.BlockSpec((1, tm, d), lambda bi, i: (bi, i, 0)),
            pl.BlockSpec((1, d), lambda bi, i: (0, 0)),
            _mod_spec(mod),
            pl.BlockSpec((d, LANES), lambda bi, i: (0, 0)),
            pl.BlockSpec((1, LANES), lambda bi, i: (0, 0)),
        ],
        out_specs=[pl.BlockSpec((1, tm, d), lambda bi, i: (bi, i, 0)),
                   pl.BlockSpec((1, tm, LANES), lambda bi, i: (bi, i, 0))],
        out_shape=[jax.ShapeDtypeStruct((b, l, d), BF16), jax.ShapeDtypeStruct((b, l, LANES), F32)],
        compiler_params=_cp(("parallel", "parallel")),
        name=name,
    )(x, g.reshape(1, d), mod, w_r, b_r)


def _expert_body(be_ref, nu_ref, x_ref, wu_ref, wd_ref, o_ref):
    i = pl.program_id(0)
    ff = wd_ref.shape[1]

    @pl.when(i < nu_ref[0])
    def _():
        u = _dot(x_ref[...], wu_ref[0])
        gate = u[:, :ff]
        act = gate * _sigmoid(gate) * u[:, ff:]
        o_ref[...] = _dot(_bf(act), wd_ref[0])

    @pl.when(i >= nu_ref[0])
    def _():
        o_ref[...] = jnp.zeros_like(o_ref)


def _experts(xb, blk_expert, n_used, w_up, w_down):
    rows, d = xb.shape
    nb = rows // MOE_BLOCK
    ff2 = w_up.shape[2]
    ff = w_down.shape[1]
    return pl.pallas_call(
        _expert_body,
        grid_spec=pltpu.PrefetchScalarGridSpec(
            num_scalar_prefetch=2,
            grid=(nb,),
            in_specs=[
                pl.BlockSpec((MOE_BLOCK, d), lambda i, be, nu: (i, 0)),
                pl.BlockSpec((1, d, ff2), lambda i, be, nu: (be[i], 0, 0)),
                pl.BlockSpec((1, ff, d), lambda i, be, nu: (be[i], 0, 0)),
            ],
            out_specs=pl.BlockSpec((MOE_BLOCK, d), lambda i, be, nu: (i, 0)),
        ),
        out_shape=jax.ShapeDtypeStruct((rows, d), F32),
        compiler_params=_cp(("arbitrary",)),
        name="moe_experts",
    )(blk_expert, n_used, xb, w_up, w_down)


def _moe_res_body(x_ref, y0_ref, y1_ref, sel_ref, m_ref, o_ref):
    sel = sel_ref[0]
    y = y0_ref[0] * sel[:, 2:3] + y1_ref[0] * sel[:, 3:4]
    o_ref[0] = x_ref[0] + m_ref[0][5:6] * y


def _moe_res(x, y0, y1, sel, mod, *, name):
    b, l, d = x.shape
    tm = _row_tile(l, 512)
    row = pl.BlockSpec((1, tm, d), lambda bi, i: (bi, i, 0))
    return pl.pallas_call(
        _moe_res_body,
        grid=(b, l // tm),
        in_specs=[row, row, row, pl.BlockSpec((1, tm, LANES), lambda bi, i: (bi, i, 0)), _mod_spec(mod)],
        out_specs=row,
        out_shape=jax.ShapeDtypeStruct((b, l, d), F32),
        input_output_aliases={0: 0},
        compiler_params=_cp(("parallel", "parallel")),
        name=name,
    )(x, y0, y1, sel, mod)


def _dispatch_tables(eid, n_tok):
    a = n_tok * 2
    flat_e = eid.reshape(a)
    order = jnp.argsort(flat_e)
    se = flat_e[order]
    counts = jnp.sum((flat_e[:, None] == jnp.arange(N_EXPERTS, dtype=jnp.int32)[None, :]).astype(jnp.int32), axis=0)
    start = jnp.cumsum(counts) - counts
    nblk = (counts + MOE_BLOCK - 1) // MOE_BLOCK
    blk_end = jnp.cumsum(nblk)
    blk_start = blk_end - nblk
    dest = blk_start[se] * MOE_BLOCK + (jnp.arange(a, dtype=jnp.int32) - start[se])
    nb = -(-a // MOE_BLOCK) + N_EXPERTS
    blk_expert = jnp.minimum(jnp.searchsorted(blk_end, jnp.arange(nb, dtype=jnp.int32), side="right"),
                             N_EXPERTS - 1).astype(jnp.int32)
    rows = jnp.arange(nb * MOE_BLOCK, dtype=jnp.int32)
    e_row = blk_expert[rows // MOE_BLOCK]
    local = rows - blk_start[e_row] * MOE_BLOCK
    sorted_idx = jnp.clip(start[e_row] + local, 0, a - 1)
    src = jnp.where(local < counts[e_row], order[sorted_idx] // 2, 0)
    pos = jnp.zeros((a,), jnp.int32).at[order].set(dest).reshape(n_tok, 2)
    return src, pos, blk_expert, blk_end[-1:].astype(jnp.int32)


def _hier_moe(x, xc, mod, modc, g2, wg, bg, we, be, w_up, w_down, li):
    b, s, d = x.shape
    w_r = jnp.zeros((d, LANES), F32).at[:, :N_GROUPS].set(wg).at[:, N_GROUPS:N_GROUPS + N_EXPERTS].set(we)
    b_r = jnp.zeros((1, LANES), F32).at[0, :N_GROUPS].set(bg).at[0, N_GROUPS:N_GROUPS + N_EXPERTS].set(be)
    h, sel = _router(x, g2, mod, w_r, b_r, name=f"router{li}")
    h = h.reshape(b * s, d)
    sel_flat = sel.reshape(b * s, LANES)
    if xc is not None:
        c = xc.shape[1]
        hc, selc = _router(xc, g2, modc, w_r, b_r, name=f"router_ctx{li}")
        h = jnp.concatenate([h, hc.reshape(b * c, d)], axis=0)
        sel_flat = jnp.concatenate([sel_flat, selc.reshape(b * c, LANES)], axis=0)
    n_tok = h.shape[0]
    eid = sel_flat[:, 0:2].astype(jnp.int32)
    src, pos, blk_expert, n_used = _dispatch_tables(eid, n_tok)
    yb = _experts(h[src], blk_expert, n_used, _bf(w_up), _bf(w_down))
    y0 = yb[pos[:, 0]]
    y1 = yb[pos[:, 1]]
    n_lat = b * s
    x = _moe_res(x, y0[:n_lat].reshape(b, s, d), y1[:n_lat].reshape(b, s, d), sel, mod, name=f"moe_res{li}")
    if xc is not None:
        xc = _moe_res(xc, y0[n_lat:].reshape(b, c, d), y1[n_lat:].reshape(b, c, d), selc, modc,
                      name=f"moe_res_ctx{li}")
    return x, xc


def _final_body(x_ref, g_ref, o_ref):
    x = x_ref[0]
    ms = jnp.mean(x * x, axis=-1, keepdims=True)
    o_ref[0] = x * lax.rsqrt(ms + NORM_EPS) * g_ref[...]


def _final_norm(x, g):
    b, l, d = x.shape
    tm = _row_tile(l, 512)
    row = pl.BlockSpec((1, tm, d), lambda bi, i: (bi, i, 0))
    return pl.pallas_call(
        _final_body,
        grid=(b, l // tm),
        in_specs=[row, pl.BlockSpec((1, d), lambda bi, i: (0, 0))],
        out_specs=row,
        out_shape=jax.ShapeDtypeStruct((b, l, d), F32),
        compiler_params=_cp(("parallel", "parallel")),
        name="final_norm",
    )(x, g.reshape(1, d))


def _rope_tables(seq):
    rows = seq // GRID_W
    row = jnp.repeat(jnp.arange(rows), GRID_W).astype(F32)
    col = jnp.tile(jnp.arange(GRID_W), rows).astype(F32)
    nf = HEAD_DIM // 4
    inv = ROPE_BASE ** (-jnp.arange(nf, dtype=F32) / nf)
    ar, ac = row[:, None] * inv, col[:, None] * inv
    ang = jnp.concatenate([ar, ar, ac, ac] * (LANES // HEAD_DIM), axis=-1)
    return jnp.cos(ang), jnp.sin(ang)


def kernel(x, c, ctx, c_ctx, mod_w, mod_b, norm_g, final_g, win_w_qkv, win_sink, win_w_o, diff_w_qkv, diff_lambda, diff_subln_g, diff_w_o, pool_w_group, pool_b_group, pool_scale, rwkv_mu, rwkv_w_rkv, rwkv_w0, rwkv_w_a1, rwkv_w_a2, rwkv_a0, rwkv_a_a1, rwkv_a_a2, rwkv_g1, rwkv_g2, rwkv_k_k, rwkv_k_a, rwkv_r_k, rwkv_ln_w, rwkv_ln_b, rwkv_w_o, moe_wg, moe_bg, moe_we, moe_be, moe_w_up, moe_w_down):
    b, s, d = x.shape
    depth = mod_w.shape[0]
    n_mixers = 4
    cos, sin = _rope_tables(s)
    rows = -(-(b + 1) // 8) * 8
    c_all = jnp.zeros((rows, d), F32).at[:b].set(c).at[b].set(c_ctx)
    mods = _mod_all(c_all, mod_w, mod_b).reshape(depth, rows, 6, d)
    xc = ctx
    for i in range(depth):
        m, occ = i % n_mixers, i // n_mixers
        last = i == depth - 1
        mod = mods[i, :b]
        modc = mods[i, b:b + 1]
        g1 = norm_g[i, 0]
        if m == 0:
            nq = win_sink.shape[1] * HEAD_DIM
            nk = WIN_KV_HEADS * HEAD_DIM
            w = _bf(win_w_qkv[occ])
            qkv = _proj(x, g1, mod, w, cos, sin, n_rope=nq + nk, n_q=nq, scale_i=1, shift_i=0, name="win_qkv")
            qkvc = _proj(xc, g1, modc, w, None, None, n_rope=0, n_q=nq, scale_i=1, shift_i=0, name="win_qkv_ctx")
            o = _win_attn(qkv, qkvc, win_sink[occ], local=True, name="win_attn")
            wo = _bf(win_w_o[occ])
            x = _res_proj(o, wo, x, mod, gate_i=2, name="win_out")
            if not last:
                oc = _win_attn(None, qkvc, win_sink[occ], local=False, name="win_attn_ctx")
                xc = _res_proj(oc, wo, xc, modc, gate_i=2, name="win_out_ctx")
        elif m == 1:
            lambda_init = 0.8 - 0.6 * math.exp(-0.3 * i)
            w = _bf(diff_w_qkv[occ])
            qkv = _proj(x, g1, mod, w, cos, sin, n_rope=2 * d, n_q=d, scale_i=1, shift_i=0, name="diff_qkv")
            qkvc = _proj(xc, g1, modc, w, None, None, n_rope=0, n_q=d, scale_i=1, shift_i=0, name="diff_qkv_ctx")
            o = _diff_attn(qkv, qkvc, diff_lambda[occ], diff_subln_g[occ], local=True,
                           lambda_init=lambda_init, name="diff_attn")
            wo = _bf(diff_w_o[occ])
            x = _res_proj(o, wo, x, mod, gate_i=2, name="diff_out")
            if not last:
                oc = _diff_attn(None, qkvc, diff_lambda[occ], diff_subln_g[occ], local=False,
                                lambda_init=lambda_init, name="diff_attn_ctx")
                xc = _res_proj(oc, wo, xc, modc, gate_i=2, name="diff_out_ctx")
        elif m == 2:
            x = _pool_mix(x, g1, mod, pool_w_group[occ], pool_b_group[occ], pool_scale[occ], name="pool")
            if not last:
                xc = _pool_mix(xc, g1, modc, pool_w_group[occ], pool_b_group[occ], pool_scale[occ], name="pool_ctx")
        else:
            p = dict(mu=rwkv_mu[occ], w_rkv=rwkv_w_rkv[occ], w0=rwkv_w0[occ], w_a1=rwkv_w_a1[occ],
                     w_a2=rwkv_w_a2[occ], a0=rwkv_a0[occ], a_a1=rwkv_a_a1[occ], a_a2=rwkv_a_a2[occ],
                     g1=rwkv_g1[occ], g2=rwkv_g2[occ], k_k=rwkv_k_k[occ], k_a=rwkv_k_a[occ])
            assert last, "the context stream's RWKV output path is only needed for non-final layers"
            r, v, kk, lw, bb, kd, g = _rwkv_features(x, g1, mod, p, with_out=True, name="rwkv_feat")
            vc, kkc, lwc, bc, kdc = _rwkv_features(xc, g1, modc, p, with_out=False, name="rwkv_feat_ctx")
            nh = d // HEAD_DIM
            s0 = jnp.zeros((2, b, nh, HEAD_DIM, HEAD_DIM), F32)
            (s_ctx,) = _rwkv_scan(kkc, vc, None, lwc, bc, kdc, s0, name="rwkv_scan_ctx")
            o, _ = _rwkv_scan(kk, v, r, lw, bb, kd, s_ctx, name="rwkv_scan")
            x = _rwkv_output(o, r, v, kd, g, rwkv_r_k[occ], rwkv_ln_w[occ], rwkv_ln_b[occ], rwkv_w_o[occ],
                             x, mod, name="rwkv_out")
        x, xc = _hier_moe(x, None if last else xc, mod, modc, norm_g[i, 1], moe_wg[i], moe_bg[i], moe_we[i],
                          moe_be[i], moe_w_up[i], moe_w_down[i], i)
    return _final_norm(x, final_g)
```

```python
import functools
import math

import jax
import jax.numpy as jnp
from jax import lax
from jax.experimental import pallas as pl
from jax.experimental.pallas import tpu as pltpu

F32 = jnp.float32
BF16 = jnp.bfloat16

HEAD_DIM = 64
GRID_W = 64
ROPE_BASE = 10000.0
NORM_EPS = 1e-6
NEG_INF = -1e30
WIN_KV_HEADS = 4
WIN_BLOCK = 128
POOL_WINDOWS = (2, 4, 8, 16)
POOL_HALO = 8
GN_EPS = 64e-5
N_GROUPS = 4
EXPERTS_PER_GROUP = 8
N_EXPERTS = N_GROUPS * EXPERTS_PER_GROUP
MOE_BLOCK = 256
DIFF_Q_ROWS = 1024
DIFF_Q_SUB = 256
RWKV_CHUNK = 64
RWKV_HEADS_PER_STEP = 16
LANES = 128
V7X_VMEM_LIMIT = 48 * 1024 * 1024
HI = lax.Precision.HIGHEST


def _cp(sem, vmem=V7X_VMEM_LIMIT):
    return pltpu.CompilerParams(dimension_semantics=sem, vmem_limit_bytes=vmem)


def _bf(x):
    return x.astype(BF16)


def _dot(a, b, precision=None):
    return jnp.dot(a, b, preferred_element_type=F32, precision=precision)


def _dot_t(a, b, precision=None):
    return lax.dot_general(a, b, (((1,), (1,)), ((), ())), preferred_element_type=F32, precision=precision)


def _dot_l(a, b, precision=None):
    return lax.dot_general(a, b, (((0,), (0,)), ((), ())), preferred_element_type=F32, precision=precision)


def _mm(a, b):
    return _dot(_bf(a), _bf(b))


def _mm_t(a, b):
    return _dot_t(_bf(a), _bf(b))


def _mm_l(a, b):
    return _dot_l(_bf(a), _bf(b))


def _cumulate(tri, x):
    hi = _bf(x)
    r1 = x - hi.astype(F32)
    mid = _bf(r1)
    lo = _bf(r1 - mid.astype(F32))
    return _dot(tri, hi) + _dot(tri, mid) + _dot(tri, lo)


def _norm_mod(x, g, scale, shift):
    ms = jnp.mean(x * x, axis=-1, keepdims=True)
    y = x * lax.rsqrt(ms + NORM_EPS) * g
    return y * (1.0 + scale) + shift


def _sigmoid(x):
    return 1.0 / (1.0 + jnp.exp(-x))


def _row_tile(n, pref):
    t = min(pref, n)
    assert n % t == 0
    return t


def _mod_body(c_ref, w_ref, b_ref, o_ref):
    c = c_ref[...]
    s = c * _sigmoid(c)
    o_ref[0] = _dot(_bf(s), _bf(w_ref[0])) + b_ref[0]


def _mod_all(c_all, mod_w, mod_b):
    depth, d, n = mod_w.shape
    r = c_all.shape[0]
    tn = 1536
    return pl.pallas_call(
        _mod_body,
        grid=(depth, n // tn),
        in_specs=[
            pl.BlockSpec((r, d), lambda i, j: (0, 0)),
            pl.BlockSpec((1, d, tn), lambda i, j: (i, 0, j)),
            pl.BlockSpec((1, 1, tn), lambda i, j: (i, 0, j)),
        ],
        out_specs=pl.BlockSpec((1, r, tn), lambda i, j: (i, 0, j)),
        out_shape=jax.ShapeDtypeStruct((depth, r, n), F32),
        compiler_params=_cp(("parallel", "parallel")),
        name="adaln_mod",
    )(c_all, mod_w, mod_b.reshape(depth, 1, n))


def _mod_spec(mod):
    if mod.shape[0] == 1:
        return pl.BlockSpec((1,) + mod.shape[1:], lambda b, i: (0, 0, 0))
    return pl.BlockSpec((1,) + mod.shape[1:], lambda b, i: (b, 0, 0))


def _rope_tile(y, cos, sin, first_half):
    fwd = pltpu.roll(y, LANES - 16, 1)
    bwd = pltpu.roll(y, 16, 1)
    rot = jnp.where(first_half, -fwd, bwd)
    return y * cos + rot * sin


def _proj_body(*refs, n_rope, n_q, scale_i, shift_i):
    if n_rope:
        x_ref, g_ref, m_ref, w_ref, cos_ref, sin_ref, o_ref = refs
    else:
        x_ref, g_ref, m_ref, w_ref, o_ref = refs
    m = m_ref[0]
    h = _norm_mod(x_ref[0], g_ref[...], m[scale_i:scale_i + 1], m[shift_i:shift_i + 1])
    y = _dot(_bf(h), w_ref[...])
    n = y.shape[1]
    if n_rope:
        cos, sin = cos_ref[...], sin_ref[...]
        first_half = (lax.broadcasted_iota(jnp.int32, cos.shape, 1) & 31) < 16
    for j in range(n // LANES):
        blk = y[:, j * LANES:(j + 1) * LANES]
        if j * LANES < n_q:
            blk = blk * (HEAD_DIM ** -0.5)
        if j * LANES < n_rope:
            blk = _rope_tile(blk, cos, sin, first_half)
        o_ref[0, :, j * LANES:(j + 1) * LANES] = _bf(blk)


def _proj(x, g, mod, w, cos, sin, *, n_rope, n_q, scale_i, shift_i, name):
    b, l, d = x.shape
    n = w.shape[1]
    tm = _row_tile(l, 256)
    in_specs = [
        pl.BlockSpec((1, tm, d), lambda bi, i: (bi, i, 0)),
        pl.BlockSpec((1, d), lambda bi, i: (0, 0)),
        _mod_spec(mod),
        pl.BlockSpec((d, n), lambda bi, i: (0, 0)),
    ]
    args = [x, g.reshape(1, d), mod, w]
    if n_rope:
        in_specs += [pl.BlockSpec((tm, LANES), lambda bi, i: (i, 0))] * 2
        args += [cos, sin]
    return pl.pallas_call(
        functools.partial(_proj_body, n_rope=n_rope, n_q=n_q, scale_i=scale_i, shift_i=shift_i),
        grid=(b, l // tm),
        in_specs=in_specs,
        out_specs=pl.BlockSpec((1, tm, n), lambda bi, i: (bi, i, 0)),
        out_shape=jax.ShapeDtypeStruct((b, l, n), BF16),
        compiler_params=_cp(("parallel", "parallel")),
        name=name,
    )(*args)


def _res_body(a_ref, w_ref, x_ref, m_ref, o_ref, *, gate_i):
    y = _dot(a_ref[0], w_ref[...])
    o_ref[0] = x_ref[0] + m_ref[0][gate_i:gate_i + 1] * y


def _res_proj(a, w, x, mod, *, gate_i, name):
    b, l, d = x.shape
    k = a.shape[-1]
    tm = _row_tile(l, 512)
    in_specs = [
        pl.BlockSpec((1, tm, k), lambda bi, i: (bi, i, 0)),
        pl.BlockSpec((k, d), lambda bi, i: (0, 0)),
        pl.BlockSpec((1, tm, d), lambda bi, i: (bi, i, 0)),
        _mod_spec(mod),
    ]
    args = [a, w, x, mod]
    return pl.pallas_call(
        functools.partial(_res_body, gate_i=gate_i),
        grid=(b, l // tm),
        in_specs=in_specs,
        out_specs=pl.BlockSpec((1, tm, d), lambda bi, i: (bi, i, 0)),
        out_shape=jax.ShapeDtypeStruct((b, l, d), F32),
        input_output_aliases={2: 0},
        compiler_params=_cp(("parallel", "parallel")),
        name=name,
    )(*args)


def _win_body(*refs, seq, local, n_heads):
    if local:
        q_ref, kp_ref, kc_ref, kn_ref, vp_ref, vc_ref, vn_ref, kx_ref, vx_ref, sink_ref, o_ref = refs
    else:
        q_ref, kx_ref, vx_ref, sink_ref, o_ref = refs
    tq = q_ref.shape[1]
    grp = n_heads // WIN_KV_HEADS
    n = pl.program_id(1)
    if local:
        span = 3 * tq
        row = lax.broadcasted_iota(jnp.int32, (grp * tq, span), 0) & (tq - 1)
        col = lax.broadcasted_iota(jnp.int32, (grp * tq, span), 1)
        rel = col - tq - row
        key_pos = n * tq - tq + col
        mask = (jnp.abs(rel) <= tq) & (key_pos >= 0) & (key_pos < seq)
    hks = range(WIN_KV_HEADS)
    ks = [slice(hk * HEAD_DIM, (hk + 1) * HEAD_DIM) for hk in hks]
    qh = [jnp.concatenate(
        [q_ref[0, :, (hk * grp + g) * HEAD_DIM:(hk * grp + g + 1) * HEAD_DIM] for g in range(grp)], axis=0)
        for hk in hks]
    sink = [jnp.concatenate([jnp.full((tq, 1), sink_ref[hk * grp + g], F32) for g in range(grp)], axis=0)
            for hk in hks]
    s_ctx = [_dot_t(qh[hk], kx_ref[0, :, ks[hk]]) for hk in hks]
    m = [jnp.maximum(jnp.max(s_ctx[hk], axis=1, keepdims=True), sink[hk]) for hk in hks]
    if local:
        s_loc = [jnp.where(mask, _dot_t(qh[hk], jnp.concatenate(
            [kp_ref[0, :, ks[hk]], kc_ref[0, :, ks[hk]], kn_ref[0, :, ks[hk]]], axis=0)), NEG_INF) for hk in hks]
        m = [jnp.maximum(m[hk], jnp.max(s_loc[hk], axis=1, keepdims=True)) for hk in hks]
    p_ctx = [jnp.exp(s_ctx[hk] - m[hk]) for hk in hks]
    den = [jnp.sum(p_ctx[hk], axis=1, keepdims=True) + jnp.exp(sink[hk] - m[hk]) for hk in hks]
    o = [_dot(_bf(p_ctx[hk]), vx_ref[0, :, ks[hk]]) for hk in hks]
    if local:
        p_loc = [jnp.exp(s_loc[hk] - m[hk]) for hk in hks]
        den = [den[hk] + jnp.sum(p_loc[hk], axis=1, keepdims=True) for hk in hks]
        o = [o[hk] + _dot(_bf(p_loc[hk]), jnp.concatenate(
            [vp_ref[0, :, ks[hk]], vc_ref[0, :, ks[hk]], vn_ref[0, :, ks[hk]]], axis=0)) for hk in hks]
    for hk in hks:
        oh = o[hk] / den[hk]
        for g in range(grp):
            hq = hk * grp + g
            o_ref[0, :, hq * HEAD_DIM:(hq + 1) * HEAD_DIM] = _bf(oh[g * tq:(g + 1) * tq])


def _win_attn(qkv, qkvc, sink, *, local, name):
    src = qkv if local else qkvc
    b, l, _ = src.shape
    c = qkvc.shape[1]
    n_heads = sink.shape[0]
    d = n_heads * HEAD_DIM
    kvw = WIN_KV_HEADS * HEAD_DIM
    kcol = d // kvw
    tq = WIN_BLOCK
    nb = l // tq
    in_specs = [pl.BlockSpec((1, tq, d), lambda bi, i: (bi, i, 0))]
    args = [src]
    if local:
        for colb in (kcol, kcol + 1):
            in_specs += [
                pl.BlockSpec((1, tq, kvw), lambda bi, i, colb=colb: (bi, jnp.maximum(i - 1, 0), colb)),
                pl.BlockSpec((1, tq, kvw), lambda bi, i, colb=colb: (bi, i, colb)),
                pl.BlockSpec((1, tq, kvw), lambda bi, i, colb=colb: (bi, jnp.minimum(i + 1, nb - 1), colb)),
            ]
            args += [qkv, qkv, qkv]
    in_specs += [
        pl.BlockSpec((1, c, kvw), lambda bi, i: (bi, 0, kcol)),
        pl.BlockSpec((1, c, kvw), lambda bi, i: (bi, 0, kcol + 1)),
        pl.BlockSpec(memory_space=pltpu.SMEM),
    ]
    args += [qkvc, qkvc, sink]
    return pl.pallas_call(
        functools.partial(_win_body, seq=l, local=local, n_heads=n_heads),
        grid=(b, nb),
        in_specs=in_specs,
        out_specs=pl.BlockSpec((1, tq, d), lambda bi, i: (bi, i, 0)),
        out_shape=jax.ShapeDtypeStruct((b, l, d), BF16),
        compiler_params=_cp(("parallel", "parallel")),
        name=name,
    )(*args)


def _diff_body(*refs, n_lat, tk, sub, lambda_init):
    if n_lat:
        lam_ref, g_ref, q_ref, kl_ref, vl_ref, kx_ref, vx_ref, o_ref, s_ref = refs
    else:
        lam_ref, g_ref, q_ref, kx_ref, vx_ref, o_ref, s_ref = refs
    hw = 2 * HEAD_DIM
    n_sub = q_ref.shape[1] // sub
    lane = lax.broadcasted_iota(jnp.int32, (sub, hw), 1)
    chunks = [(kl_ref, vl_ref, i * tk, tk, i * tk) for i in range(n_lat)] if n_lat else []
    chunks.append((kx_ref, vx_ref, 0, kx_ref.shape[1], n_lat * tk))
    lam = lam_ref[...]
    lam_full = (jnp.exp(jnp.sum(lam[0:1] * lam[1:2], axis=1, keepdims=True))
                - jnp.exp(jnp.sum(lam[2:3] * lam[3:4], axis=1, keepdims=True)) + lambda_init)

    def stacked_q(j):
        q = q_ref[0, j * sub:(j + 1) * sub, :]
        zero = jnp.zeros_like(q)
        return jnp.concatenate([jnp.where(lane < HEAD_DIM, q, zero), jnp.where(lane >= HEAD_DIM, q, zero)], axis=0)

    def score_chunk(j, qq, ch, mx):
        k_ref, _, row, n, col = ch
        s = _dot_t(qq, k_ref[0, row:row + n, :])
        s_ref[j % 2, :, col:col + n] = s
        for t in range(n // LANES):
            mx = jnp.maximum(mx, s[:, t * LANES:(t + 1) * LANES])
        return mx

    def value_chunk(j, m, ch, acc):
        _, v_ref, row, n, col = ch
        p = jnp.concatenate(
            [jnp.exp(s_ref[j % 2, :, col + t * LANES:col + (t + 1) * LANES] - m) for t in range(n // LANES)], axis=1)
        v = v_ref[0, row:row + n, :]
        return acc + _dot(_bf(p), jnp.concatenate([v, jnp.ones_like(v)], axis=1))

    def finish(j, acc):
        a = acc[:, :hw] / acc[:, hw:]
        o = a[:sub] - lam_full * a[sub:]
        ms = jnp.mean(o * o, axis=-1, keepdims=True)
        o = o * lax.rsqrt(ms + NORM_EPS) * g_ref[...] * (1.0 - lambda_init)
        o_ref[0, j * sub:(j + 1) * sub, :] = _bf(o)

    m_prev = None
    for j in range(n_sub + 1):
        if j < n_sub:
            qq = stacked_q(j)
            mx = jnp.full((2 * sub, LANES), NEG_INF, F32)
        if j > 0:
            acc = jnp.zeros((2 * sub, 2 * hw), F32)
        for ch in chunks:
            if j > 0:
                acc = value_chunk(j - 1, m_prev, ch, acc)
            if j < n_sub:
                mx = score_chunk(j, qq, ch, mx)
        if j > 0:
            finish(j - 1, acc)
        if j < n_sub:
            m_prev = jnp.broadcast_to(jnp.max(mx, axis=1, keepdims=True), (2 * sub, LANES))


def _diff_attn(qkv, qkvc, lam, subln_g, *, local, lambda_init, name):
    src = qkv if local else qkvc
    b, l, n3 = src.shape
    d = n3 // 3
    c = qkvc.shape[1]
    hw = 2 * HEAD_DIM
    nh = d // hw
    tq = _row_tile(l, DIFF_Q_ROWS)
    sub = min(DIFF_Q_SUB, tq)
    tk = 512
    in_specs = [
        pl.BlockSpec((4, HEAD_DIM), lambda bi, h, i: (0, 0)),
        pl.BlockSpec((1, hw), lambda bi, h, i: (0, 0)),
        pl.BlockSpec((1, tq, hw), lambda bi, h, i: (bi, i, h)),
    ]
    args = [lam, subln_g.reshape(1, hw), src]
    n_lat = 0
    if local:
        s = qkv.shape[1]
        n_lat = s // tk
        in_specs += [
            pl.BlockSpec((1, s, hw), lambda bi, h, i: (bi, 0, nh + h)),
            pl.BlockSpec((1, s, hw), lambda bi, h, i: (bi, 0, 2 * nh + h)),
        ]
        args += [qkv, qkv]
    in_specs += [
        pl.BlockSpec((1, c, hw), lambda bi, h, i: (bi, 0, nh + h)),
        pl.BlockSpec((1, c, hw), lambda bi, h, i: (bi, 0, 2 * nh + h)),
    ]
    args += [qkvc, qkvc]
    return pl.pallas_call(
        functools.partial(_diff_body, n_lat=n_lat, tk=tk, sub=sub, lambda_init=lambda_init),
        grid=(b, nh, l // tq),
        in_specs=in_specs,
        out_specs=pl.BlockSpec((1, tq, hw), lambda bi, h, i: (bi, i, h)),
        out_shape=jax.ShapeDtypeStruct((b, l, d), BF16),
        scratch_shapes=[pltpu.VMEM((2, 2 * sub, n_lat * tk + c), F32)],
        compiler_params=_cp(("parallel", "parallel", "parallel")),
        name=name,
    )(*args)


def _halo_specs(l, tm, d):
    nh = l // POOL_HALO
    per = tm // POOL_HALO
    return [
        pl.BlockSpec((1, POOL_HALO, d), lambda bi, i: (bi, jnp.maximum(i * per - 1, 0), 0)),
        pl.BlockSpec((1, tm, d), lambda bi, i: (bi, i, 0)),
        pl.BlockSpec((1, POOL_HALO, d), lambda bi, i: (bi, jnp.minimum((i + 1) * per, nh - 1), 0)),
    ]


def _fill_normed(h_ref, xp_ref, x_ref, xn_ref, g, scale, shift):
    tm = x_ref.shape[1]
    h_ref[0:POOL_HALO, :] = _norm_mod(xp_ref[0], g, scale, shift)
    h_ref[POOL_HALO:POOL_HALO + tm, :] = _norm_mod(x_ref[0], g, scale, shift)
    h_ref[POOL_HALO + tm:2 * POOL_HALO + tm, :] = _norm_mod(xn_ref[0], g, scale, shift)


def _pool_body(xp_ref, x_ref, xn_ref, g_ref, m_ref, w_ref, b_ref, s_ref, o_ref, h_ref, *, seq):
    tm = x_ref.shape[1]
    d = x_ref.shape[2]
    gd = d // len(POOL_WINDOWS)
    m = m_ref[0]
    _fill_normed(h_ref, xp_ref, x_ref, xn_ref, g_ref[...], m[1:2], m[0:1])
    pos = pl.program_id(1) * tm + lax.broadcasted_iota(jnp.int32, (tm, 1), 0)
    x = x_ref[0]
    for gi, w in enumerate(POOL_WINDOWS):
        cols = slice(gi * gd, (gi + 1) * gd)
        acc = jnp.zeros((tm, gd), F32)
        for off in range(-(w // 2), w - w // 2):
            valid = (pos + off >= 0) & (pos + off < seq)
            acc = acc + jnp.where(valid, h_ref[POOL_HALO + off:POOL_HALO + off + tm, cols], 0.0)
        lo = jnp.maximum(pos - w // 2, 0)
        hi = jnp.minimum(pos + w - w // 2, seq)
        y = acc / (hi - lo).astype(F32) - h_ref[POOL_HALO:POOL_HALO + tm, cols]
        y = (_dot(_bf(y), w_ref[gi]) + b_ref[:, cols]) * s_ref[:, cols]
        o_ref[0, :, cols] = x[:, cols] + m[2:3, cols] * y


def _pool_mix(x, g, mod, w_group, b_group, layer_scale, *, name):
    b, l, d = x.shape
    tm = _row_tile(l, 256)
    gd = d // len(POOL_WINDOWS)
    return pl.pallas_call(
        functools.partial(_pool_body, seq=l),
        grid=(b, l // tm),
        in_specs=_halo_specs(l, tm, d) + [
            pl.BlockSpec((1, d), lambda bi, i: (0, 0)),
            _mod_spec(mod),
            pl.BlockSpec((len(POOL_WINDOWS), gd, gd), lambda bi, i: (0, 0, 0)),
            pl.BlockSpec((1, d), lambda bi, i: (0, 0)),
            pl.BlockSpec((1, d), lambda bi, i: (0, 0)),
        ],
        out_specs=pl.BlockSpec((1, tm, d), lambda bi, i: (bi, i, 0)),
        out_shape=jax.ShapeDtypeStruct((b, l, d), F32),
        scratch_shapes=[pltpu.VMEM((tm + 2 * POOL_HALO, d), F32)],
        compiler_params=_cp(("parallel", "parallel")),
        name=name,
    )(x, x, x, g.reshape(1, d), mod, _bf(w_group), b_group.reshape(1, d), layer_scale.reshape(1, d))


def _softplus(z):
    return jnp.maximum(z, 0.0) + jnp.log1p(jnp.exp(-jnp.abs(z)))


def _rwkv_feat_body(*refs, seq, with_out):
    (xp_ref, x_ref, xn_ref, g_ref, m_ref, mu_ref, wrkv_ref, w0_ref, wa1_ref, wa2_ref, a0_ref, aa1_ref,
     aa2_ref, g1_ref, g2_ref, kk_ref, ka_ref) = refs[:17]
    if with_out:
        r_out, v_out, kk_out, lw_out, b_out, kd_out, g_out, h_ref = refs[17:]
    else:
        v_out, kk_out, lw_out, b_out, kd_out, h_ref = refs[17:]
    tm = x_ref.shape[1]
    d = x_ref.shape[2]
    nh = d // HEAD_DIM
    m = m_ref[0]
    _fill_normed(h_ref, xp_ref, x_ref, xn_ref, g_ref[...], m[1:2], m[0:1])
    pos = pl.program_id(1) * tm + lax.broadcasted_iota(jnp.int32, (tm, 1), 0)
    t = h_ref[POOL_HALO:POOL_HALO + tm, :]
    dp = jnp.where(pos >= 1, h_ref[POOL_HALO - 1:POOL_HALO - 1 + tm, :], 0.0) - t
    dn = jnp.where(pos < seq - 1, h_ref[POOL_HALO + 1:POOL_HALO + 1 + tm, :], 0.0) - t

    def mix(i):
        return _bf(t + dp * mu_ref[0, i:i + 1, :] + dn * mu_ref[1, i:i + 1, :])

    def put(ref, val, lead=()):
        for h in range(nh):
            ref[lead + (0, h)] = val[:, h * HEAD_DIM:(h + 1) * HEAD_DIM]

    k = _dot(mix(2), wrkv_ref[1])
    put(v_out, _dot(mix(3), wrkv_ref[2]))
    kk = k * kk_ref[...]
    kk_heads = []
    for h in range(nh):
        kh = kk[:, h * HEAD_DIM:(h + 1) * HEAD_DIM]
        nrm = jnp.sqrt(jnp.sum(kh * kh, axis=-1, keepdims=True))
        kk_heads.append(kh / jnp.maximum(nrm, 1e-12))
        kk_out[0, h] = kk_heads[h]
    xw, xa = mix(1), mix(4)
    for di in range(2):
        z = w0_ref[di:di + 1, :] + _dot(_bf(jnp.tanh(_dot(xw, wa1_ref[di]))), wa2_ref[di])
        w = -_softplus(-z) - 0.5
        put(lw_out, -jnp.exp(w), (di,))
        a = _sigmoid(a0_ref[di:di + 1, :] + _dot(_bf(_dot(xa, aa1_ref[di])), aa2_ref[di]))
        put(kd_out, k * (1.0 + (a - 1.0) * ka_ref[...]), (di,))
        for h in range(nh):
            cols = slice(h * HEAD_DIM, (h + 1) * HEAD_DIM)
            b_out[di, 0, h] = kk_heads[h] * a[:, cols]
    if with_out:
        put(r_out, _dot(mix(0), wrkv_ref[0]))
        g_out[0] = _dot(_bf(_sigmoid(_dot(mix(5), g1_ref[...]))), g2_ref[...])


def _rwkv_features(x, g, mod, p, *, with_out, name):
    b, l, d = x.shape
    nh = d // HEAD_DIM
    tm = _row_tile(l, 128)
    const = lambda a: pl.BlockSpec(a.shape, lambda bi, i, nd=a.ndim: (0,) * nd)
    weights = [p["mu"], _bf(p["w_rkv"]), p["w0"], _bf(p["w_a1"]), _bf(p["w_a2"]), p["a0"], _bf(p["a_a1"]),
               _bf(p["a_a2"]), _bf(p["g1"]), _bf(p["g2"]), p["k_k"].reshape(1, d), p["k_a"].reshape(1, d)]
    head = pl.BlockSpec((1, nh, tm, HEAD_DIM), lambda bi, i: (bi, 0, i, 0))
    head2 = pl.BlockSpec((2, 1, nh, tm, HEAD_DIM), lambda bi, i: (0, bi, 0, i, 0))
    hs = jax.ShapeDtypeStruct((b, nh, l, HEAD_DIM), F32)
    hs2 = jax.ShapeDtypeStruct((2, b, nh, l, HEAD_DIM), F32)
    out_specs = [head, head, head2, head2, head2]
    out_shape = [hs, hs, hs2, hs2, hs2]
    if with_out:
        out_specs = [head] + out_specs + [pl.BlockSpec((1, tm, d), lambda bi, i: (bi, i, 0))]
        out_shape = [hs] + out_shape + [jax.ShapeDtypeStruct((b, l, d), F32)]
    return pl.pallas_call(
        functools.partial(_rwkv_feat_body, seq=l, with_out=with_out),
        grid=(b, l // tm),
        in_specs=_halo_specs(l, tm, d) + [pl.BlockSpec((1, d), lambda bi, i: (0, 0)), _mod_spec(mod)]
        + [const(a) for a in weights],
        out_specs=out_specs,
        out_shape=out_shape,
        scratch_shapes=[pltpu.VMEM((tm + 2 * POOL_HALO, d), F32)],
        compiler_params=_cp(("parallel", "parallel")),
        name=name,
    )(x, x, x, g.reshape(1, d), mod, *weights)


def _scan_body(*refs, emit, n_chunks):
    if emit:
        kk_ref, v_ref, r_ref, lw_ref, b_ref, kd_ref, s0_ref, o_ref, sfin_ref, s_ref = refs
    else:
        kk_ref, v_ref, lw_ref, b_ref, kd_ref, s0_ref, sfin_ref, s_ref = refs
    hb = kk_ref.shape[1]
    L = kk_ref.shape[2]
    rev = pl.program_id(0) == 1
    c = pl.program_id(3)

    @pl.when(c == 0)
    def _():
        s_ref[...] = s0_ref[0, 0]

    row = lax.broadcasted_iota(jnp.int32, (L, L), 0)
    col = lax.broadcasted_iota(jnp.int32, (L, L), 1)
    flip = rev.astype(jnp.int32)
    p_row = row + flip * (L - 1 - 2 * row)
    p_col = col + flip * (L - 1 - 2 * col)
    incl = p_col <= p_row
    strict = p_col < p_row
    tri = _bf(incl.astype(F32))
    eye = row == col
    assert L == HEAD_DIM
    levels = []
    m = 1
    while m < L:
        same = (p_row // (2 * m)) == (p_col // (2 * m))
        levels.append(same & ((p_row & (2 * m - 1)) >= m) & ((p_col & (2 * m - 1)) < m))
        m *= 2

    hs = range(hb)
    lw = [lw_ref[0, 0, h] for h in hs]
    v = [v_ref[0, h] for h in hs]
    b = [b_ref[0, 0, h] for h in hs]
    kd = [kd_ref[0, 0, h] for h in hs]
    s0 = [s_ref[h] for h in hs]
    cl = [_cumulate(tri, lw[h]) for h in hs]
    cl_tot = [jnp.sum(lw[h], axis=0, keepdims=True) for h in hs]
    w_inv = [jnp.exp(-cl[h]) for h in hs]
    kh = [kk_ref[0, h] * jnp.exp(cl[h] - lw[h]) for h in hs]
    rhs = [jnp.concatenate([b[h] * w_inv[h], kd[h] * w_inv[h]], axis=0) for h in hs]
    if emit:
        rh = [r_ref[0, h] * jnp.exp(cl[h]) for h in hs]
        lhs = [jnp.concatenate([kh[h], rh[h]], axis=0) for h in hs]
    else:
        lhs = kh
    big = [_mm_t(lhs[h], rhs[h]) for h in hs]
    m_b = [big[h][:L, :L] for h in hs]
    mkv = [_mm(jnp.where(strict, big[h][:L, L:], 0.0), v[h]) for h in hs]
    t = [jnp.where(eye, 1.0, 0.0) - jnp.where(levels[0], m_b[h], 0.0) for h in hs]
    for lm in levels[1:]:
        tl = [_mm(t[h], jnp.where(lm, m_b[h], 0.0)) for h in hs]
        t = [t[h] - _mm(tl[h], t[h]) for h in hs]
    y = [_mm(t[h], jnp.concatenate([kh[h], mkv[h]], axis=1)) for h in hs]
    w_last = [jnp.exp(cl_tot[h] - cl[h]) for h in hs]
    gh = [_mm_l(y[h], b[h] * w_last[h]) for h in hs]
    vk = [_mm_l(v[h], kd[h] * w_last[h]) for h in hs]
    g_mat = [jnp.where(eye, jnp.exp(cl_tot[h]), 0.0) - gh[h][:HEAD_DIM] for h in hs]
    if emit:
        ab_y = [_mm(jnp.where(incl, big[h][L:, :L], 0.0), y[h]) for h in hs]
        akv = [_mm(jnp.where(incl, big[h][L:, L:], 0.0), v[h]) for h in hs]
        rs = [_mm_t(rh[h] - ab_y[h][:, :HEAD_DIM], s0[h]) for h in hs]
        for h in hs:
            o_ref[0, 0, h] = rs[h] + akv[h] - ab_y[h][:, HEAD_DIM:]
    sg = [_mm(s0[h], g_mat[h]) for h in hs]
    for h in hs:
        s_ref[h] = sg[h] + vk[h] - gh[h][HEAD_DIM:]

    @pl.when(c == n_chunks - 1)
    def _():
        sfin_ref[0, 0] = s_ref[...]


def _rwkv_scan(kk, v, r, lw, b, kd, s0, *, name):
    bsz, nh, l, dh = kk.shape
    L = RWKV_CHUNK
    hb = RWKV_HEADS_PER_STEP
    nch = l // L
    emit = r is not None

    def chunk(d, c):
        return c + d * (nch - 1 - 2 * c)

    one = pl.BlockSpec((1, hb, L, dh), lambda d, bi, hg, c: (bi, hg, chunk(d, c), 0))
    two = pl.BlockSpec((1, 1, hb, L, dh), lambda d, bi, hg, c: (d, bi, hg, chunk(d, c), 0))
    st = pl.BlockSpec((1, 1, hb, dh, dh), lambda d, bi, hg, c: (d, bi, hg, 0, 0))
    in_specs = [one, one] + ([one] if emit else []) + [two, two, two, st]
    args = [kk, v] + ([r] if emit else []) + [lw, b, kd, s0]
    s_shape = jax.ShapeDtypeStruct((2, bsz, nh, dh, dh), F32)
    if emit:
        out_specs = [two, st]
        out_shape = [jax.ShapeDtypeStruct((2, bsz, nh, l, dh), F32), s_shape]
    else:
        out_specs = [st]
        out_shape = [s_shape]
    return pl.pallas_call(
        functools.partial(_scan_body, emit=emit, n_chunks=nch),
        grid=(2, bsz, nh // hb, nch),
        in_specs=in_specs,
        out_specs=out_specs,
        out_shape=out_shape,
        scratch_shapes=[pltpu.VMEM((hb, dh, dh), F32)],
        compiler_params=_cp(("parallel", "parallel", "parallel", "arbitrary")),
        name=name,
    )(*args)


def _rwkv_out_body(o_ref, r_ref, v_ref, kd_ref, g_ref, rk_ref, lnw_ref, lnb_ref, w_ref, x_ref, m_ref, out_ref):
    nh = r_ref.shape[1]
    tm = r_ref.shape[2]
    acc = jnp.zeros((tm, x_ref.shape[2]), F32)
    for h in range(nh):
        cols = slice(h * HEAD_DIM, (h + 1) * HEAD_DIM)
        o = o_ref[0, 0, h] + o_ref[1, 0, h]
        mean = jnp.mean(o, axis=-1, keepdims=True)
        var = jnp.mean(jnp.square(o - mean), axis=-1, keepdims=True)
        on = (o - mean) * lax.rsqrt(var + GN_EPS) * lnw_ref[h:h + 1, :] + lnb_ref[h:h + 1, :]
        r = r_ref[0, h]
        rk = rk_ref[h:h + 1, :]
        bonus = (jnp.sum(r * kd_ref[0, 0, h] * rk, axis=-1, keepdims=True) * v_ref[0, h]
                 + jnp.sum(r * kd_ref[1, 0, h] * rk, axis=-1, keepdims=True) * v_ref[0, h])
        a = (on + bonus) * g_ref[0, :, cols]
        acc = acc + _dot(_bf(a), w_ref[cols, :])
    out_ref[0] = x_ref[0] + m_ref[0][2:3] * acc


def _rwkv_output(o, r, v, kd, g, r_k, ln_w, ln_b, w_o, x, mod, *, name):
    b, l, d = x.shape
    nh = d // HEAD_DIM
    tm = _row_tile(l, 256)
    head = pl.BlockSpec((1, nh, tm, HEAD_DIM), lambda bi, i: (bi, 0, i, 0))
    head2 = pl.BlockSpec((2, 1, nh, tm, HEAD_DIM), lambda bi, i: (0, bi, 0, i, 0))
    row = pl.BlockSpec((1, tm, d), lambda bi, i: (bi, i, 0))
    small = pl.BlockSpec((nh, HEAD_DIM), lambda bi, i: (0, 0))
    return pl.pallas_call(
        _rwkv_out_body,
        grid=(b, l // tm),
        in_specs=[head2, head, head, head2, row, small, small, small,
                  pl.BlockSpec((d, d), lambda bi, i: (0, 0)), row, _mod_spec(mod)],
        out_specs=row,
        out_shape=jax.ShapeDtypeStruct((b, l, d), F32),
        input_output_aliases={9: 0},
        compiler_params=_cp(("parallel", "parallel")),
        name=name,
    )(o, r, v, kd, g, r_k, ln_w.reshape(nh, HEAD_DIM), ln_b.reshape(nh, HEAD_DIM), _bf(w_o), x, mod)


def _router_body(x_ref, g_ref, m_ref, whi_ref, wlo_ref, b_ref, cin_ref, h_ref, sel_ref, cnt_ref, run_ref):
    @pl.when((pl.program_id(0) == 0) & (pl.program_id(1) == 0))
    def _():
        run_ref[...] = cin_ref[...]

    m = m_ref[0]
    h = _norm_mod(x_ref[0], g_ref[...], m[4:5], m[3:4])
    hi = _bf(h)
    h_ref[0] = hi
    lo_part = _bf(h - hi.astype(F32))
    lg = _dot(hi, whi_ref[...]) + _dot(hi, wlo_ref[...]) + _dot(lo_part, whi_ref[...]) + b_ref[...]
    lane = lax.broadcasted_iota(jnp.int32, lg.shape, 1)
    big = jnp.int32(1 << 20)
    g_logit = jnp.where(lane < N_GROUPS, lg, -jnp.inf)
    g_max = jnp.max(g_logit, axis=1, keepdims=True)
    gsel = jnp.min(jnp.where(g_logit == g_max, lane, big), axis=1, keepdims=True)
    p_grp = 1.0 / jnp.sum(jnp.exp(g_logit - g_max), axis=1, keepdims=True)
    lo = N_GROUPS + gsel * EXPERTS_PER_GROUP
    e_logit = jnp.where((lane >= lo) & (lane < lo + EXPERTS_PER_GROUP), lg, -jnp.inf)
    e_max = jnp.max(e_logit, axis=1, keepdims=True)
    i1 = jnp.min(jnp.where(e_logit == e_max, lane, big), axis=1, keepdims=True)
    rest = jnp.where(lane == i1, -jnp.inf, e_logit)
    e2 = jnp.max(rest, axis=1, keepdims=True)
    i2 = jnp.min(jnp.where(rest == e2, lane, big), axis=1, keepdims=True)
    q2 = jnp.exp(e2 - e_max)
    w1 = p_grp / (1.0 + q2)
    w2 = p_grp * q2 / (1.0 + q2)
    pick1, pick2 = lane == i1, lane == i2
    both = jnp.where(pick1 | pick2, 1.0, 0.0)
    tm = lg.shape[0]
    earlier = lax.broadcasted_iota(jnp.int32, (tm, tm), 1) < lax.broadcasted_iota(jnp.int32, (tm, tm), 0)
    ahead = _dot(_bf(earlier.astype(F32)), _bf(both)) + run_ref[...]
    r1 = jnp.sum(jnp.where(pick1, ahead, 0.0), axis=1, keepdims=True)
    r2 = jnp.sum(jnp.where(pick2, ahead, 0.0), axis=1, keepdims=True)
    run_ref[...] = run_ref[...] + jnp.sum(both, axis=0, keepdims=True)
    cnt_ref[...] = run_ref[...]
    cols = ((i1 - N_GROUPS).astype(F32), (i2 - N_GROUPS).astype(F32), w1, w2, r1, r2)
    sel = jnp.zeros_like(lg)
    for j, val in enumerate(cols):
        sel = jnp.where(lane == j, val, sel)
    sel_ref[0] = sel


def _router(x, g, mod, w_hi, w_lo, b_r, cnt_in, *, name):
    b, l, d = x.shape
    tm = _row_tile(l, 256)
    wide = pl.BlockSpec((1, LANES), lambda bi, i: (0, 0))
    return pl.pallas_call(
        _router_body,
        grid=(b, l // tm),
        in_specs=[
            pl.BlockSpec((1, tm, d), lambda bi, i: (bi, i, 0)),
            pl.BlockSpec((1, d), lambda bi, i: (0, 0)),
            _mod_spec(mod),
            pl.BlockSpec((d, LANES), lambda bi, i: (0, 0)),
            pl.BlockSpec((d, LANES), lambda bi, i: (0, 0)),
            wide,
            wide,
        ],
        out_specs=[pl.BlockSpec((1, tm, d), lambda bi, i: (bi, i, 0)),
                   pl.BlockSpec((1, tm, LANES), lambda bi, i: (bi, i, 0)),
                   wide],
        out_shape=[jax.ShapeDtypeStruct((b, l, d), BF16), jax.ShapeDtypeStruct((b, l, LANES), F32),
                   jax.ShapeDtypeStruct((1, LANES), F32)],
        scratch_shapes=[pltpu.VMEM((1, LANES), F32)],
        compiler_params=_cp(("arbitrary", "arbitrary")),
        name=name,
    )(x, g.reshape(1, d), mod, w_hi, w_lo, b_r, cnt_in)


def _expert_body(be_ref, nu_ref, x_ref, wu_ref, wd_ref, o_ref, wub_ref, wdb_ref):
    i = pl.program_id(0)
    ff = wd_ref.shape[1]

    @pl.when((i == 0) | (be_ref[i] != be_ref[jnp.maximum(i - 1, 0)]))
    def _():
        wub_ref[...] = _bf(wu_ref[0])
        wdb_ref[...] = _bf(wd_ref[0])

    @pl.when(i < nu_ref[0])
    def _():
        u = _dot(x_ref[...], wub_ref[...])
        gate = u[:, :ff]
        act = gate * _sigmoid(gate) * u[:, ff:]
        o_ref[...] = _dot(_bf(act), wdb_ref[...])

    @pl.when(i >= nu_ref[0])
    def _():
        o_ref[...] = jnp.zeros_like(o_ref)


def _experts(xb, blk_expert, n_used, w_up, w_down):
    rows, d = xb.shape
    nb = rows // MOE_BLOCK
    ff2 = w_up.shape[2]
    ff = w_down.shape[1]
    return pl.pallas_call(
        _expert_body,
        grid_spec=pltpu.PrefetchScalarGridSpec(
            num_scalar_prefetch=2,
            grid=(nb,),
            in_specs=[
                pl.BlockSpec((MOE_BLOCK, d), lambda i, be, nu: (i, 0)),
                pl.BlockSpec((1, d, ff2), lambda i, be, nu: (be[i], 0, 0)),
                pl.BlockSpec((1, ff, d), lambda i, be, nu: (be[i], 0, 0)),
            ],
            out_specs=pl.BlockSpec((MOE_BLOCK, d), lambda i, be, nu: (i, 0)),
            scratch_shapes=[pltpu.VMEM((d, ff2), BF16), pltpu.VMEM((ff, d), BF16)],
        ),
        out_shape=jax.ShapeDtypeStruct((rows, d), F32),
        compiler_params=_cp(("arbitrary",)),
        name="moe_experts",
    )(blk_expert, n_used, xb, w_up, w_down)


def _moe_res_body(x_ref, y0_ref, y1_ref, sel_ref, m_ref, o_ref):
    sel = sel_ref[0]
    y = y0_ref[0] * sel[:, 2:3] + y1_ref[0] * sel[:, 3:4]
    o_ref[0] = x_ref[0] + m_ref[0][5:6] * y


def _moe_res(x, y0, y1, sel, mod, *, name):
    b, l, d = x.shape
    tm = _row_tile(l, 512)
    row = pl.BlockSpec((1, tm, d), lambda bi, i: (bi, i, 0))
    return pl.pallas_call(
        _moe_res_body,
        grid=(b, l // tm),
        in_specs=[row, row, row, pl.BlockSpec((1, tm, LANES), lambda bi, i: (bi, i, 0)), _mod_spec(mod)],
        out_specs=row,
        out_shape=jax.ShapeDtypeStruct((b, l, d), F32),
        input_output_aliases={0: 0},
        compiler_params=_cp(("parallel", "parallel")),
        name=name,
    )(x, y0, y1, sel, mod)


def _dispatch_tables(eids, counts):
    flat_e = eids.reshape(-1)
    a = flat_e.shape[0]
    order = jnp.argsort(flat_e)
    start = jnp.cumsum(counts) - counts
    nblk = (counts + MOE_BLOCK - 1) // MOE_BLOCK
    blk_end = jnp.cumsum(nblk)
    blk_start = blk_end - nblk
    nb = -(-a // MOE_BLOCK) + N_EXPERTS
    blk = jnp.arange(nb, dtype=jnp.int32)
    blk_expert = jnp.minimum(jnp.sum((blk[:, None] >= blk_end[None, :]).astype(jnp.int32), axis=1), N_EXPERTS - 1)
    local = jnp.arange(MOE_BLOCK, dtype=jnp.int32)[None, :] + ((blk - blk_start[blk_expert]) * MOE_BLOCK)[:, None]
    sorted_idx = jnp.clip(start[blk_expert][:, None] + local, 0, a - 1)
    src = jnp.where(local < counts[blk_expert][:, None], order[sorted_idx] // 2, 0).reshape(nb * MOE_BLOCK)
    return src, blk_start, blk_expert, blk_end[-1:].astype(jnp.int32)


def _row_positions(sel, blk_start):
    eid = sel[..., 0:2].astype(jnp.int32)
    first = jnp.sum(jnp.where(eid[..., None] == jnp.arange(N_EXPERTS, dtype=jnp.int32), blk_start, 0), axis=-1)
    return first * MOE_BLOCK + sel[..., 4:6].astype(jnp.int32)


def _hier_moe(x, xc, mod, modc, g2, wg, bg, we, be, w_up, w_down, li):
    b, s, d = x.shape
    w_r = jnp.zeros((d, LANES), F32).at[:, :N_GROUPS].set(wg).at[:, N_GROUPS:N_GROUPS + N_EXPERTS].set(we)
    b_r = jnp.zeros((1, LANES), F32).at[0, :N_GROUPS].set(bg).at[0, N_GROUPS:N_GROUPS + N_EXPERTS].set(be)
    w_hi = _bf(w_r)
    w_lo = _bf(w_r - w_hi.astype(F32))
    h, sel, cnt = _router(x, g2, mod, w_hi, w_lo, b_r, jnp.zeros((1, LANES), F32), name=f"router{li}")
    h = h.reshape(b * s, d)
    eids = sel[..., 0:2].reshape(b * s, 2)
    if xc is not None:
        c = xc.shape[1]
        hc, selc, cnt = _router(xc, g2, modc, w_hi, w_lo, b_r, cnt, name=f"router_ctx{li}")
        h = jnp.concatenate([h, hc.reshape(b * c, d)], axis=0)
        eids = jnp.concatenate([eids, selc[..., 0:2].reshape(b * c, 2)], axis=0)
    counts = cnt[0, N_GROUPS:N_GROUPS + N_EXPERTS].astype(jnp.int32)
    src, blk_start, blk_expert, n_used = _dispatch_tables(eids.astype(jnp.int32), counts)
    yb = _experts(h[src], blk_expert, n_used, w_up, w_down)
    pos = _row_positions(sel, blk_start)
    x = _moe_res(x, yb[pos[..., 0]], yb[pos[..., 1]], sel, mod, name=f"moe_res{li}")
    if xc is not None:
        posc = _row_positions(selc, blk_start)
        xc = _moe_res(xc, yb[posc[..., 0]], yb[posc[..., 1]], selc, modc, name=f"moe_res_ctx{li}")
    return x, xc


def _final_body(x_ref, g_ref, o_ref):
    x = x_ref[0]
    ms = jnp.mean(x * x, axis=-1, keepdims=True)
    o_ref[0] = x * lax.rsqrt(ms + NORM_EPS) * g_ref[...]


def _final_norm(x, g):
    b, l, d = x.shape
    tm = _row_tile(l, 512)
    row = pl.BlockSpec((1, tm, d), lambda bi, i: (bi, i, 0))
    return pl.pallas_call(
        _final_body,
        grid=(b, l // tm),
        in_specs=[row, pl.BlockSpec((1, d), lambda bi, i: (0, 0))],
        out_specs=row,
        out_shape=jax.ShapeDtypeStruct((b, l, d), F32),
        compiler_params=_cp(("parallel", "parallel")),
        name="final_norm",
    )(x, g.reshape(1, d))


def _rope_tables(seq):
    rows = seq // GRID_W
    row = jnp.repeat(jnp.arange(rows), GRID_W).astype(F32)
    col = jnp.tile(jnp.arange(GRID_W), rows).astype(F32)
    nf = HEAD_DIM // 4
    inv = ROPE_BASE ** (-jnp.arange(nf, dtype=F32) / nf)
    ar, ac = row[:, None] * inv, col[:, None] * inv
    ang = jnp.concatenate([ar, ar, ac, ac] * (LANES // HEAD_DIM), axis=-1)
    return jnp.cos(ang), jnp.sin(ang)


def kernel(x, c, ctx, c_ctx, mod_w, mod_b, norm_g, final_g, win_w_qkv, win_sink, win_w_o, diff_w_qkv, diff_lambda, diff_subln_g, diff_w_o, pool_w_group, pool_b_group, pool_scale, rwkv_mu, rwkv_w_rkv, rwkv_w0, rwkv_w_a1, rwkv_w_a2, rwkv_a0, rwkv_a_a1, rwkv_a_a2, rwkv_g1, rwkv_g2, rwkv_k_k, rwkv_k_a, rwkv_r_k, rwkv_ln_w, rwkv_ln_b, rwkv_w_o, moe_wg, moe_bg, moe_we, moe_be, moe_w_up, moe_w_down):
    b, s, d = x.shape
    depth = mod_w.shape[0]
    n_mixers = 4
    cos, sin = _rope_tables(s)
    rows = -(-(b + 1) // 8) * 8
    c_all = jnp.zeros((rows, d), F32).at[:b].set(c).at[b].set(c_ctx)
    mods = _mod_all(c_all, mod_w, mod_b).reshape(depth, rows, 6, d)
    xc = ctx
    for i in range(depth):
        m, occ = i % n_mixers, i // n_mixers
        last = i == depth - 1
        mod = mods[i, :b]
        modc = mods[i, b:b + 1]
        g1 = norm_g[i, 0]
        if m == 0:
            nq = win_sink.shape[1] * HEAD_DIM
            nk = WIN_KV_HEADS * HEAD_DIM
            w = _bf(win_w_qkv[occ])
            qkv = _proj(x, g1, mod, w, cos, sin, n_rope=nq + nk, n_q=nq, scale_i=1, shift_i=0, name="win_qkv")
            qkvc = _proj(xc, g1, modc, w, None, None, n_rope=0, n_q=nq, scale_i=1, shift_i=0, name="win_qkv_ctx")
            o = _win_attn(qkv, qkvc, win_sink[occ], local=True, name="win_attn")
            wo = _bf(win_w_o[occ])
            x = _res_proj(o, wo, x, mod, gate_i=2, name="win_out")
            if not last:
                oc = _win_attn(None, qkvc, win_sink[occ], local=False, name="win_attn_ctx")
                xc = _res_proj(oc, wo, xc, modc, gate_i=2, name="win_out_ctx")
        elif m == 1:
            lambda_init = 0.8 - 0.6 * math.exp(-0.3 * i)
            w = _bf(diff_w_qkv[occ])
            qkv = _proj(x, g1, mod, w, cos, sin, n_rope=2 * d, n_q=d, scale_i=1, shift_i=0, name="diff_qkv")
            qkvc = _proj(xc, g1, modc, w, None, None, n_rope=0, n_q=d, scale_i=1, shift_i=0, name="diff_qkv_ctx")
            o = _diff_attn(qkv, qkvc, diff_lambda[occ], diff_subln_g[occ], local=True,
                           lambda_init=lambda_init, name="diff_attn")
            wo = _bf(diff_w_o[occ])
            x = _res_proj(o, wo, x, mod, gate_i=2, name="diff_out")
            if not last:
                oc = _diff_attn(None, qkvc, diff_lambda[occ], diff_subln_g[occ], local=False,
                                lambda_init=lambda_init, name="diff_attn_ctx")
                xc = _res_proj(oc, wo, xc, modc, gate_i=2, name="diff_out_ctx")
        elif m == 2:
            x = _pool_mix(x, g1, mod, pool_w_group[occ], pool_b_group[occ], pool_scale[occ], name="pool")
            if not last:
                xc = _pool_mix(xc, g1, modc, pool_w_group[occ], pool_b_group[occ], pool_scale[occ], name="pool_ctx")
        else:
            p = dict(mu=rwkv_mu[occ], w_rkv=rwkv_w_rkv[occ], w0=rwkv_w0[occ], w_a1=rwkv_w_a1[occ],
                     w_a2=rwkv_w_a2[occ], a0=rwkv_a0[occ], a_a1=rwkv_a_a1[occ], a_a2=rwkv_a_a2[occ],
                     g1=rwkv_g1[occ], g2=rwkv_g2[occ], k_k=rwkv_k_k[occ], k_a=rwkv_k_a[occ])
            assert last, "the context stream's RWKV output path is only needed for non-final layers"
            r, v, kk, lw, bb, kd, g = _rwkv_features(x, g1, mod, p, with_out=True, name="rwkv_feat")
            vc, kkc, lwc, bc, kdc = _rwkv_features(xc, g1, modc, p, with_out=False, name="rwkv_feat_ctx")
            nh = d // HEAD_DIM
            s0 = jnp.zeros((2, b, nh, HEAD_DIM, HEAD_DIM), F32)
            (s_ctx,) = _rwkv_scan(kkc, vc, None, lwc, bc, kdc, s0, name="rwkv_scan_ctx")
            o, _ = _rwkv_scan(kk, v, r, lw, bb, kd, s_ctx, name="rwkv_scan")
            x = _rwkv_output(o, r, v, kd, g, rwkv_r_k[occ], rwkv_ln_w[occ], rwkv_ln_b[occ], rwkv_w_o[occ],
                             x, mod, name="rwkv_out")
        x, xc = _hier_moe(x, None if last else xc, mod, modc, norm_g[i, 1], moe_wg[i], moe_bg[i], moe_we[i],
                          moe_be[i], moe_w_up[i], moe_w_down[i], i)
    return _final_norm(x, final_g)
```

```python
import functools
import math

import jax
import jax.numpy as jnp
from jax import lax
from jax.experimental import pallas as pl
from jax.experimental.pallas import tpu as pltpu

F32 = jnp.float32
BF16 = jnp.bfloat16

HEAD_DIM = 64
GRID_W = 64
ROPE_BASE = 10000.0
NORM_EPS = 1e-6
NEG_INF = -1e30
WIN_KV_HEADS = 4
WIN_BLOCK = 128
POOL_WINDOWS = (2, 4, 8, 16)
POOL_HALO = 8
GN_EPS = 64e-5
N_GROUPS = 4
EXPERTS_PER_GROUP = 8
N_EXPERTS = N_GROUPS * EXPERTS_PER_GROUP
MOE_BLOCK = 512
DIFF_Q_ROWS = 1024
DIFF_Q_SUB = 256
RWKV_CHUNK = 64
LANES = 128
V7X_VMEM_LIMIT = 48 * 1024 * 1024
HI = lax.Precision.HIGHEST


def _cp(sem, vmem=V7X_VMEM_LIMIT):
    return pltpu.CompilerParams(dimension_semantics=sem, vmem_limit_bytes=vmem)


def _bf(x):
    return x.astype(BF16)


def _dot(a, b, precision=None):
    return jnp.dot(a, b, preferred_element_type=F32, precision=precision)


def _dot_t(a, b, precision=None):
    return lax.dot_general(a, b, (((1,), (1,)), ((), ())), preferred_element_type=F32, precision=precision)


def _dot_l(a, b, precision=None):
    return lax.dot_general(a, b, (((0,), (0,)), ((), ())), preferred_element_type=F32, precision=precision)


def _mm(a, b):
    return _dot(_bf(a), _bf(b))


def _mm_t(a, b):
    return _dot_t(_bf(a), _bf(b))


def _mm_l(a, b):
    return _dot_l(_bf(a), _bf(b))


def _split3(x):
    hi = _bf(x)
    r1 = x - hi.astype(F32)
    mid = _bf(r1)
    return hi, mid, _bf(r1 - mid.astype(F32))


def _cumulate(tri, x):
    hi, mid, lo = _split3(x)
    return _dot(tri, hi) + _dot(tri, mid) + _dot(tri, lo)


def _head_sums(x):
    i = lax.broadcasted_iota(jnp.int32, (LANES, LANES), 0) // HEAD_DIM
    j = lax.broadcasted_iota(jnp.int32, (LANES, LANES), 1) // HEAD_DIM
    ones = _bf(jnp.where(i == j, 1.0, 0.0))
    out = []
    for t in range(x.shape[1] // LANES):
        hi, mid, lo = _split3(x[:, t * LANES:(t + 1) * LANES])
        out.append(_dot(hi, ones) + _dot(mid, ones) + _dot(lo, ones))
    return jnp.concatenate(out, axis=1)


def _norm_mod(x, g, scale, shift):
    ms = jnp.mean(x * x, axis=-1, keepdims=True)
    y = x * lax.rsqrt(ms + NORM_EPS) * g
    return y * (1.0 + scale) + shift


def _sigmoid(x):
    return 1.0 / (1.0 + jnp.exp(-x))


def _row_tile(n, pref):
    t = min(pref, n)
    assert n % t == 0
    return t


def _mod_body(c_ref, w_ref, b_ref, o_ref):
    c = c_ref[...]
    s = c * _sigmoid(c)
    o_ref[0] = _dot(_bf(s), _bf(w_ref[0])) + b_ref[0]


def _mod_all(c_all, mod_w, mod_b):
    depth, d, n = mod_w.shape
    r = c_all.shape[0]
    tn = 1536
    return pl.pallas_call(
        _mod_body,
        grid=(depth, n // tn),
        in_specs=[
            pl.BlockSpec((r, d), lambda i, j: (0, 0)),
            pl.BlockSpec((1, d, tn), lambda i, j: (i, 0, j)),
            pl.BlockSpec((1, 1, tn), lambda i, j: (i, 0, j)),
        ],
        out_specs=pl.BlockSpec((1, r, tn), lambda i, j: (i, 0, j)),
        out_shape=jax.ShapeDtypeStruct((depth, r, n), F32),
        compiler_params=_cp(("parallel", "parallel")),
        name="adaln_mod",
    )(c_all, mod_w, mod_b.reshape(depth, 1, n))


def _mod_spec(mod):
    if mod.shape[0] == 1:
        return pl.BlockSpec((1,) + mod.shape[1:], lambda b, i: (0, 0, 0))
    return pl.BlockSpec((1,) + mod.shape[1:], lambda b, i: (b, 0, 0))


def _rope_tile(y, cos, sin, first_half):
    fwd = pltpu.roll(y, LANES - 16, 1)
    bwd = pltpu.roll(y, 16, 1)
    rot = jnp.where(first_half, -fwd, bwd)
    return y * cos + rot * sin


def _proj_body(*refs, n_rope, n_q, scale_i, shift_i):
    if n_rope:
        x_ref, g_ref, m_ref, w_ref, cos_ref, sin_ref, o_ref = refs
    else:
        x_ref, g_ref, m_ref, w_ref, o_ref = refs
    m = m_ref[0]
    h = _norm_mod(x_ref[0], g_ref[...], m[scale_i:scale_i + 1], m[shift_i:shift_i + 1])
    y = _dot(_bf(h), w_ref[...])
    n = y.shape[1]
    if n_rope:
        cos, sin = cos_ref[...], sin_ref[...]
        first_half = (lax.broadcasted_iota(jnp.int32, cos.shape, 1) & 31) < 16
    for j in range(n // LANES):
        blk = y[:, j * LANES:(j + 1) * LANES]
        if j * LANES < n_q:
            blk = blk * (HEAD_DIM ** -0.5)
        if j * LANES < n_rope:
            blk = _rope_tile(blk, cos, sin, first_half)
        o_ref[0, :, j * LANES:(j + 1) * LANES] = _bf(blk)


def _proj(x, g, mod, w, cos, sin, *, n_rope, n_q, scale_i, shift_i, name):
    b, l, d = x.shape
    n = w.shape[1]
    tm = _row_tile(l, 256)
    in_specs = [
        pl.BlockSpec((1, tm, d), lambda bi, i: (bi, i, 0)),
        pl.BlockSpec((1, d), lambda bi, i: (0, 0)),
        _mod_spec(mod),
        pl.BlockSpec((d, n), lambda bi, i: (0, 0)),
    ]
    args = [x, g.reshape(1, d), mod, w]
    if n_rope:
        in_specs += [pl.BlockSpec((tm, LANES), lambda bi, i: (i, 0))] * 2
        args += [cos, sin]
    return pl.pallas_call(
        functools.partial(_proj_body, n_rope=n_rope, n_q=n_q, scale_i=scale_i, shift_i=shift_i),
        grid=(b, l // tm),
        in_specs=in_specs,
        out_specs=pl.BlockSpec((1, tm, n), lambda bi, i: (bi, i, 0)),
        out_shape=jax.ShapeDtypeStruct((b, l, n), BF16),
        compiler_params=_cp(("parallel", "parallel")),
        name=name,
    )(*args)


def _res_body(a_ref, w_ref, x_ref, m_ref, o_ref, *, gate_i):
    y = _dot(a_ref[0], w_ref[...])
    o_ref[0] = x_ref[0] + m_ref[0][gate_i:gate_i + 1] * y


def _res_proj(a, w, x, mod, *, gate_i, name):
    b, l, d = x.shape
    k = a.shape[-1]
    tm = _row_tile(l, 512)
    in_specs = [
        pl.BlockSpec((1, tm, k), lambda bi, i: (bi, i, 0)),
        pl.BlockSpec((k, d), lambda bi, i: (0, 0)),
        pl.BlockSpec((1, tm, d), lambda bi, i: (bi, i, 0)),
        _mod_spec(mod),
    ]
    args = [a, w, x, mod]
    return pl.pallas_call(
        functools.partial(_res_body, gate_i=gate_i),
        grid=(b, l // tm),
        in_specs=in_specs,
        out_specs=pl.BlockSpec((1, tm, d), lambda bi, i: (bi, i, 0)),
        out_shape=jax.ShapeDtypeStruct((b, l, d), F32),
        input_output_aliases={2: 0},
        compiler_params=_cp(("parallel", "parallel")),
        name=name,
    )(*args)


def _win_body(*refs, seq, local, n_heads):
    if local:
        q_ref, kp_ref, kc_ref, kn_ref, vp_ref, vc_ref, vn_ref, kx_ref, vx_ref, sink_ref, o_ref = refs
    else:
        q_ref, kx_ref, vx_ref, sink_ref, o_ref = refs
    tq = q_ref.shape[1]
    grp = n_heads // WIN_KV_HEADS
    n = pl.program_id(1)
    if local:
        span = 3 * tq
        row = lax.broadcasted_iota(jnp.int32, (grp * tq, span), 0) & (tq - 1)
        col = lax.broadcasted_iota(jnp.int32, (grp * tq, span), 1)
        rel = col - tq - row
        key_pos = n * tq - tq + col
        mask = (jnp.abs(rel) <= tq) & (key_pos >= 0) & (key_pos < seq)
    hks = range(WIN_KV_HEADS)
    ks = [slice(hk * HEAD_DIM, (hk + 1) * HEAD_DIM) for hk in hks]
    qh = [jnp.concatenate(
        [q_ref[0, :, (hk * grp + g) * HEAD_DIM:(hk * grp + g + 1) * HEAD_DIM] for g in range(grp)], axis=0)
        for hk in hks]
    sink = [jnp.concatenate([jnp.full((tq, 1), sink_ref[hk * grp + g], F32) for g in range(grp)], axis=0)
            for hk in hks]
    s_ctx = [_dot_t(qh[hk], kx_ref[0, :, ks[hk]]) for hk in hks]
    m = [jnp.maximum(jnp.max(s_ctx[hk], axis=1, keepdims=True), sink[hk]) for hk in hks]
    if local:
        s_loc = [jnp.where(mask, _dot_t(qh[hk], jnp.concatenate(
            [kp_ref[0, :, ks[hk]], kc_ref[0, :, ks[hk]], kn_ref[0, :, ks[hk]]], axis=0)), NEG_INF) for hk in hks]
        m = [jnp.maximum(m[hk], jnp.max(s_loc[hk], axis=1, keepdims=True)) for hk in hks]
    p_ctx = [jnp.exp(s_ctx[hk] - m[hk]) for hk in hks]
    den = [jnp.sum(p_ctx[hk], axis=1, keepdims=True) + jnp.exp(sink[hk] - m[hk]) for hk in hks]
    o = [_dot(_bf(p_ctx[hk]), vx_ref[0, :, ks[hk]]) for hk in hks]
    if local:
        p_loc = [jnp.exp(s_loc[hk] - m[hk]) for hk in hks]
        den = [den[hk] + jnp.sum(p_loc[hk], axis=1, keepdims=True) for hk in hks]
        o = [o[hk] + _dot(_bf(p_loc[hk]), jnp.concatenate(
            [vp_ref[0, :, ks[hk]], vc_ref[0, :, ks[hk]], vn_ref[0, :, ks[hk]]], axis=0)) for hk in hks]
    for hk in hks:
        oh = o[hk] / den[hk]
        for g in range(grp):
            hq = hk * grp + g
            o_ref[0, :, hq * HEAD_DIM:(hq + 1) * HEAD_DIM] = _bf(oh[g * tq:(g + 1) * tq])


def _win_attn(qkv, qkvc, sink, *, local, name):
    src = qkv if local else qkvc
    b, l, _ = src.shape
    c = qkvc.shape[1]
    n_heads = sink.shape[0]
    d = n_heads * HEAD_DIM
    kvw = WIN_KV_HEADS * HEAD_DIM
    kcol = d // kvw
    tq = WIN_BLOCK
    nb = l // tq
    in_specs = [pl.BlockSpec((1, tq, d), lambda bi, i: (bi, i, 0))]
    args = [src]
    if local:
        for colb in (kcol, kcol + 1):
            in_specs += [
                pl.BlockSpec((1, tq, kvw), lambda bi, i, colb=colb: (bi, jnp.maximum(i - 1, 0), colb)),
                pl.BlockSpec((1, tq, kvw), lambda bi, i, colb=colb: (bi, i, colb)),
                pl.BlockSpec((1, tq, kvw), lambda bi, i, colb=colb: (bi, jnp.minimum(i + 1, nb - 1), colb)),
            ]
            args += [qkv, qkv, qkv]
    in_specs += [
        pl.BlockSpec((1, c, kvw), lambda bi, i: (bi, 0, kcol)),
        pl.BlockSpec((1, c, kvw), lambda bi, i: (bi, 0, kcol + 1)),
        pl.BlockSpec(memory_space=pltpu.SMEM),
    ]
    args += [qkvc, qkvc, sink]
    return pl.pallas_call(
        functools.partial(_win_body, seq=l, local=local, n_heads=n_heads),
        grid=(b, nb),
        in_specs=in_specs,
        out_specs=pl.BlockSpec((1, tq, d), lambda bi, i: (bi, i, 0)),
        out_shape=jax.ShapeDtypeStruct((b, l, d), BF16),
        compiler_params=_cp(("parallel", "parallel")),
        name=name,
    )(*args)


def _diff_body(*refs, n_lat, tk, sub, lambda_init):
    if n_lat:
        lam_ref, g_ref, q_ref, kl_ref, vl_ref, kx_ref, vx_ref, o_ref, s_ref = refs
    else:
        lam_ref, g_ref, q_ref, kx_ref, vx_ref, o_ref, s_ref = refs
    hw = 2 * HEAD_DIM
    n_sub = q_ref.shape[1] // sub
    lane = lax.broadcasted_iota(jnp.int32, (sub, hw), 1)
    chunks = [(kl_ref, vl_ref, i * tk, tk, i * tk) for i in range(n_lat)] if n_lat else []
    chunks.append((kx_ref, vx_ref, 0, kx_ref.shape[1], n_lat * tk))
    lam = lam_ref[...]
    lam_full = (jnp.exp(jnp.sum(lam[0:1] * lam[1:2], axis=1, keepdims=True))
                - jnp.exp(jnp.sum(lam[2:3] * lam[3:4], axis=1, keepdims=True)) + lambda_init)

    def stacked_q(j):
        q = q_ref[0, j * sub:(j + 1) * sub, :]
        zero = jnp.zeros_like(q)
        return jnp.concatenate([jnp.where(lane < HEAD_DIM, q, zero), jnp.where(lane >= HEAD_DIM, q, zero)], axis=0)

    def score_chunk(j, qq, ch, mx):
        k_ref, _, row, n, col = ch
        s = _dot_t(qq, k_ref[0, row:row + n, :])
        s_ref[j % 2, :, col:col + n] = s
        for t in range(n // LANES):
            mx = jnp.maximum(mx, s[:, t * LANES:(t + 1) * LANES])
        return mx

    def value_chunk(j, m, ch, acc):
        _, v_ref, row, n, col = ch
        p = jnp.concatenate(
            [jnp.exp(s_ref[j % 2, :, col + t * LANES:col + (t + 1) * LANES] - m) for t in range(n // LANES)], axis=1)
        v = v_ref[0, row:row + n, :]
        return acc + _dot(_bf(p), jnp.concatenate([v, jnp.ones_like(v)], axis=1))

    def finish(j, acc):
        a = acc[:, :hw] / acc[:, hw:]
        o = a[:sub] - lam_full * a[sub:]
        ms = jnp.mean(o * o, axis=-1, keepdims=True)
        o = o * lax.rsqrt(ms + NORM_EPS) * g_ref[...] * (1.0 - lambda_init)
        o_ref[0, j * sub:(j + 1) * sub, :] = _bf(o)

    m_prev = None
    for j in range(n_sub + 1):
        if j < n_sub:
            qq = stacked_q(j)
            mx = jnp.full((2 * sub, LANES), NEG_INF, F32)
        if j > 0:
            acc = jnp.zeros((2 * sub, 2 * hw), F32)
        for ch in chunks:
            if j > 0:
                acc = value_chunk(j - 1, m_prev, ch, acc)
            if j < n_sub:
                mx = score_chunk(j, qq, ch, mx)
        if j > 0:
            finish(j - 1, acc)
        if j < n_sub:
            m_prev = jnp.broadcast_to(jnp.max(mx, axis=1, keepdims=True), (2 * sub, LANES))


def _diff_attn(qkv, qkvc, lam, subln_g, *, local, lambda_init, name):
    src = qkv if local else qkvc
    b, l, n3 = src.shape
    d = n3 // 3
    c = qkvc.shape[1]
    hw = 2 * HEAD_DIM
    nh = d // hw
    tq = _row_tile(l, DIFF_Q_ROWS)
    sub = min(DIFF_Q_SUB, tq)
    tk = 512
    in_specs = [
        pl.BlockSpec((4, HEAD_DIM), lambda bi, h, i: (0, 0)),
        pl.BlockSpec((1, hw), lambda bi, h, i: (0, 0)),
        pl.BlockSpec((1, tq, hw), lambda bi, h, i: (bi, i, h)),
    ]
    args = [lam, subln_g.reshape(1, hw), src]
    n_lat = 0
    if local:
        s = qkv.shape[1]
        n_lat = s // tk
        in_specs += [
            pl.BlockSpec((1, s, hw), lambda bi, h, i: (bi, 0, nh + h)),
            pl.BlockSpec((1, s, hw), lambda bi, h, i: (bi, 0, 2 * nh + h)),
        ]
        args += [qkv, qkv]
    in_specs += [
        pl.BlockSpec((1, c, hw), lambda bi, h, i: (bi, 0, nh + h)),
        pl.BlockSpec((1, c, hw), lambda bi, h, i: (bi, 0, 2 * nh + h)),
    ]
    args += [qkvc, qkvc]
    return pl.pallas_call(
        functools.partial(_diff_body, n_lat=n_lat, tk=tk, sub=sub, lambda_init=lambda_init),
        grid=(b, nh, l // tq),
        in_specs=in_specs,
        out_specs=pl.BlockSpec((1, tq, hw), lambda bi, h, i: (bi, i, h)),
        out_shape=jax.ShapeDtypeStruct((b, l, d), BF16),
        scratch_shapes=[pltpu.VMEM((2, 2 * sub, n_lat * tk + c), F32)],
        compiler_params=_cp(("parallel", "parallel", "parallel")),
        name=name,
    )(*args)


def _halo_specs(l, tm, d):
    nh = l // POOL_HALO
    per = tm // POOL_HALO
    return [
        pl.BlockSpec((1, POOL_HALO, d), lambda bi, i: (bi, jnp.maximum(i * per - 1, 0), 0)),
        pl.BlockSpec((1, tm, d), lambda bi, i: (bi, i, 0)),
        pl.BlockSpec((1, POOL_HALO, d), lambda bi, i: (bi, jnp.minimum((i + 1) * per, nh - 1), 0)),
    ]


def _fill_normed(h_ref, xp_ref, x_ref, xn_ref, g, scale, shift):
    tm = x_ref.shape[1]
    h_ref[0:POOL_HALO, :] = _norm_mod(xp_ref[0], g, scale, shift)
    h_ref[POOL_HALO:POOL_HALO + tm, :] = _norm_mod(x_ref[0], g, scale, shift)
    h_ref[POOL_HALO + tm:2 * POOL_HALO + tm, :] = _norm_mod(xn_ref[0], g, scale, shift)


def _pool_body(xp_ref, x_ref, xn_ref, g_ref, m_ref, w_ref, b_ref, s_ref, o_ref, h_ref, *, seq):
    tm = x_ref.shape[1]
    d = x_ref.shape[2]
    gd = d // len(POOL_WINDOWS)
    m = m_ref[0]
    _fill_normed(h_ref, xp_ref, x_ref, xn_ref, g_ref[...], m[1:2], m[0:1])
    pos = pl.program_id(1) * tm + lax.broadcasted_iota(jnp.int32, (tm, 1), 0)
    x = x_ref[0]
    for gi, w in enumerate(POOL_WINDOWS):
        cols = slice(gi * gd, (gi + 1) * gd)
        acc = jnp.zeros((tm, gd), F32)
        for off in range(-(w // 2), w - w // 2):
            valid = (pos + off >= 0) & (pos + off < seq)
            acc = acc + jnp.where(valid, h_ref[POOL_HALO + off:POOL_HALO + off + tm, cols], 0.0)
        lo = jnp.maximum(pos - w // 2, 0)
        hi = jnp.minimum(pos + w - w // 2, seq)
        y = acc / (hi - lo).astype(F32) - h_ref[POOL_HALO:POOL_HALO + tm, cols]
        y = (_dot(_bf(y), w_ref[gi]) + b_ref[:, cols]) * s_ref[:, cols]
        o_ref[0, :, cols] = x[:, cols] + m[2:3, cols] * y


def _pool_mix(x, g, mod, w_group, b_group, layer_scale, *, name):
    b, l, d = x.shape
    tm = _row_tile(l, 256)
    gd = d // len(POOL_WINDOWS)
    return pl.pallas_call(
        functools.partial(_pool_body, seq=l),
        grid=(b, l // tm),
        in_specs=_halo_specs(l, tm, d) + [
            pl.BlockSpec((1, d), lambda bi, i: (0, 0)),
            _mod_spec(mod),
            pl.BlockSpec((len(POOL_WINDOWS), gd, gd), lambda bi, i: (0, 0, 0)),
            pl.BlockSpec((1, d), lambda bi, i: (0, 0)),
            pl.BlockSpec((1, d), lambda bi, i: (0, 0)),
        ],
        out_specs=pl.BlockSpec((1, tm, d), lambda bi, i: (bi, i, 0)),
        out_shape=jax.ShapeDtypeStruct((b, l, d), F32),
        scratch_shapes=[pltpu.VMEM((tm + 2 * POOL_HALO, d), F32)],
        compiler_params=_cp(("parallel", "parallel")),
        name=name,
    )(x, x, x, g.reshape(1, d), mod, _bf(w_group), b_group.reshape(1, d), layer_scale.reshape(1, d))


def _softplus(z):
    return jnp.maximum(z, 0.0) + jnp.log1p(jnp.exp(-jnp.abs(z)))


def _rwkv_feat_body(*refs, seq, with_out):
    (xp_ref, x_ref, xn_ref, g_ref, m_ref, mu_ref, wrkv_ref, w0_ref, wa1_ref, wa2_ref, a0_ref, aa1_ref,
     aa2_ref, g1_ref, g2_ref, kk_ref, ka_ref) = refs[:17]
    if with_out:
        r_out, v_out, kk_out, lw_out, b_out, kd_out, g_out, h_ref = refs[17:]
    else:
        v_out, kk_out, lw_out, b_out, kd_out, h_ref = refs[17:]
    tm = x_ref.shape[1]
    m = m_ref[0]
    _fill_normed(h_ref, xp_ref, x_ref, xn_ref, g_ref[...], m[1:2], m[0:1])
    pos = pl.program_id(1) * tm + lax.broadcasted_iota(jnp.int32, (tm, 1), 0)
    t = h_ref[POOL_HALO:POOL_HALO + tm, :]
    dp = jnp.where(pos >= 1, h_ref[POOL_HALO - 1:POOL_HALO - 1 + tm, :], 0.0) - t
    dn = jnp.where(pos < seq - 1, h_ref[POOL_HALO + 1:POOL_HALO + 1 + tm, :], 0.0) - t

    def mix(i):
        return _bf(t + dp * mu_ref[0, i:i + 1, :] + dn * mu_ref[1, i:i + 1, :])

    k = _dot(mix(2), wrkv_ref[1])
    v_out[0] = _dot(mix(3), wrkv_ref[2])
    kk = k * kk_ref[...]
    kk = kk / jnp.maximum(jnp.sqrt(_head_sums(kk * kk)), 1e-12)
    kk_out[0] = kk
    xw, xa = mix(1), mix(4)
    for di in range(2):
        z = w0_ref[di:di + 1, :] + _dot(_bf(jnp.tanh(_dot(xw, wa1_ref[di]))), wa2_ref[di])
        w = -_softplus(-z) - 0.5
        lw_out[di, 0] = -jnp.exp(w)
        a = _sigmoid(a0_ref[di:di + 1, :] + _dot(_bf(_dot(xa, aa1_ref[di])), aa2_ref[di]))
        kd_out[di, 0] = k * (1.0 + (a - 1.0) * ka_ref[...])
        b_out[di, 0] = kk * a
    if with_out:
        r_out[0] = _dot(mix(0), wrkv_ref[0])
        g_out[0] = _dot(_bf(_sigmoid(_dot(mix(5), g1_ref[...]))), g2_ref[...])


def _rwkv_features(x, g, mod, p, *, with_out, name):
    b, l, d = x.shape
    tm = _row_tile(l, 128)
    const = lambda a: pl.BlockSpec(a.shape, lambda bi, i, nd=a.ndim: (0,) * nd)
    weights = [p["mu"], _bf(p["w_rkv"]), p["w0"], _bf(p["w_a1"]), _bf(p["w_a2"]), p["a0"], _bf(p["a_a1"]),
               _bf(p["a_a2"]), _bf(p["g1"]), _bf(p["g2"]), p["k_k"].reshape(1, d), p["k_a"].reshape(1, d)]
    one = pl.BlockSpec((1, tm, d), lambda bi, i: (bi, i, 0))
    two = pl.BlockSpec((2, 1, tm, d), lambda bi, i: (0, bi, i, 0))
    s1 = jax.ShapeDtypeStruct((b, l, d), F32)
    s2 = jax.ShapeDtypeStruct((2, b, l, d), F32)
    out_specs = [one, one, two, two, two]
    out_shape = [s1, s1, s2, s2, s2]
    if with_out:
        out_specs = [one] + out_specs + [one]
        out_shape = [s1] + out_shape + [s1]
    return pl.pallas_call(
        functools.partial(_rwkv_feat_body, seq=l, with_out=with_out),
        grid=(b, l // tm),
        in_specs=_halo_specs(l, tm, d) + [pl.BlockSpec((1, d), lambda bi, i: (0, 0)), _mod_spec(mod)]
        + [const(a) for a in weights],
        out_specs=out_specs,
        out_shape=out_shape,
        scratch_shapes=[pltpu.VMEM((tm + 2 * POOL_HALO, d), F32)],
        compiler_params=_cp(("parallel", "parallel")),
        name=name,
    )(x, x, x, g.reshape(1, d), mod, *weights)


def _scan_body(*refs, emit, n_chunks):
    if emit:
        kk_ref, v_ref, r_ref, lw_ref, b_ref, kd_ref, s0_ref, o_ref, sfin_ref, s_ref = refs
    else:
        kk_ref, v_ref, lw_ref, b_ref, kd_ref, s0_ref, sfin_ref, s_ref = refs
    hb = s_ref.shape[0]
    L = kk_ref.shape[1]
    rev = pl.program_id(0) == 1
    c = pl.program_id(2)

    @pl.when(c == 0)
    def _():
        s_ref[...] = s0_ref[0, 0]

    row = lax.broadcasted_iota(jnp.int32, (L, L), 0)
    col = lax.broadcasted_iota(jnp.int32, (L, L), 1)
    flip = rev.astype(jnp.int32)
    p_row = row + flip * (L - 1 - 2 * row)
    p_col = col + flip * (L - 1 - 2 * col)
    incl = p_col <= p_row
    strict = p_col < p_row
    tri = _bf(incl.astype(F32))
    eye = row == col
    assert L == HEAD_DIM
    levels = []
    m = 1
    while m < L:
        same = (p_row // (2 * m)) == (p_col // (2 * m))
        levels.append(same & ((p_row & (2 * m - 1)) >= m) & ((p_col & (2 * m - 1)) < m))
        m *= 2

    hs = range(hb)
    cut = lambda a: [a[:, h * HEAD_DIM:(h + 1) * HEAD_DIM] for h in hs]
    lw_all = lw_ref[0, 0]
    b_all = b_ref[0, 0]
    kd_all = kd_ref[0, 0]
    cl_all = _cumulate(tri, lw_all)
    tot_all = jnp.sum(lw_all, axis=0, keepdims=True)
    w_inv = jnp.exp(-cl_all)
    w_last = jnp.exp(tot_all - cl_all)
    kh = cut(kk_ref[0] * jnp.exp(cl_all - lw_all))
    bt, kt = cut(b_all * w_inv), cut(kd_all * w_inv)
    bt_l, kt_l = cut(b_all * w_last), cut(kd_all * w_last)
    w_tot = cut(jnp.exp(tot_all))
    v = cut(v_ref[0])
    s0 = [s_ref[h] for h in hs]
    rhs = [jnp.concatenate([bt[h], kt[h]], axis=0) for h in hs]
    if emit:
        rh = cut(r_ref[0] * jnp.exp(cl_all))
        lhs = [jnp.concatenate([kh[h], rh[h]], axis=0) for h in hs]
    else:
        lhs = kh
    big = [_mm_t(lhs[h], rhs[h]) for h in hs]
    m_b = [big[h][:L, :L] for h in hs]
    mkv = [_mm(jnp.where(strict, big[h][:L, L:], 0.0), v[h]) for h in hs]
    t = [jnp.where(eye, 1.0, 0.0) - jnp.where(levels[0], m_b[h], 0.0) for h in hs]
    for lm in levels[1:]:
        tl = [_mm(t[h], jnp.where(lm, m_b[h], 0.0)) for h in hs]
        t = [t[h] - _mm(tl[h], t[h]) for h in hs]
    y = [_mm(t[h], jnp.concatenate([kh[h], mkv[h]], axis=1)) for h in hs]
    gh = [_mm_l(y[h], bt_l[h]) for h in hs]
    vk = [_mm_l(v[h], kt_l[h]) for h in hs]
    g_mat = [jnp.where(eye, w_tot[h], 0.0) - gh[h][:HEAD_DIM] for h in hs]
    if emit:
        ab_y = [_mm(jnp.where(incl, big[h][L:, :L], 0.0), y[h]) for h in hs]
        akv = [_mm(jnp.where(incl, big[h][L:, L:], 0.0), v[h]) for h in hs]
        rs = [_mm_t(rh[h] - ab_y[h][:, :HEAD_DIM], s0[h]) for h in hs]
        o_ref[0, 0] = jnp.concatenate([rs[h] + akv[h] - ab_y[h][:, HEAD_DIM:] for h in hs], axis=1)
    sg = [_mm(s0[h], g_mat[h]) for h in hs]
    for h in hs:
        s_ref[h] = sg[h] + vk[h] - gh[h][HEAD_DIM:]

    @pl.when(c == n_chunks - 1)
    def _():
        sfin_ref[0, 0] = s_ref[...]


def _rwkv_scan(kk, v, r, lw, b, kd, s0, *, name):
    bsz, l, d = kk.shape
    dh = HEAD_DIM
    nh = d // dh
    L = RWKV_CHUNK
    nch = l // L
    emit = r is not None

    def chunk(di, c):
        return c + di * (nch - 1 - 2 * c)

    one = pl.BlockSpec((1, L, d), lambda di, bi, c: (bi, chunk(di, c), 0))
    two = pl.BlockSpec((1, 1, L, d), lambda di, bi, c: (di, bi, chunk(di, c), 0))
    st = pl.BlockSpec((1, 1, nh, dh, dh), lambda di, bi, c: (di, bi, 0, 0, 0))
    in_specs = [one, one] + ([one] if emit else []) + [two, two, two, st]
    args = [kk, v] + ([r] if emit else []) + [lw, b, kd, s0]
    s_shape = jax.ShapeDtypeStruct((2, bsz, nh, dh, dh), F32)
    if emit:
        out_specs = [two, st]
        out_shape = [jax.ShapeDtypeStruct((2, bsz, l, d), F32), s_shape]
    else:
        out_specs = [st]
        out_shape = [s_shape]
    return pl.pallas_call(
        functools.partial(_scan_body, emit=emit, n_chunks=nch),
        grid=(2, bsz, nch),
        in_specs=in_specs,
        out_specs=out_specs,
        out_shape=out_shape,
        scratch_shapes=[pltpu.VMEM((nh, dh, dh), F32)],
        compiler_params=_cp(("parallel", "parallel", "arbitrary")),
        name=name,
    )(*args)


def _rwkv_out_body(o_ref, r_ref, v_ref, kd_ref, g_ref, rk_ref, lnw_ref, lnb_ref, w_ref, x_ref, m_ref, out_ref):
    o = o_ref[0, 0] + o_ref[1, 0]
    mean = _head_sums(o) * (1.0 / HEAD_DIM)
    cen = o - mean
    var = _head_sums(cen * cen) * (1.0 / HEAD_DIM)
    on = cen * lax.rsqrt(var + GN_EPS) * lnw_ref[...] + lnb_ref[...]
    bonus = _head_sums(r_ref[0] * (kd_ref[0, 0] + kd_ref[1, 0]) * rk_ref[...]) * v_ref[0]
    a = (on + bonus) * g_ref[0]
    out_ref[0] = x_ref[0] + m_ref[0][2:3] * _dot(_bf(a), w_ref[...])


def _rwkv_output(o, r, v, kd, g, r_k, ln_w, ln_b, w_o, x, mod, *, name):
    b, l, d = x.shape
    tm = _row_tile(l, 256)
    row = pl.BlockSpec((1, tm, d), lambda bi, i: (bi, i, 0))
    row2 = pl.BlockSpec((2, 1, tm, d), lambda bi, i: (0, bi, i, 0))
    small = pl.BlockSpec((1, d), lambda bi, i: (0, 0))
    return pl.pallas_call(
        _rwkv_out_body,
        grid=(b, l // tm),
        in_specs=[row2, row, row, row2, row, small, small, small,
                  pl.BlockSpec((d, d), lambda bi, i: (0, 0)), row, _mod_spec(mod)],
        out_specs=row,
        out_shape=jax.ShapeDtypeStruct((b, l, d), F32),
        input_output_aliases={9: 0},
        compiler_params=_cp(("parallel", "parallel")),
        name=name,
    )(o, r, v, kd, g, r_k.reshape(1, d), ln_w.reshape(1, d), ln_b.reshape(1, d), _bf(w_o), x, mod)


def _router_body(x_ref, g_ref, m_ref, whi_ref, wlo_ref, b_ref, cin_ref, h_ref, sel_ref, cnt_ref, run_ref):
    @pl.when((pl.program_id(0) == 0) & (pl.program_id(1) == 0))
    def _():
        run_ref[...] = cin_ref[...]

    m = m_ref[0]
    h = _norm_mod(x_ref[0], g_ref[...], m[4:5], m[3:4])
    hi = _bf(h)
    h_ref[0] = hi
    lo_part = _bf(h - hi.astype(F32))
    lg = _dot(hi, whi_ref[...]) + _dot(hi, wlo_ref[...]) + _dot(lo_part, whi_ref[...]) + b_ref[...]
    lane = lax.broadcasted_iota(jnp.int32, lg.shape, 1)
    big = jnp.int32(1 << 20)
    g_logit = jnp.where(lane < N_GROUPS, lg, -jnp.inf)
    g_max = jnp.max(g_logit, axis=1, keepdims=True)
    gsel = jnp.min(jnp.where(g_logit == g_max, lane, big), axis=1, keepdims=True)
    p_grp = 1.0 / jnp.sum(jnp.exp(g_logit - g_max), axis=1, keepdims=True)
    lo = N_GROUPS + gsel * EXPERTS_PER_GROUP
    e_logit = jnp.where((lane >= lo) & (lane < lo + EXPERTS_PER_GROUP), lg, -jnp.inf)
    e_max = jnp.max(e_logit, axis=1, keepdims=True)
    i1 = jnp.min(jnp.where(e_logit == e_max, lane, big), axis=1, keepdims=True)
    rest = jnp.where(lane == i1, -jnp.inf, e_logit)
    e2 = jnp.max(rest, axis=1, keepdims=True)
    i2 = jnp.min(jnp.where(rest == e2, lane, big), axis=1, keepdims=True)
    q2 = jnp.exp(e2 - e_max)
    w1 = p_grp / (1.0 + q2)
    w2 = p_grp * q2 / (1.0 + q2)
    pick1, pick2 = lane == i1, lane == i2
    both = jnp.where(pick1 | pick2, 1.0, 0.0)
    tm = lg.shape[0]
    earlier = lax.broadcasted_iota(jnp.int32, (tm, tm), 1) < lax.broadcasted_iota(jnp.int32, (tm, tm), 0)
    ahead = _dot(_bf(earlier.astype(F32)), _bf(both)) + run_ref[...]
    r1 = jnp.sum(jnp.where(pick1, ahead, 0.0), axis=1, keepdims=True)
    r2 = jnp.sum(jnp.where(pick2, ahead, 0.0), axis=1, keepdims=True)
    run_ref[...] = run_ref[...] + jnp.sum(both, axis=0, keepdims=True)
    cnt_ref[...] = run_ref[...]
    cols = ((i1 - N_GROUPS).astype(F32), (i2 - N_GROUPS).astype(F32), w1, w2, r1, r2)
    sel = jnp.zeros_like(lg)
    for j, val in enumerate(cols):
        sel = jnp.where(lane == j, val, sel)
    sel_ref[0] = sel


def _router(x, g, mod, w_hi, w_lo, b_r, cnt_in, *, name):
    b, l, d = x.shape
    tm = _row_tile(l, 256)
    wide = pl.BlockSpec((1, LANES), lambda bi, i: (0, 0))
    return pl.pallas_call(
        _router_body,
        grid=(b, l // tm),
        in_specs=[
            pl.BlockSpec((1, tm, d), lambda bi, i: (bi, i, 0)),
            pl.BlockSpec((1, d), lambda bi, i: (0, 0)),
            _mod_spec(mod),
            pl.BlockSpec((d, LANES), lambda bi, i: (0, 0)),
            pl.BlockSpec((d, LANES), lambda bi, i: (0, 0)),
            wide,
            wide,
        ],
        out_specs=[pl.BlockSpec((1, tm, d), lambda bi, i: (bi, i, 0)),
                   pl.BlockSpec((1, tm, LANES), lambda bi, i: (bi, i, 0)),
                   wide],
        out_shape=[jax.ShapeDtypeStruct((b, l, d), BF16), jax.ShapeDtypeStruct((b, l, LANES), F32),
                   jax.ShapeDtypeStruct((1, LANES), F32)],
        scratch_shapes=[pltpu.VMEM((1, LANES), F32)],
        compiler_params=_cp(("arbitrary", "arbitrary")),
        name=name,
    )(x, g.reshape(1, d), mod, w_hi, w_lo, b_r, cnt_in)


def _expert_body(be_ref, nu_ref, x_ref, wu_ref, wd_ref, o_ref, wub_ref, wdb_ref):
    i = pl.program_id(0)
    ff = wd_ref.shape[1]

    @pl.when((i == 0) | (be_ref[i] != be_ref[jnp.maximum(i - 1, 0)]))
    def _():
        wub_ref[...] = _bf(wu_ref[0])
        wdb_ref[...] = _bf(wd_ref[0])

    @pl.when(i < nu_ref[0])
    def _():
        u = _dot(x_ref[...], wub_ref[...])
        gate = u[:, :ff]
        act = gate * _sigmoid(gate) * u[:, ff:]
        o_ref[...] = _dot(_bf(act), wdb_ref[...])

    @pl.when(i >= nu_ref[0])
    def _():
        o_ref[...] = jnp.zeros_like(o_ref)


def _experts(xb, blk_expert, n_used, w_up, w_down):
    rows, d = xb.shape
    nb = rows // MOE_BLOCK
    ff2 = w_up.shape[2]
    ff = w_down.shape[1]
    return pl.pallas_call(
        _expert_body,
        grid_spec=pltpu.PrefetchScalarGridSpec(
            num_scalar_prefetch=2,
            grid=(nb,),
            in_specs=[
                pl.BlockSpec((MOE_BLOCK, d), lambda i, be, nu: (i, 0)),
                pl.BlockSpec((1, d, ff2), lambda i, be, nu: (be[i], 0, 0)),
                pl.BlockSpec((1, ff, d), lambda i, be, nu: (be[i], 0, 0)),
            ],
            out_specs=pl.BlockSpec((MOE_BLOCK, d), lambda i, be, nu: (i, 0)),
            scratch_shapes=[pltpu.VMEM((d, ff2), BF16), pltpu.VMEM((ff, d), BF16)],
        ),
        out_shape=jax.ShapeDtypeStruct((rows, d), F32),
        compiler_params=_cp(("arbitrary",)),
        name="moe_experts",
    )(blk_expert, n_used, xb, w_up, w_down)


def _moe_res_body(x_ref, y0_ref, y1_ref, sel_ref, m_ref, o_ref):
    sel = sel_ref[0]
    y = y0_ref[0] * sel[:, 2:3] + y1_ref[0] * sel[:, 3:4]
    o_ref[0] = x_ref[0] + m_ref[0][5:6] * y


def _moe_res(x, y0, y1, sel, mod, *, name):
    b, l, d = x.shape
    tm = _row_tile(l, 512)
    row = pl.BlockSpec((1, tm, d), lambda bi, i: (bi, i, 0))
    return pl.pallas_call(
        _moe_res_body,
        grid=(b, l // tm),
        in_specs=[row, row, row, pl.BlockSpec((1, tm, LANES), lambda bi, i: (bi, i, 0)), _mod_spec(mod)],
        out_specs=row,
        out_shape=jax.ShapeDtypeStruct((b, l, d), F32),
        input_output_aliases={0: 0},
        compiler_params=_cp(("parallel", "parallel")),
        name=name,
    )(x, y0, y1, sel, mod)


def _dispatch_tables(eids, counts):
    flat_e = eids.reshape(-1)
    a = flat_e.shape[0]
    order = jnp.argsort(flat_e)
    start = jnp.cumsum(counts) - counts
    nblk = (counts + MOE_BLOCK - 1) // MOE_BLOCK
    blk_end = jnp.cumsum(nblk)
    blk_start = blk_end - nblk
    nb = -(-a // MOE_BLOCK) + N_EXPERTS
    blk = jnp.arange(nb, dtype=jnp.int32)
    blk_expert = jnp.minimum(jnp.sum((blk[:, None] >= blk_end[None, :]).astype(jnp.int32), axis=1), N_EXPERTS - 1)
    local = jnp.arange(MOE_BLOCK, dtype=jnp.int32)[None, :] + ((blk - blk_start[blk_expert]) * MOE_BLOCK)[:, None]
    sorted_idx = jnp.clip(start[blk_expert][:, None] + local, 0, a - 1)
    src = jnp.where(local < counts[blk_expert][:, None], order[sorted_idx] // 2, 0).reshape(nb * MOE_BLOCK)
    return src, blk_start, blk_expert, blk_end[-1:].astype(jnp.int32)


def _row_positions(sel, blk_start):
    eid = sel[..., 0:2].astype(jnp.int32)
    first = jnp.sum(jnp.where(eid[..., None] == jnp.arange(N_EXPERTS, dtype=jnp.int32), blk_start, 0), axis=-1)
    return first * MOE_BLOCK + sel[..., 4:6].astype(jnp.int32)


def _hier_moe(x, xc, mod, modc, g2, wg, bg, we, be, w_up, w_down, li):
    b, s, d = x.shape
    w_r = jnp.zeros((d, LANES), F32).at[:, :N_GROUPS].set(wg).at[:, N_GROUPS:N_GROUPS + N_EXPERTS].set(we)
    b_r = jnp.zeros((1, LANES), F32).at[0, :N_GROUPS].set(bg).at[0, N_GROUPS:N_GROUPS + N_EXPERTS].set(be)
    w_hi = _bf(w_r)
    w_lo = _bf(w_r - w_hi.astype(F32))
    h, sel, cnt = _router(x, g2, mod, w_hi, w_lo, b_r, jnp.zeros((1, LANES), F32), name=f"router{li}")
    h = h.reshape(b * s, d)
    eids = sel[..., 0:2].reshape(b * s, 2)
    if xc is not None:
        c = xc.shape[1]
        hc, selc, cnt = _router(xc, g2, modc, w_hi, w_lo, b_r, cnt, name=f"router_ctx{li}")
        h = jnp.concatenate([h, hc.reshape(b * c, d)], axis=0)
        eids = jnp.concatenate([eids, selc[..., 0:2].reshape(b * c, 2)], axis=0)
    counts = cnt[0, N_GROUPS:N_GROUPS + N_EXPERTS].astype(jnp.int32)
    src, blk_start, blk_expert, n_used = _dispatch_tables(eids.astype(jnp.int32), counts)
    yb = _experts(h[src], blk_expert, n_used, w_up, w_down)
    pos = _row_positions(sel, blk_start)
    x = _moe_res(x, yb[pos[..., 0]], yb[pos[..., 1]], sel, mod, name=f"moe_res{li}")
    if xc is not None:
        posc = _row_positions(selc, blk_start)
        xc = _moe_res(xc, yb[posc[..., 0]], yb[posc[..., 1]], selc, modc, name=f"moe_res_ctx{li}")
    return x, xc


def _final_body(x_ref, g_ref, o_ref):
    x = x_ref[0]
    ms = jnp.mean(x * x, axis=-1, keepdims=True)
    o_ref[0] = x * lax.rsqrt(ms + NORM_EPS) * g_ref[...]


def _final_norm(x, g):
    b, l, d = x.shape
    tm = _row_tile(l, 512)
    row = pl.BlockSpec((1, tm, d), lambda bi, i: (bi, i, 0))
    return pl.pallas_call(
        _final_body,
        grid=(b, l // tm),
        in_specs=[row, pl.BlockSpec((1, d), lambda bi, i: (0, 0))],
        out_specs=row,
        out_shape=jax.ShapeDtypeStruct((b, l, d), F32),
        compiler_params=_cp(("parallel", "parallel")),
        name="final_norm",
    )(x, g.reshape(1, d))


def _rope_tables(seq):
    rows = seq // GRID_W
    row = jnp.repeat(jnp.arange(rows), GRID_W).astype(F32)
    col = jnp.tile(jnp.arange(GRID_W), rows).astype(F32)
    nf = HEAD_DIM // 4
    inv = ROPE_BASE ** (-jnp.arange(nf, dtype=F32) / nf)
    ar, ac = row[:, None] * inv, col[:, None] * inv
    ang = jnp.concatenate([ar, ar, ac, ac] * (LANES // HEAD_DIM), axis=-1)
    return jnp.cos(ang), jnp.sin(ang)


def kernel(x, c, ctx, c_ctx, mod_w, mod_b, norm_g, final_g, win_w_qkv, win_sink, win_w_o, diff_w_qkv, diff_lambda, diff_subln_g, diff_w_o, pool_w_group, pool_b_group, pool_scale, rwkv_mu, rwkv_w_rkv, rwkv_w0, rwkv_w_a1, rwkv_w_a2, rwkv_a0, rwkv_a_a1, rwkv_a_a2, rwkv_g1, rwkv_g2, rwkv_k_k, rwkv_k_a, rwkv_r_k, rwkv_ln_w, rwkv_ln_b, rwkv_w_o, moe_wg, moe_bg, moe_we, moe_be, moe_w_up, moe_w_down):
    b, s, d = x.shape
    depth = mod_w.shape[0]
    n_mixers = 4
    cos, sin = _rope_tables(s)
    rows = -(-(b + 1) // 8) * 8
    c_all = jnp.zeros((rows, d), F32).at[:b].set(c).at[b].set(c_ctx)
    mods = _mod_all(c_all, mod_w, mod_b).reshape(depth, rows, 6, d)
    xc = ctx
    for i in range(depth):
        m, occ = i % n_mixers, i // n_mixers
        last = i == depth - 1
        mod = mods[i, :b]
        modc = mods[i, b:b + 1]
        g1 = norm_g[i, 0]
        if m == 0:
            nq = win_sink.shape[1] * HEAD_DIM
            nk = WIN_KV_HEADS * HEAD_DIM
            w = _bf(win_w_qkv[occ])
            qkv = _proj(x, g1, mod, w, cos, sin, n_rope=nq + nk, n_q=nq, scale_i=1, shift_i=0, name="win_qkv")
            qkvc = _proj(xc, g1, modc, w, None, None, n_rope=0, n_q=nq, scale_i=1, shift_i=0, name="win_qkv_ctx")
            o = _win_attn(qkv, qkvc, win_sink[occ], local=True, name="win_attn")
            wo = _bf(win_w_o[occ])
            x = _res_proj(o, wo, x, mod, gate_i=2, name="win_out")
            if not last:
                oc = _win_attn(None, qkvc, win_sink[occ], local=False, name="win_attn_ctx")
                xc = _res_proj(oc, wo, xc, modc, gate_i=2, name="win_out_ctx")
        elif m == 1:
            lambda_init = 0.8 - 0.6 * math.exp(-0.3 * i)
            w = _bf(diff_w_qkv[occ])
            qkv = _proj(x, g1, mod, w, cos, sin, n_rope=2 * d, n_q=d, scale_i=1, shift_i=0, name="diff_qkv")
            qkvc = _proj(xc, g1, modc, w, None, None, n_rope=0, n_q=d, scale_i=1, shift_i=0, name="diff_qkv_ctx")
            o = _diff_attn(qkv, qkvc, diff_lambda[occ], diff_subln_g[occ], local=True,
                           lambda_init=lambda_init, name="diff_attn")
            wo = _bf(diff_w_o[occ])
            x = _res_proj(o, wo, x, mod, gate_i=2, name="diff_out")
            if not last:
                oc = _diff_attn(None, qkvc, diff_lambda[occ], diff_subln_g[occ], local=False,
                                lambda_init=lambda_init, name="diff_attn_ctx")
                xc = _res_proj(oc, wo, xc, modc, gate_i=2, name="diff_out_ctx")
        elif m == 2:
            x = _pool_mix(x, g1, mod, pool_w_group[occ], pool_b_group[occ], pool_scale[occ], name="pool")
            if not last:
                xc = _pool_mix(xc, g1, modc, pool_w_group[occ], pool_b_group[occ], pool_scale[occ], name="pool_ctx")
        else:
            p = dict(mu=rwkv_mu[occ], w_rkv=rwkv_w_rkv[occ], w0=rwkv_w0[occ], w_a1=rwkv_w_a1[occ],
                     w_a2=rwkv_w_a2[occ], a0=rwkv_a0[occ], a_a1=rwkv_a_a1[occ], a_a2=rwkv_a_a2[occ],
                     g1=rwkv_g1[occ], g2=rwkv_g2[occ], k_k=rwkv_k_k[occ], k_a=rwkv_k_a[occ])
            assert last, "the context stream's RWKV output path is only needed for non-final layers"
            r, v, kk, lw, bb, kd, g = _rwkv_features(x, g1, mod, p, with_out=True, name="rwkv_feat")
            vc, kkc, lwc, bc, kdc = _rwkv_features(xc, g1, modc, p, with_out=False, name="rwkv_feat_ctx")
            nh = d // HEAD_DIM
            s0 = jnp.zeros((2, b, nh, HEAD_DIM, HEAD_DIM), F32)
            (s_ctx,) = _rwkv_scan(kkc, vc, None, lwc, bc, kdc, s0, name="rwkv_scan_ctx")
            o, _ = _rwkv_scan(kk, v, r, lw, bb, kd, s_ctx, name="rwkv_scan")
            x = _rwkv_output(o, r, v, kd, g, rwkv_r_k[occ], rwkv_ln_w[occ], rwkv_ln_b[occ], rwkv_w_o[occ],
                             x, mod, name="rwkv_out")
        x, xc = _hier_moe(x, None if last else xc, mod, modc, norm_g[i, 1], moe_wg[i], moe_bg[i], moe_we[i],
                          moe_be[i], moe_w_up[i], moe_w_down[i], i)
    return _final_norm(x, final_g)
```

```python
import functools
import math

import jax
import jax.numpy as jnp
from jax import lax
from jax.experimental import pallas as pl
from jax.experimental.pallas import tpu as pltpu

F32 = jnp.float32
BF16 = jnp.bfloat16

HEAD_DIM = 64
GRID_W = 64
ROPE_BASE = 10000.0
NORM_EPS = 1e-6
NEG_INF = -1e30
WIN_KV_HEADS = 4
WIN_BLOCK = 128
POOL_WINDOWS = (2, 4, 8, 16)
POOL_HALO = 8
GN_EPS = 64e-5
N_GROUPS = 4
EXPERTS_PER_GROUP = 8
N_EXPERTS = N_GROUPS * EXPERTS_PER_GROUP
MOE_BLOCK = 512
DIFF_Q_ROWS = 1024
DIFF_Q_SUB = 256
RWKV_CHUNK = 64
LANES = 128
V7X_VMEM_LIMIT = 48 * 1024 * 1024
HI = lax.Precision.HIGHEST


def _cp(sem, vmem=V7X_VMEM_LIMIT):
    return pltpu.CompilerParams(dimension_semantics=sem, vmem_limit_bytes=vmem)


def _bf(x):
    return x.astype(BF16)


def _dot(a, b, precision=None):
    return jnp.dot(a, b, preferred_element_type=F32, precision=precision)


def _dot_t(a, b, precision=None):
    return lax.dot_general(a, b, (((1,), (1,)), ((), ())), preferred_element_type=F32, precision=precision)


def _dot_l(a, b, precision=None):
    return lax.dot_general(a, b, (((0,), (0,)), ((), ())), preferred_element_type=F32, precision=precision)


def _mm(a, b):
    return _dot(_bf(a), _bf(b))


def _mm_t(a, b):
    return _dot_t(_bf(a), _bf(b))


def _mm_l(a, b):
    return _dot_l(_bf(a), _bf(b))


def _split3(x):
    hi = _bf(x)
    r1 = x - hi.astype(F32)
    mid = _bf(r1)
    return hi, mid, _bf(r1 - mid.astype(F32))


def _cumulate(tri, x):
    hi, mid, lo = _split3(x)
    return _dot(tri, hi) + _dot(tri, mid) + _dot(tri, lo)


def _head_sums(x):
    i = lax.broadcasted_iota(jnp.int32, (LANES, LANES), 0) // HEAD_DIM
    j = lax.broadcasted_iota(jnp.int32, (LANES, LANES), 1) // HEAD_DIM
    ones = _bf(jnp.where(i == j, 1.0, 0.0))
    out = []
    for t in range(x.shape[1] // LANES):
        hi, mid, lo = _split3(x[:, t * LANES:(t + 1) * LANES])
        out.append(_dot(hi, ones) + _dot(mid, ones) + _dot(lo, ones))
    return jnp.concatenate(out, axis=1)


def _norm_mod(x, g, scale, shift):
    ms = jnp.mean(x * x, axis=-1, keepdims=True)
    y = x * lax.rsqrt(ms + NORM_EPS) * g
    return y * (1.0 + scale) + shift


def _sigmoid(x):
    return 1.0 / (1.0 + jnp.exp(-x))


def _row_tile(n, pref):
    t = min(pref, n)
    assert n % t == 0
    return t


def _mod_body(c_ref, w_ref, b_ref, o_ref):
    c = c_ref[...]
    s = c * _sigmoid(c)
    o_ref[0] = _dot(_bf(s), _bf(w_ref[0])) + b_ref[0]


def _mod_all(c_all, mod_w, mod_b):
    depth, d, n = mod_w.shape
    r = c_all.shape[0]
    tn = 1536
    return pl.pallas_call(
        _mod_body,
        grid=(depth, n // tn),
        in_specs=[
            pl.BlockSpec((r, d), lambda i, j: (0, 0)),
            pl.BlockSpec((1, d, tn), lambda i, j: (i, 0, j)),
            pl.BlockSpec((1, 1, tn), lambda i, j: (i, 0, j)),
        ],
        out_specs=pl.BlockSpec((1, r, tn), lambda i, j: (i, 0, j)),
        out_shape=jax.ShapeDtypeStruct((depth, r, n), F32),
        compiler_params=_cp(("parallel", "parallel")),
        name="adaln_mod",
    )(c_all, mod_w, mod_b.reshape(depth, 1, n))


def _mod_spec(mod):
    if mod.shape[0] == 1:
        return pl.BlockSpec((1,) + mod.shape[1:], lambda b, i: (0, 0, 0))
    return pl.BlockSpec((1,) + mod.shape[1:], lambda b, i: (b, 0, 0))


def _rope_tile(y, cos, sin, first_half):
    fwd = pltpu.roll(y, LANES - 16, 1)
    bwd = pltpu.roll(y, 16, 1)
    rot = jnp.where(first_half, -fwd, bwd)
    return y * cos + rot * sin


def _proj_body(*refs, n_rope, n_q, scale_i, shift_i):
    if n_rope:
        x_ref, g_ref, m_ref, w_ref, cos_ref, sin_ref, o_ref = refs
    else:
        x_ref, g_ref, m_ref, w_ref, o_ref = refs
    m = m_ref[0]
    h = _norm_mod(x_ref[0], g_ref[...], m[scale_i:scale_i + 1], m[shift_i:shift_i + 1])
    y = _dot(_bf(h), w_ref[...])
    n = y.shape[1]
    if n_rope:
        cos, sin = cos_ref[...], sin_ref[...]
        first_half = (lax.broadcasted_iota(jnp.int32, cos.shape, 1) & 31) < 16
    for j in range(n // LANES):
        blk = y[:, j * LANES:(j + 1) * LANES]
        if j * LANES < n_q:
            blk = blk * (HEAD_DIM ** -0.5)
        if j * LANES < n_rope:
            blk = _rope_tile(blk, cos, sin, first_half)
        o_ref[0, :, j * LANES:(j + 1) * LANES] = _bf(blk)


def _proj(x, g, mod, w, cos, sin, *, n_rope, n_q, scale_i, shift_i, name):
    b, l, d = x.shape
    n = w.shape[1]
    tm = _row_tile(l, 256)
    in_specs = [
        pl.BlockSpec((1, tm, d), lambda bi, i: (bi, i, 0)),
        pl.BlockSpec((1, d), lambda bi, i: (0, 0)),
        _mod_spec(mod),
        pl.BlockSpec((d, n), lambda bi, i: (0, 0)),
    ]
    args = [x, g.reshape(1, d), mod, w]
    if n_rope:
        in_specs += [pl.BlockSpec((tm, LANES), lambda bi, i: (i, 0))] * 2
        args += [cos, sin]
    return pl.pallas_call(
        functools.partial(_proj_body, n_rope=n_rope, n_q=n_q, scale_i=scale_i, shift_i=shift_i),
        grid=(b, l // tm),
        in_specs=in_specs,
        out_specs=pl.BlockSpec((1, tm, n), lambda bi, i: (bi, i, 0)),
        out_shape=jax.ShapeDtypeStruct((b, l, n), BF16),
        compiler_params=_cp(("parallel", "parallel")),
        name=name,
    )(*args)


def _res_body(a_ref, w_ref, x_ref, m_ref, o_ref, *, gate_i):
    y = _dot(a_ref[0], w_ref[...])
    o_ref[0] = x_ref[0] + m_ref[0][gate_i:gate_i + 1] * y


def _res_proj(a, w, x, mod, *, gate_i, name):
    b, l, d = x.shape
    k = a.shape[-1]
    tm = _row_tile(l, 512)
    in_specs = [
        pl.BlockSpec((1, tm, k), lambda bi, i: (bi, i, 0)),
        pl.BlockSpec((k, d), lambda bi, i: (0, 0)),
        pl.BlockSpec((1, tm, d), lambda bi, i: (bi, i, 0)),
        _mod_spec(mod),
    ]
    args = [a, w, x, mod]
    return pl.pallas_call(
        functools.partial(_res_body, gate_i=gate_i),
        grid=(b, l // tm),
        in_specs=in_specs,
        out_specs=pl.BlockSpec((1, tm, d), lambda bi, i: (bi, i, 0)),
        out_shape=jax.ShapeDtypeStruct((b, l, d), F32),
        input_output_aliases={2: 0},
        compiler_params=_cp(("parallel", "parallel")),
        name=name,
    )(*args)


def _win_body(*refs, seq, local, n_heads):
    if local:
        q_ref, kp_ref, kc_ref, kn_ref, vp_ref, vc_ref, vn_ref, kx_ref, vx_ref, sink_ref, o_ref = refs
    else:
        q_ref, kx_ref, vx_ref, sink_ref, o_ref = refs
    tq = q_ref.shape[1]
    grp = n_heads // WIN_KV_HEADS
    n = pl.program_id(1)
    if local:
        span = 3 * tq
        row = lax.broadcasted_iota(jnp.int32, (grp * tq, span), 0) & (tq - 1)
        col = lax.broadcasted_iota(jnp.int32, (grp * tq, span), 1)
        rel = col - tq - row
        key_pos = n * tq - tq + col
        mask = (jnp.abs(rel) <= tq) & (key_pos >= 0) & (key_pos < seq)
    hks = range(WIN_KV_HEADS)
    ks = [slice(hk * HEAD_DIM, (hk + 1) * HEAD_DIM) for hk in hks]
    qh = [jnp.concatenate(
        [q_ref[0, :, (hk * grp + g) * HEAD_DIM:(hk * grp + g + 1) * HEAD_DIM] for g in range(grp)], axis=0)
        for hk in hks]
    sink = [jnp.concatenate([jnp.full((tq, 1), sink_ref[hk * grp + g], F32) for g in range(grp)], axis=0)
            for hk in hks]
    s_ctx = [_dot_t(qh[hk], kx_ref[0, :, ks[hk]]) for hk in hks]
    m = [jnp.maximum(jnp.max(s_ctx[hk], axis=1, keepdims=True), sink[hk]) for hk in hks]
    if local:
        s_loc = [jnp.where(mask, _dot_t(qh[hk], jnp.concatenate(
            [kp_ref[0, :, ks[hk]], kc_ref[0, :, ks[hk]], kn_ref[0, :, ks[hk]]], axis=0)), NEG_INF) for hk in hks]
        m = [jnp.maximum(m[hk], jnp.max(s_loc[hk], axis=1, keepdims=True)) for hk in hks]
    p_ctx = [jnp.exp(s_ctx[hk] - m[hk]) for hk in hks]
    den = [jnp.sum(p_ctx[hk], axis=1, keepdims=True) + jnp.exp(sink[hk] - m[hk]) for hk in hks]
    o = [_dot(_bf(p_ctx[hk]), vx_ref[0, :, ks[hk]]) for hk in hks]
    if local:
        p_loc = [jnp.exp(s_loc[hk] - m[hk]) for hk in hks]
        den = [den[hk] + jnp.sum(p_loc[hk], axis=1, keepdims=True) for hk in hks]
        o = [o[hk] + _dot(_bf(p_loc[hk]), jnp.concatenate(
            [vp_ref[0, :, ks[hk]], vc_ref[0, :, ks[hk]], vn_ref[0, :, ks[hk]]], axis=0)) for hk in hks]
    for hk in hks:
        oh = o[hk] / den[hk]
        for g in range(grp):
            hq = hk * grp + g
            o_ref[0, :, hq * HEAD_DIM:(hq + 1) * HEAD_DIM] = _bf(oh[g * tq:(g + 1) * tq])


def _win_attn(qkv, qkvc, sink, *, local, name):
    src = qkv if local else qkvc
    b, l, _ = src.shape
    c = qkvc.shape[1]
    n_heads = sink.shape[0]
    d = n_heads * HEAD_DIM
    kvw = WIN_KV_HEADS * HEAD_DIM
    kcol = d // kvw
    tq = WIN_BLOCK
    nb = l // tq
    in_specs = [pl.BlockSpec((1, tq, d), lambda bi, i: (bi, i, 0))]
    args = [src]
    if local:
        for colb in (kcol, kcol + 1):
            in_specs += [
                pl.BlockSpec((1, tq, kvw), lambda bi, i, colb=colb: (bi, jnp.maximum(i - 1, 0), colb)),
                pl.BlockSpec((1, tq, kvw), lambda bi, i, colb=colb: (bi, i, colb)),
                pl.BlockSpec((1, tq, kvw), lambda bi, i, colb=colb: (bi, jnp.minimum(i + 1, nb - 1), colb)),
            ]
            args += [qkv, qkv, qkv]
    in_specs += [
        pl.BlockSpec((1, c, kvw), lambda bi, i: (bi, 0, kcol)),
        pl.BlockSpec((1, c, kvw), lambda bi, i: (bi, 0, kcol + 1)),
        pl.BlockSpec(memory_space=pltpu.SMEM),
    ]
    args += [qkvc, qkvc, sink]
    return pl.pallas_call(
        functools.partial(_win_body, seq=l, local=local, n_heads=n_heads),
        grid=(b, nb),
        in_specs=in_specs,
        out_specs=pl.BlockSpec((1, tq, d), lambda bi, i: (bi, i, 0)),
        out_shape=jax.ShapeDtypeStruct((b, l, d), BF16),
        compiler_params=_cp(("parallel", "parallel")),
        name=name,
    )(*args)


def _diff_body(*refs, n_lat, tk, sub, lambda_init):
    if n_lat:
        lam_ref, g_ref, q_ref, kl_ref, vl_ref, kx_ref, vx_ref, o_ref, s_ref = refs
    else:
        lam_ref, g_ref, q_ref, kx_ref, vx_ref, o_ref, s_ref = refs
    hw = 2 * HEAD_DIM
    n_sub = q_ref.shape[1] // sub
    lane = lax.broadcasted_iota(jnp.int32, (sub, hw), 1)
    chunks = [(kl_ref, vl_ref, i * tk, tk, i * tk) for i in range(n_lat)] if n_lat else []
    chunks.append((kx_ref, vx_ref, 0, kx_ref.shape[1], n_lat * tk))
    lam = lam_ref[...]
    lam_full = (jnp.exp(jnp.sum(lam[0:1] * lam[1:2], axis=1, keepdims=True))
                - jnp.exp(jnp.sum(lam[2:3] * lam[3:4], axis=1, keepdims=True)) + lambda_init)

    def stacked_q(j):
        q = q_ref[0, j * sub:(j + 1) * sub, :]
        zero = jnp.zeros_like(q)
        return jnp.concatenate([jnp.where(lane < HEAD_DIM, q, zero), jnp.where(lane >= HEAD_DIM, q, zero)], axis=0)

    def score_chunk(j, qq, ch, mx):
        k_ref, _, row, n, col = ch
        s = _dot_t(qq, k_ref[0, row:row + n, :])
        s_ref[j % 2, :, col:col + n] = s
        for t in range(n // LANES):
            mx = jnp.maximum(mx, s[:, t * LANES:(t + 1) * LANES])
        return mx

    def value_chunk(j, m, ch, acc):
        _, v_ref, row, n, col = ch
        p = jnp.concatenate(
            [jnp.exp(s_ref[j % 2, :, col + t * LANES:col + (t + 1) * LANES] - m) for t in range(n // LANES)], axis=1)
        v = v_ref[0, row:row + n, :]
        return acc + _dot(_bf(p), jnp.concatenate([v, jnp.ones_like(v)], axis=1))

    def finish(j, acc):
        a = acc[:, :hw] / acc[:, hw:]
        o = a[:sub] - lam_full * a[sub:]
        ms = jnp.mean(o * o, axis=-1, keepdims=True)
        o = o * lax.rsqrt(ms + NORM_EPS) * g_ref[...] * (1.0 - lambda_init)
        o_ref[0, j * sub:(j + 1) * sub, :] = _bf(o)

    m_prev = None
    for j in range(n_sub + 1):
        if j < n_sub:
            qq = stacked_q(j)
            mx = jnp.full((2 * sub, LANES), NEG_INF, F32)
        if j > 0:
            acc = jnp.zeros((2 * sub, 2 * hw), F32)
        for ch in chunks:
            if j > 0:
                acc = value_chunk(j - 1, m_prev, ch, acc)
            if j < n_sub:
                mx = score_chunk(j, qq, ch, mx)
        if j > 0:
            finish(j - 1, acc)
        if j < n_sub:
            m_prev = jnp.broadcast_to(jnp.max(mx, axis=1, keepdims=True), (2 * sub, LANES))


def _diff_attn(qkv, qkvc, lam, subln_g, *, local, lambda_init, name):
    src = qkv if local else qkvc
    b, l, n3 = src.shape
    d = n3 // 3
    c = qkvc.shape[1]
    hw = 2 * HEAD_DIM
    nh = d // hw
    tq = _row_tile(l, DIFF_Q_ROWS)
    sub = min(DIFF_Q_SUB, tq)
    tk = 512
    in_specs = [
        pl.BlockSpec((4, HEAD_DIM), lambda bi, h, i: (0, 0)),
        pl.BlockSpec((1, hw), lambda bi, h, i: (0, 0)),
        pl.BlockSpec((1, tq, hw), lambda bi, h, i: (bi, i, h)),
    ]
    args = [lam, subln_g.reshape(1, hw), src]
    n_lat = 0
    if local:
        s = qkv.shape[1]
        n_lat = s // tk
        in_specs += [
            pl.BlockSpec((1, s, hw), lambda bi, h, i: (bi, 0, nh + h)),
            pl.BlockSpec((1, s, hw), lambda bi, h, i: (bi, 0, 2 * nh + h)),
        ]
        args += [qkv, qkv]
    in_specs += [
        pl.BlockSpec((1, c, hw), lambda bi, h, i: (bi, 0, nh + h)),
        pl.BlockSpec((1, c, hw), lambda bi, h, i: (bi, 0, 2 * nh + h)),
    ]
    args += [qkvc, qkvc]
    return pl.pallas_call(
        functools.partial(_diff_body, n_lat=n_lat, tk=tk, sub=sub, lambda_init=lambda_init),
        grid=(b, nh, l // tq),
        in_specs=in_specs,
        out_specs=pl.BlockSpec((1, tq, hw), lambda bi, h, i: (bi, i, h)),
        out_shape=jax.ShapeDtypeStruct((b, l, d), BF16),
        scratch_shapes=[pltpu.VMEM((2, 2 * sub, n_lat * tk + c), F32)],
        compiler_params=_cp(("parallel", "parallel", "parallel")),
        name=name,
    )(*args)


def _halo_specs(l, tm, d):
    nh = l // POOL_HALO
    per = tm // POOL_HALO
    return [
        pl.BlockSpec((1, POOL_HALO, d), lambda bi, i: (bi, jnp.maximum(i * per - 1, 0), 0)),
        pl.BlockSpec((1, tm, d), lambda bi, i: (bi, i, 0)),
        pl.BlockSpec((1, POOL_HALO, d), lambda bi, i: (bi, jnp.minimum((i + 1) * per, nh - 1), 0)),
    ]


def _fill_normed(h_ref, xp_ref, x_ref, xn_ref, g, scale, shift):
    tm = x_ref.shape[1]
    h_ref[0:POOL_HALO, :] = _norm_mod(xp_ref[0], g, scale, shift)
    h_ref[POOL_HALO:POOL_HALO + tm, :] = _norm_mod(x_ref[0], g, scale, shift)
    h_ref[POOL_HALO + tm:2 * POOL_HALO + tm, :] = _norm_mod(xn_ref[0], g, scale, shift)


def _pool_body(xp_ref, x_ref, xn_ref, g_ref, m_ref, w_ref, b_ref, s_ref, o_ref, h_ref, *, seq):
    tm = x_ref.shape[1]
    d = x_ref.shape[2]
    gd = d // len(POOL_WINDOWS)
    m = m_ref[0]
    _fill_normed(h_ref, xp_ref, x_ref, xn_ref, g_ref[...], m[1:2], m[0:1])
    pos = pl.program_id(1) * tm + lax.broadcasted_iota(jnp.int32, (tm, 1), 0)
    x = x_ref[0]
    for gi, w in enumerate(POOL_WINDOWS):
        cols = slice(gi * gd, (gi + 1) * gd)
        acc = jnp.zeros((tm, gd), F32)
        for off in range(-(w // 2), w - w // 2):
            valid = (pos + off >= 0) & (pos + off < seq)
            acc = acc + jnp.where(valid, h_ref[POOL_HALO + off:POOL_HALO + off + tm, cols], 0.0)
        lo = jnp.maximum(pos - w // 2, 0)
        hi = jnp.minimum(pos + w - w // 2, seq)
        y = acc / (hi - lo).astype(F32) - h_ref[POOL_HALO:POOL_HALO + tm, cols]
        y = (_dot(_bf(y), w_ref[gi]) + b_ref[:, cols]) * s_ref[:, cols]
        o_ref[0, :, cols] = x[:, cols] + m[2:3, cols] * y


def _pool_mix(x, g, mod, w_group, b_group, layer_scale, *, name):
    b, l, d = x.shape
    tm = _row_tile(l, 256)
    gd = d // len(POOL_WINDOWS)
    return pl.pallas_call(
        functools.partial(_pool_body, seq=l),
        grid=(b, l // tm),
        in_specs=_halo_specs(l, tm, d) + [
            pl.BlockSpec((1, d), lambda bi, i: (0, 0)),
            _mod_spec(mod),
            pl.BlockSpec((len(POOL_WINDOWS), gd, gd), lambda bi, i: (0, 0, 0)),
            pl.BlockSpec((1, d), lambda bi, i: (0, 0)),
            pl.BlockSpec((1, d), lambda bi, i: (0, 0)),
        ],
        out_specs=pl.BlockSpec((1, tm, d), lambda bi, i: (bi, i, 0)),
        out_shape=jax.ShapeDtypeStruct((b, l, d), F32),
        scratch_shapes=[pltpu.VMEM((tm + 2 * POOL_HALO, d), F32)],
        compiler_params=_cp(("parallel", "parallel")),
        name=name,
    )(x, x, x, g.reshape(1, d), mod, _bf(w_group), b_group.reshape(1, d), layer_scale.reshape(1, d))


def _softplus(z):
    return jnp.maximum(z, 0.0) + jnp.log1p(jnp.exp(-jnp.abs(z)))


def _rwkv_feat_body(*refs, seq, with_out):
    (xp_ref, x_ref, xn_ref, g_ref, m_ref, mu_ref, wrkv_ref, w0_ref, wa1_ref, wa2_ref, a0_ref, aa1_ref,
     aa2_ref, g1_ref, g2_ref, kk_ref, ka_ref) = refs[:17]
    if with_out:
        r_out, v_out, kk_out, lw_out, b_out, kd_out, g_out, h_ref = refs[17:]
    else:
        v_out, kk_out, lw_out, b_out, kd_out, h_ref = refs[17:]
    tm = x_ref.shape[1]
    m = m_ref[0]
    _fill_normed(h_ref, xp_ref, x_ref, xn_ref, g_ref[...], m[1:2], m[0:1])
    pos = pl.program_id(1) * tm + lax.broadcasted_iota(jnp.int32, (tm, 1), 0)
    t = h_ref[POOL_HALO:POOL_HALO + tm, :]
    dp = jnp.where(pos >= 1, h_ref[POOL_HALO - 1:POOL_HALO - 1 + tm, :], 0.0) - t
    dn = jnp.where(pos < seq - 1, h_ref[POOL_HALO + 1:POOL_HALO + 1 + tm, :], 0.0) - t

    def mix(i):
        return _bf(t + dp * mu_ref[0, i:i + 1, :] + dn * mu_ref[1, i:i + 1, :])

    k = _dot(mix(2), wrkv_ref[1])
    v_out[0] = _dot(mix(3), wrkv_ref[2])
    kk = k * kk_ref[...]
    kk = kk / jnp.maximum(jnp.sqrt(_head_sums(kk * kk)), 1e-12)
    kk_out[0] = kk
    xw, xa = mix(1), mix(4)
    for di in range(2):
        z = w0_ref[di:di + 1, :] + _dot(_bf(jnp.tanh(_dot(xw, wa1_ref[di]))), wa2_ref[di])
        w = -_softplus(-z) - 0.5
        lw_out[di, 0] = -jnp.exp(w)
        a = _sigmoid(a0_ref[di:di + 1, :] + _dot(_bf(_dot(xa, aa1_ref[di])), aa2_ref[di]))
        kd_out[di, 0] = k * (1.0 + (a - 1.0) * ka_ref[...])
        b_out[di, 0] = kk * a
    if with_out:
        r_out[0] = _dot(mix(0), wrkv_ref[0])
        g_out[0] = _dot(_bf(_sigmoid(_dot(mix(5), g1_ref[...]))), g2_ref[...])


def _rwkv_features(x, g, mod, p, *, with_out, name):
    b, l, d = x.shape
    tm = _row_tile(l, 128)
    const = lambda a: pl.BlockSpec(a.shape, lambda bi, i, nd=a.ndim: (0,) * nd)
    weights = [p["mu"], _bf(p["w_rkv"]), p["w0"], _bf(p["w_a1"]), _bf(p["w_a2"]), p["a0"], _bf(p["a_a1"]),
               _bf(p["a_a2"]), _bf(p["g1"]), _bf(p["g2"]), p["k_k"].reshape(1, d), p["k_a"].reshape(1, d)]
    one = pl.BlockSpec((1, tm, d), lambda bi, i: (bi, i, 0))
    two = pl.BlockSpec((2, 1, tm, d), lambda bi, i: (0, bi, i, 0))
    s1 = jax.ShapeDtypeStruct((b, l, d), F32)
    s2 = jax.ShapeDtypeStruct((2, b, l, d), F32)
    out_specs = [one, one, two, two, two]
    out_shape = [s1, s1, s2, s2, s2]
    if with_out:
        out_specs = [one] + out_specs + [one]
        out_shape = [s1] + out_shape + [s1]
    return pl.pallas_call(
        functools.partial(_rwkv_feat_body, seq=l, with_out=with_out),
        grid=(b, l // tm),
        in_specs=_halo_specs(l, tm, d) + [pl.BlockSpec((1, d), lambda bi, i: (0, 0)), _mod_spec(mod)]
        + [const(a) for a in weights],
        out_specs=out_specs,
        out_shape=out_shape,
        scratch_shapes=[pltpu.VMEM((tm + 2 * POOL_HALO, d), F32)],
        compiler_params=_cp(("parallel", "parallel")),
        name=name,
    )(x, x, x, g.reshape(1, d), mod, *weights)


def _scan_body(*refs, emit, n_chunks):
    if emit:
        kk_ref, v_ref, r_ref, lw_ref, b_ref, kd_ref, s0_ref, o_ref, sfin_ref, s_ref = refs
    else:
        kk_ref, v_ref, lw_ref, b_ref, kd_ref, s0_ref, sfin_ref, s_ref = refs
    hb = s_ref.shape[0]
    L = kk_ref.shape[1]
    rev = pl.program_id(0) == 1
    c = pl.program_id(2)

    @pl.when(c == 0)
    def _():
        s_ref[...] = s0_ref[0, 0]

    row = lax.broadcasted_iota(jnp.int32, (L, L), 0)
    col = lax.broadcasted_iota(jnp.int32, (L, L), 1)
    flip = rev.astype(jnp.int32)
    p_row = row + flip * (L - 1 - 2 * row)
    p_col = col + flip * (L - 1 - 2 * col)
    incl = p_col <= p_row
    strict = p_col < p_row
    tri = _bf(incl.astype(F32))
    eye = row == col
    assert L == HEAD_DIM
    levels = []
    m = 1
    while m < L:
        same = (p_row // (2 * m)) == (p_col // (2 * m))
        levels.append(same & ((p_row & (2 * m - 1)) >= m) & ((p_col & (2 * m - 1)) < m))
        m *= 2

    hs = range(hb)
    cut = lambda a: [a[:, h * HEAD_DIM:(h + 1) * HEAD_DIM] for h in hs]
    lw_all = lw_ref[0, 0]
    b_all = b_ref[0, 0]
    kd_all = kd_ref[0, 0]
    cl_all = _cumulate(tri, lw_all)
    tot_all = jnp.sum(lw_all, axis=0, keepdims=True)
    w_inv = jnp.exp(-cl_all)
    w_last = jnp.exp(tot_all - cl_all)
    kh = cut(kk_ref[0] * jnp.exp(cl_all - lw_all))
    bt, kt = cut(b_all * w_inv), cut(kd_all * w_inv)
    bt_l, kt_l = cut(b_all * w_last), cut(kd_all * w_last)
    w_tot = cut(jnp.exp(tot_all))
    v = cut(v_ref[0])
    s0 = [s_ref[h] for h in hs]
    rhs = [jnp.concatenate([bt[h], kt[h]], axis=0) for h in hs]
    if emit:
        rh = cut(r_ref[0] * jnp.exp(cl_all))
        lhs = [jnp.concatenate([kh[h], rh[h]], axis=0) for h in hs]
    else:
        lhs = kh
    big = [_mm_t(lhs[h], rhs[h]) for h in hs]
    m_b = [big[h][:L, :L] for h in hs]
    mkv = [_mm(jnp.where(strict, big[h][:L, L:], 0.0), v[h]) for h in hs]
    t = [jnp.where(eye, 1.0, 0.0) - jnp.where(levels[0], m_b[h], 0.0) for h in hs]
    for lm in levels[1:]:
        tl = [_mm(t[h], jnp.where(lm, m_b[h], 0.0)) for h in hs]
        t = [t[h] - _mm(tl[h], t[h]) for h in hs]
    y = [_mm(t[h], jnp.concatenate([kh[h], mkv[h]], axis=1)) for h in hs]
    gh = [_mm_l(y[h], bt_l[h]) for h in hs]
    vk = [_mm_l(v[h], kt_l[h]) for h in hs]
    g_mat = [jnp.where(eye, w_tot[h], 0.0) - gh[h][:HEAD_DIM] for h in hs]
    if emit:
        ab_y = [_mm(jnp.where(incl, big[h][L:, :L], 0.0), y[h]) for h in hs]
        akv = [_mm(jnp.where(incl, big[h][L:, L:], 0.0), v[h]) for h in hs]
        rs = [_mm_t(rh[h] - ab_y[h][:, :HEAD_DIM], s0[h]) for h in hs]
        o_ref[0, 0] = jnp.concatenate([rs[h] + akv[h] - ab_y[h][:, HEAD_DIM:] for h in hs], axis=1)
    sg = [_mm(s0[h], g_mat[h]) for h in hs]
    for h in hs:
        s_ref[h] = sg[h] + vk[h] - gh[h][HEAD_DIM:]

    @pl.when(c == n_chunks - 1)
    def _():
        sfin_ref[0, 0] = s_ref[...]


def _rwkv_scan(kk, v, r, lw, b, kd, s0, *, name):
    bsz, l, d = kk.shape
    dh = HEAD_DIM
    nh = d // dh
    L = RWKV_CHUNK
    nch = l // L
    emit = r is not None

    def chunk(di, c):
        return c + di * (nch - 1 - 2 * c)

    one = pl.BlockSpec((1, L, d), lambda di, bi, c: (bi, chunk(di, c), 0))
    two = pl.BlockSpec((1, 1, L, d), lambda di, bi, c: (di, bi, chunk(di, c), 0))
    st = pl.BlockSpec((1, 1, nh, dh, dh), lambda di, bi, c: (di, bi, 0, 0, 0))
    in_specs = [one, one] + ([one] if emit else []) + [two, two, two, st]
    args = [kk, v] + ([r] if emit else []) + [lw, b, kd, s0]
    s_shape = jax.ShapeDtypeStruct((2, bsz, nh, dh, dh), F32)
    if emit:
        out_specs = [two, st]
        out_shape = [jax.ShapeDtypeStruct((2, bsz, l, d), F32), s_shape]
    else:
        out_specs = [st]
        out_shape = [s_shape]
    return pl.pallas_call(
        functools.partial(_scan_body, emit=emit, n_chunks=nch),
        grid=(2, bsz, nch),
        in_specs=in_specs,
        out_specs=out_specs,
        out_shape=out_shape,
        scratch_shapes=[pltpu.VMEM((nh, dh, dh), F32)],
        compiler_params=_cp(("parallel", "parallel", "arbitrary")),
        name=name,
    )(*args)


def _rwkv_out_body(o_ref, r_ref, v_ref, kd_ref, g_ref, rk_ref, lnw_ref, lnb_ref, w_ref, x_ref, m_ref, out_ref):
    o = o_ref[0, 0] + o_ref[1, 0]
    mean = _head_sums(o) * (1.0 / HEAD_DIM)
    cen = o - mean
    var = _head_sums(cen * cen) * (1.0 / HEAD_DIM)
    on = cen * lax.rsqrt(var + GN_EPS) * lnw_ref[...] + lnb_ref[...]
    bonus = _head_sums(r_ref[0] * (kd_ref[0, 0] + kd_ref[1, 0]) * rk_ref[...]) * v_ref[0]
    a = (on + bonus) * g_ref[0]
    out_ref[0] = x_ref[0] + m_ref[0][2:3] * _dot(_bf(a), w_ref[...])


def _rwkv_output(o, r, v, kd, g, r_k, ln_w, ln_b, w_o, x, mod, *, name):
    b, l, d = x.shape
    tm = _row_tile(l, 256)
    row = pl.BlockSpec((1, tm, d), lambda bi, i: (bi, i, 0))
    row2 = pl.BlockSpec((2, 1, tm, d), lambda bi, i: (0, bi, i, 0))
    small = pl.BlockSpec((1, d), lambda bi, i: (0, 0))
    return pl.pallas_call(
        _rwkv_out_body,
        grid=(b, l // tm),
        in_specs=[row2, row, row, row2, row, small, small, small,
                  pl.BlockSpec((d, d), lambda bi, i: (0, 0)), row, _mod_spec(mod)],
        out_specs=row,
        out_shape=jax.ShapeDtypeStruct((b, l, d), F32),
        input_output_aliases={9: 0},
        compiler_params=_cp(("parallel", "parallel")),
        name=name,
    )(o, r, v, kd, g, r_k.reshape(1, d), ln_w.reshape(1, d), ln_b.reshape(1, d), _bf(w_o), x, mod)


def _router_body(x_ref, g_ref, m_ref, whi_ref, wlo_ref, b_ref, cin_ref, h_ref, sel_ref, cnt_ref, run_ref):
    @pl.when((pl.program_id(0) == 0) & (pl.program_id(1) == 0))
    def _():
        run_ref[...] = cin_ref[...]

    m = m_ref[0]
    h = _norm_mod(x_ref[0], g_ref[...], m[4:5], m[3:4])
    hi = _bf(h)
    h_ref[0] = h
    lo_part = _bf(h - hi.astype(F32))
    lg = _dot(hi, whi_ref[...]) + _dot(hi, wlo_ref[...]) + _dot(lo_part, whi_ref[...]) + b_ref[...]
    lane = lax.broadcasted_iota(jnp.int32, lg.shape, 1)
    big = jnp.int32(1 << 20)
    g_logit = jnp.where(lane < N_GROUPS, lg, -jnp.inf)
    g_max = jnp.max(g_logit, axis=1, keepdims=True)
    gsel = jnp.min(jnp.where(g_logit == g_max, lane, big), axis=1, keepdims=True)
    p_grp = 1.0 / jnp.sum(jnp.exp(g_logit - g_max), axis=1, keepdims=True)
    lo = N_GROUPS + gsel * EXPERTS_PER_GROUP
    e_logit = jnp.where((lane >= lo) & (lane < lo + EXPERTS_PER_GROUP), lg, -jnp.inf)
    e_max = jnp.max(e_logit, axis=1, keepdims=True)
    i1 = jnp.min(jnp.where(e_logit == e_max, lane, big), axis=1, keepdims=True)
    rest = jnp.where(lane == i1, -jnp.inf, e_logit)
    e2 = jnp.max(rest, axis=1, keepdims=True)
    i2 = jnp.min(jnp.where(rest == e2, lane, big), axis=1, keepdims=True)
    q2 = jnp.exp(e2 - e_max)
    w1 = p_grp / (1.0 + q2)
    w2 = p_grp * q2 / (1.0 + q2)
    pick1, pick2 = lane == i1, lane == i2
    both = jnp.where(pick1 | pick2, 1.0, 0.0)
    tm = lg.shape[0]
    earlier = lax.broadcasted_iota(jnp.int32, (tm, tm), 1) < lax.broadcasted_iota(jnp.int32, (tm, tm), 0)
    ahead = _dot(_bf(earlier.astype(F32)), _bf(both)) + run_ref[...]
    r1 = jnp.sum(jnp.where(pick1, ahead, 0.0), axis=1, keepdims=True)
    r2 = jnp.sum(jnp.where(pick2, ahead, 0.0), axis=1, keepdims=True)
    run_ref[...] = run_ref[...] + jnp.sum(both, axis=0, keepdims=True)
    cnt_ref[...] = run_ref[...]
    cols = ((i1 - N_GROUPS).astype(F32), (i2 - N_GROUPS).astype(F32), w1, w2, r1, r2)
    sel = jnp.zeros_like(lg)
    for j, val in enumerate(cols):
        sel = jnp.where(lane == j, val, sel)
    sel_ref[0] = sel


def _router(x, g, mod, w_hi, w_lo, b_r, cnt_in, *, name):
    b, l, d = x.shape
    tm = _row_tile(l, 256)
    wide = pl.BlockSpec((1, LANES), lambda bi, i: (0, 0))
    return pl.pallas_call(
        _router_body,
        grid=(b, l // tm),
        in_specs=[
            pl.BlockSpec((1, tm, d), lambda bi, i: (bi, i, 0)),
            pl.BlockSpec((1, d), lambda bi, i: (0, 0)),
            _mod_spec(mod),
            pl.BlockSpec((d, LANES), lambda bi, i: (0, 0)),
            pl.BlockSpec((d, LANES), lambda bi, i: (0, 0)),
            wide,
            wide,
        ],
        out_specs=[pl.BlockSpec((1, tm, d), lambda bi, i: (bi, i, 0)),
                   pl.BlockSpec((1, tm, LANES), lambda bi, i: (bi, i, 0)),
                   wide],
        out_shape=[jax.ShapeDtypeStruct((b, l, d), F32), jax.ShapeDtypeStruct((b, l, LANES), F32),
                   jax.ShapeDtypeStruct((1, LANES), F32)],
        scratch_shapes=[pltpu.VMEM((1, LANES), F32)],
        compiler_params=_cp(("arbitrary", "arbitrary")),
        name=name,
    )(x, g.reshape(1, d), mod, w_hi, w_lo, b_r, cnt_in)


def _expert_body(be_ref, nu_ref, x_ref, wu_ref, wd_ref, o_ref, wub_ref, wdb_ref):
    i = pl.program_id(0)
    ff = wd_ref.shape[1]

    @pl.when((i == 0) | (be_ref[i] != be_ref[jnp.maximum(i - 1, 0)]))
    def _():
        wub_ref[...] = _bf(wu_ref[0])
        wdb_ref[...] = _bf(wd_ref[0])

    @pl.when(i < nu_ref[0])
    def _():
        u = _dot(_bf(x_ref[...]), wub_ref[...])
        gate = u[:, :ff]
        act = gate * _sigmoid(gate) * u[:, ff:]
        o_ref[...] = _dot(_bf(act), wdb_ref[...])

    @pl.when(i >= nu_ref[0])
    def _():
        o_ref[...] = jnp.zeros_like(o_ref)


def _experts(xb, blk_expert, n_used, w_up, w_down):
    rows, d = xb.shape
    nb = rows // MOE_BLOCK
    ff2 = w_up.shape[2]
    ff = w_down.shape[1]
    return pl.pallas_call(
        _expert_body,
        grid_spec=pltpu.PrefetchScalarGridSpec(
            num_scalar_prefetch=2,
            grid=(nb,),
            in_specs=[
                pl.BlockSpec((MOE_BLOCK, d), lambda i, be, nu: (i, 0)),
                pl.BlockSpec((1, d, ff2), lambda i, be, nu: (be[i], 0, 0)),
                pl.BlockSpec((1, ff, d), lambda i, be, nu: (be[i], 0, 0)),
            ],
            out_specs=pl.BlockSpec((MOE_BLOCK, d), lambda i, be, nu: (i, 0)),
            scratch_shapes=[pltpu.VMEM((d, ff2), BF16), pltpu.VMEM((ff, d), BF16)],
        ),
        out_shape=jax.ShapeDtypeStruct((rows, d), F32),
        compiler_params=_cp(("arbitrary",)),
        name="moe_experts",
    )(blk_expert, n_used, xb, w_up, w_down)


def _moe_res_body(*refs, final):
    if final:
        x_ref, y0_ref, y1_ref, sel_ref, m_ref, fg_ref, o_ref = refs
    else:
        x_ref, y0_ref, y1_ref, sel_ref, m_ref, o_ref = refs
    sel = sel_ref[0]
    y = y0_ref[0] * sel[:, 2:3] + y1_ref[0] * sel[:, 3:4]
    x = x_ref[0] + m_ref[0][5:6] * y
    if final:
        ms = jnp.mean(x * x, axis=-1, keepdims=True)
        x = x * lax.rsqrt(ms + NORM_EPS) * fg_ref[...]
    o_ref[0] = x


def _moe_res(x, y0, y1, sel, mod, final_g=None, *, name):
    b, l, d = x.shape
    tm = _row_tile(l, 512)
    row = pl.BlockSpec((1, tm, d), lambda bi, i: (bi, i, 0))
    in_specs = [row, row, row, pl.BlockSpec((1, tm, LANES), lambda bi, i: (bi, i, 0)), _mod_spec(mod)]
    args = [x, y0, y1, sel, mod]
    if final_g is not None:
        in_specs.append(pl.BlockSpec((1, d), lambda bi, i: (0, 0)))
        args.append(final_g.reshape(1, d))
    return pl.pallas_call(
        functools.partial(_moe_res_body, final=final_g is not None),
        grid=(b, l // tm),
        in_specs=in_specs,
        out_specs=row,
        out_shape=jax.ShapeDtypeStruct((b, l, d), F32),
        input_output_aliases={0: 0},
        compiler_params=_cp(("parallel", "parallel")),
        name=name,
    )(*args)


def _dispatch_tables(eids, counts):
    flat_e = eids.reshape(-1)
    a = flat_e.shape[0]
    order = jnp.argsort(flat_e)
    start = jnp.cumsum(counts) - counts
    nblk = (counts + MOE_BLOCK - 1) // MOE_BLOCK
    blk_end = jnp.cumsum(nblk)
    blk_start = blk_end - nblk
    nb = -(-a // MOE_BLOCK) + N_EXPERTS
    blk = jnp.arange(nb, dtype=jnp.int32)
    blk_expert = jnp.minimum(jnp.sum((blk[:, None] >= blk_end[None, :]).astype(jnp.int32), axis=1), N_EXPERTS - 1)
    local = jnp.arange(MOE_BLOCK, dtype=jnp.int32)[None, :] + ((blk - blk_start[blk_expert]) * MOE_BLOCK)[:, None]
    sorted_idx = jnp.clip(start[blk_expert][:, None] + local, 0, a - 1)
    spread = (jnp.arange(nb * MOE_BLOCK, dtype=jnp.int32) % (a // 2)).reshape(nb, MOE_BLOCK)
    src = jnp.where(local < counts[blk_expert][:, None], order[sorted_idx] // 2, spread).reshape(nb * MOE_BLOCK)
    return src, blk_start, blk_expert, blk_end[-1:].astype(jnp.int32)


def _row_positions(sel, blk_start):
    eid = sel[..., 0:2].astype(jnp.int32)
    first = jnp.sum(jnp.where(eid[..., None] == jnp.arange(N_EXPERTS, dtype=jnp.int32), blk_start, 0), axis=-1)
    return first * MOE_BLOCK + sel[..., 4:6].astype(jnp.int32)


def _hier_moe(x, xc, mod, modc, g2, wg, bg, we, be, w_up, w_down, li, final_g=None):
    b, s, d = x.shape
    w_r = jnp.zeros((d, LANES), F32).at[:, :N_GROUPS].set(wg).at[:, N_GROUPS:N_GROUPS + N_EXPERTS].set(we)
    b_r = jnp.zeros((1, LANES), F32).at[0, :N_GROUPS].set(bg).at[0, N_GROUPS:N_GROUPS + N_EXPERTS].set(be)
    w_hi = _bf(w_r)
    w_lo = _bf(w_r - w_hi.astype(F32))
    h, sel, cnt = _router(x, g2, mod, w_hi, w_lo, b_r, jnp.zeros((1, LANES), F32), name=f"router{li}")
    h = h.reshape(b * s, d)
    eids = sel[..., 0:2].reshape(b * s, 2)
    if xc is not None:
        c = xc.shape[1]
        hc, selc, cnt = _router(xc, g2, modc, w_hi, w_lo, b_r, cnt, name=f"router_ctx{li}")
        h = jnp.concatenate([h, hc.reshape(b * c, d)], axis=0)
        eids = jnp.concatenate([eids, selc[..., 0:2].reshape(b * c, 2)], axis=0)
    counts = cnt[0, N_GROUPS:N_GROUPS + N_EXPERTS].astype(jnp.int32)
    src, blk_start, blk_expert, n_used = _dispatch_tables(eids.astype(jnp.int32), counts)
    yb = _experts(h[src], blk_expert, n_used, w_up, w_down)
    pos = _row_positions(sel, blk_start)
    x = _moe_res(x, yb[pos[..., 0]], yb[pos[..., 1]], sel, mod, final_g, name=f"moe_res{li}")
    if xc is not None:
        posc = _row_positions(selc, blk_start)
        xc = _moe_res(xc, yb[posc[..., 0]], yb[posc[..., 1]], selc, modc, name=f"moe_res_ctx{li}")
    return x, xc


def _rope_tables(seq):
    rows = seq // GRID_W
    row = jnp.repeat(jnp.arange(rows), GRID_W).astype(F32)
    col = jnp.tile(jnp.arange(GRID_W), rows).astype(F32)
    nf = HEAD_DIM // 4
    inv = ROPE_BASE ** (-jnp.arange(nf, dtype=F32) / nf)
    ar, ac = row[:, None] * inv, col[:, None] * inv
    ang = jnp.concatenate([ar, ar, ac, ac] * (LANES // HEAD_DIM), axis=-1)
    return jnp.cos(ang), jnp.sin(ang)


def kernel(x, c, ctx, c_ctx, mod_w, mod_b, norm_g, final_g, win_w_qkv, win_sink, win_w_o, diff_w_qkv, diff_lambda, diff_subln_g, diff_w_o, pool_w_group, pool_b_group, pool_scale, rwkv_mu, rwkv_w_rkv, rwkv_w0, rwkv_w_a1, rwkv_w_a2, rwkv_a0, rwkv_a_a1, rwkv_a_a2, rwkv_g1, rwkv_g2, rwkv_k_k, rwkv_k_a, rwkv_r_k, rwkv_ln_w, rwkv_ln_b, rwkv_w_o, moe_wg, moe_bg, moe_we, moe_be, moe_w_up, moe_w_down):
    b, s, d = x.shape
    depth = mod_w.shape[0]
    n_mixers = 4
    cos, sin = _rope_tables(s)
    rows = -(-(b + 1) // 8) * 8
    c_all = jnp.zeros((rows, d), F32).at[:b].set(c).at[b].set(c_ctx)
    mods = _mod_all(c_all, mod_w, mod_b).reshape(depth, rows, 6, d)
    xc = ctx
    for i in range(depth):
        m, occ = i % n_mixers, i // n_mixers
        last = i == depth - 1
        mod = mods[i, :b]
        modc = mods[i, b:b + 1]
        g1 = norm_g[i, 0]
        if m == 0:
            nq = win_sink.shape[1] * HEAD_DIM
            nk = WIN_KV_HEADS * HEAD_DIM
            w = _bf(win_w_qkv[occ])
            qkv = _proj(x, g1, mod, w, cos, sin, n_rope=nq + nk, n_q=nq, scale_i=1, shift_i=0, name="win_qkv")
            qkvc = _proj(xc, g1, modc, w, None, None, n_rope=0, n_q=nq, scale_i=1, shift_i=0, name="win_qkv_ctx")
            o = _win_attn(qkv, qkvc, win_sink[occ], local=True, name="win_attn")
            wo = _bf(win_w_o[occ])
            x = _res_proj(o, wo, x, mod, gate_i=2, name="win_out")
            if not last:
                oc = _win_attn(None, qkvc, win_sink[occ], local=False, name="win_attn_ctx")
                xc = _res_proj(oc, wo, xc, modc, gate_i=2, name="win_out_ctx")
        elif m == 1:
            lambda_init = 0.8 - 0.6 * math.exp(-0.3 * i)
            w = _bf(diff_w_qkv[occ])
            qkv = _proj(x, g1, mod, w, cos, sin, n_rope=2 * d, n_q=d, scale_i=1, shift_i=0, name="diff_qkv")
            qkvc = _proj(xc, g1, modc, w, None, None, n_rope=0, n_q=d, scale_i=1, shift_i=0, name="diff_qkv_ctx")
            o = _diff_attn(qkv, qkvc, diff_lambda[occ], diff_subln_g[occ], local=True,
                           lambda_init=lambda_init, name="diff_attn")
            wo = _bf(diff_w_o[occ])
            x = _res_proj(o, wo, x, mod, gate_i=2, name="diff_out")
            if not last:
                oc = _diff_attn(None, qkvc, diff_lambda[occ], diff_subln_g[occ], local=False,
                                lambda_init=lambda_init, name="diff_attn_ctx")
                xc = _res_proj(oc, wo, xc, modc, gate_i=2, name="diff_out_ctx")
        elif m == 2:
            x = _pool_mix(x, g1, mod, pool_w_group[occ], pool_b_group[occ], pool_scale[occ], name="pool")
            if not last:
                xc = _pool_mix(xc, g1, modc, pool_w_group[occ], pool_b_group[occ], pool_scale[occ], name="pool_ctx")
        else:
            p = dict(mu=rwkv_mu[occ], w_rkv=rwkv_w_rkv[occ], w0=rwkv_w0[occ], w_a1=rwkv_w_a1[occ],
                     w_a2=rwkv_w_a2[occ], a0=rwkv_a0[occ], a_a1=rwkv_a_a1[occ], a_a2=rwkv_a_a2[occ],
                     g1=rwkv_g1[occ], g2=rwkv_g2[occ], k_k=rwkv_k_k[occ], k_a=rwkv_k_a[occ])
            assert last, "the context stream's RWKV output path is only needed for non-final layers"
            r, v, kk, lw, bb, kd, g = _rwkv_features(x, g1, mod, p, with_out=True, name="rwkv_feat")
            vc, kkc, lwc, bc, kdc = _rwkv_features(xc, g1, modc, p, with_out=False, name="rwkv_feat_ctx")
            nh = d // HEAD_DIM
            s0 = jnp.zeros((2, b, nh, HEAD_DIM, HEAD_DIM), F32)
            (s_ctx,) = _rwkv_scan(kkc, vc, None, lwc, bc, kdc, s0, name="rwkv_scan_ctx")
            o, _ = _rwkv_scan(kk, v, r, lw, bb, kd, s_ctx, name="rwkv_scan")
            x = _rwkv_output(o, r, v, kd, g, rwkv_r_k[occ], rwkv_ln_w[occ], rwkv_ln_b[occ], rwkv_w_o[occ],
                             x, mod, name="rwkv_out")
        x, xc = _hier_moe(x, None if last else xc, mod, modc, norm_g[i, 1], moe_wg[i], moe_bg[i], moe_we[i],
                          moe_be[i], moe_w_up[i], moe_w_down[i], i, final_g if last else None)
    return x
```

```python
import functools
import math

import jax
import jax.numpy as jnp
from jax import lax
from jax.experimental import pallas as pl
from jax.experimental.pallas import tpu as pltpu

F32 = jnp.float32
BF16 = jnp.bfloat16

HEAD_DIM = 64
GRID_W = 64
ROPE_BASE = 10000.0
NORM_EPS = 1e-6
NEG_INF = -1e30
WIN_KV_HEADS = 4
WIN_BLOCK = 128
POOL_WINDOWS = (2, 4, 8, 16)
POOL_HALO = 8
GN_EPS = 64e-5
N_GROUPS = 4
EXPERTS_PER_GROUP = 8
N_EXPERTS = N_GROUPS * EXPERTS_PER_GROUP
MOE_BLOCK = 512
DIFF_Q_ROWS = 1024
DIFF_Q_SUB = 256
RWKV_CHUNK = 64
LANES = 128
V7X_VMEM_LIMIT = 48 * 1024 * 1024
HI = lax.Precision.HIGHEST


def _cp(sem, vmem=V7X_VMEM_LIMIT):
    return pltpu.CompilerParams(dimension_semantics=sem, vmem_limit_bytes=vmem)


def _bf(x):
    return x.astype(BF16)


def _dot(a, b, precision=None):
    return jnp.dot(a, b, preferred_element_type=F32, precision=precision)


def _dot_t(a, b, precision=None):
    return lax.dot_general(a, b, (((1,), (1,)), ((), ())), preferred_element_type=F32, precision=precision)


def _dot_l(a, b, precision=None):
    return lax.dot_general(a, b, (((0,), (0,)), ((), ())), preferred_element_type=F32, precision=precision)


def _mm(a, b):
    return _dot(_bf(a), _bf(b))


def _mm_t(a, b):
    return _dot_t(_bf(a), _bf(b))


def _mm_l(a, b):
    return _dot_l(_bf(a), _bf(b))


def _split3(x):
    hi = _bf(x)
    r1 = x - hi.astype(F32)
    mid = _bf(r1)
    return hi, mid, _bf(r1 - mid.astype(F32))


def _cumulate(tri, x):
    hi, mid, lo = _split3(x)
    return _dot(tri, hi) + _dot(tri, mid) + _dot(tri, lo)


def _head_sums(x):
    i = lax.broadcasted_iota(jnp.int32, (LANES, LANES), 0) // HEAD_DIM
    j = lax.broadcasted_iota(jnp.int32, (LANES, LANES), 1) // HEAD_DIM
    ones = _bf(jnp.where(i == j, 1.0, 0.0))
    out = []
    for t in range(x.shape[1] // LANES):
        hi, mid, lo = _split3(x[:, t * LANES:(t + 1) * LANES])
        out.append(_dot(hi, ones) + _dot(mid, ones) + _dot(lo, ones))
    return jnp.concatenate(out, axis=1)


def _norm_mod(x, g, scale, shift):
    ms = jnp.mean(x * x, axis=-1, keepdims=True)
    y = x * lax.rsqrt(ms + NORM_EPS) * g
    return y * (1.0 + scale) + shift


def _sigmoid(x):
    return 1.0 / (1.0 + jnp.exp(-x))


def _row_tile(n, pref):
    t = min(pref, n)
    assert n % t == 0
    return t


def _mod_body(c_ref, w_ref, b_ref, o_ref):
    c = c_ref[...]
    s = c * _sigmoid(c)
    o_ref[0] = _dot(_bf(s), _bf(w_ref[0])) + b_ref[0]


def _mod_all(c_all, mod_w, mod_b):
    depth, d, n = mod_w.shape
    r = c_all.shape[0]
    tn = 1536
    return pl.pallas_call(
        _mod_body,
        grid=(depth, n // tn),
        in_specs=[
            pl.BlockSpec((r, d), lambda i, j: (0, 0)),
            pl.BlockSpec((1, d, tn), lambda i, j: (i, 0, j)),
            pl.BlockSpec((1, 1, tn), lambda i, j: (i, 0, j)),
        ],
        out_specs=pl.BlockSpec((1, r, tn), lambda i, j: (i, 0, j)),
        out_shape=jax.ShapeDtypeStruct((depth, r, n), F32),
        compiler_params=_cp(("parallel", "parallel")),
        name="adaln_mod",
    )(c_all, mod_w, mod_b.reshape(depth, 1, n))


def _mod_spec(mod):
    if mod.shape[0] == 1:
        return pl.BlockSpec((1,) + mod.shape[1:], lambda b, i: (0, 0, 0))
    return pl.BlockSpec((1,) + mod.shape[1:], lambda b, i: (b, 0, 0))


def _rope_tile(y, cos, sin, first_half):
    fwd = pltpu.roll(y, LANES - 16, 1)
    bwd = pltpu.roll(y, 16, 1)
    rot = jnp.where(first_half, -fwd, bwd)
    return y * cos + rot * sin


def _proj_body(*refs, n_rope, n_q, scale_i, shift_i):
    if n_rope:
        x_ref, g_ref, m_ref, w_ref, cos_ref, sin_ref, o_ref = refs
    else:
        x_ref, g_ref, m_ref, w_ref, o_ref = refs
    m = m_ref[0]
    h = _norm_mod(x_ref[0], g_ref[...], m[scale_i:scale_i + 1], m[shift_i:shift_i + 1])
    y = _dot(_bf(h), w_ref[...])
    n = y.shape[1]
    if n_rope:
        cos, sin = cos_ref[...], sin_ref[...]
        first_half = (lax.broadcasted_iota(jnp.int32, cos.shape, 1) & 31) < 16
    for j in range(n // LANES):
        blk = y[:, j * LANES:(j + 1) * LANES]
        if j * LANES < n_q:
            blk = blk * (HEAD_DIM ** -0.5)
        if j * LANES < n_rope:
            blk = _rope_tile(blk, cos, sin, first_half)
        o_ref[0, :, j * LANES:(j + 1) * LANES] = _bf(blk)


def _proj(x, g, mod, w, cos, sin, *, n_rope, n_q, scale_i, shift_i, name):
    b, l, d = x.shape
    n = w.shape[1]
    tm = _row_tile(l, 256)
    in_specs = [
        pl.BlockSpec((1, tm, d), lambda bi, i: (bi, i, 0)),
        pl.BlockSpec((1, d), lambda bi, i: (0, 0)),
        _mod_spec(mod),
        pl.BlockSpec((d, n), lambda bi, i: (0, 0)),
    ]
    args = [x, g.reshape(1, d), mod, w]
    if n_rope:
        in_specs += [pl.BlockSpec((tm, LANES), lambda bi, i: (i, 0))] * 2
        args += [cos, sin]
    return pl.pallas_call(
        functools.partial(_proj_body, n_rope=n_rope, n_q=n_q, scale_i=scale_i, shift_i=shift_i),
        grid=(b, l // tm),
        in_specs=in_specs,
        out_specs=pl.BlockSpec((1, tm, n), lambda bi, i: (bi, i, 0)),
        out_shape=jax.ShapeDtypeStruct((b, l, n), BF16),
        compiler_params=_cp(("parallel", "parallel")),
        name=name,
    )(*args)


def _res_body(a_ref, w_ref, x_ref, m_ref, o_ref, *, gate_i):
    y = _dot(a_ref[0], w_ref[...])
    o_ref[0] = x_ref[0] + m_ref[0][gate_i:gate_i + 1] * y


def _res_proj(a, w, x, mod, *, gate_i, name):
    b, l, d = x.shape
    k = a.shape[-1]
    tm = _row_tile(l, 512)
    in_specs = [
        pl.BlockSpec((1, tm, k), lambda bi, i: (bi, i, 0)),
        pl.BlockSpec((k, d), lambda bi, i: (0, 0)),
        pl.BlockSpec((1, tm, d), lambda bi, i: (bi, i, 0)),
        _mod_spec(mod),
    ]
    args = [a, w, x, mod]
    return pl.pallas_call(
        functools.partial(_res_body, gate_i=gate_i),
        grid=(b, l // tm),
        in_specs=in_specs,
        out_specs=pl.BlockSpec((1, tm, d), lambda bi, i: (bi, i, 0)),
        out_shape=jax.ShapeDtypeStruct((b, l, d), F32),
        input_output_aliases={2: 0},
        compiler_params=_cp(("parallel", "parallel")),
        name=name,
    )(*args)


def _win_body(*refs, seq, local, n_heads):
    if local:
        q_ref, kp_ref, kc_ref, kn_ref, vp_ref, vc_ref, vn_ref, kx_ref, vx_ref, sink_ref, o_ref = refs
    else:
        q_ref, kx_ref, vx_ref, sink_ref, o_ref = refs
    tq = q_ref.shape[1]
    grp = n_heads // WIN_KV_HEADS
    n = pl.program_id(1)
    if local:
        span = 3 * tq
        row = lax.broadcasted_iota(jnp.int32, (grp * tq, span), 0) & (tq - 1)
        col = lax.broadcasted_iota(jnp.int32, (grp * tq, span), 1)
        rel = col - tq - row
        key_pos = n * tq - tq + col
        mask = (jnp.abs(rel) <= tq) & (key_pos >= 0) & (key_pos < seq)
    hks = range(WIN_KV_HEADS)
    ks = [slice(hk * HEAD_DIM, (hk + 1) * HEAD_DIM) for hk in hks]
    qh = [jnp.concatenate(
        [q_ref[0, :, (hk * grp + g) * HEAD_DIM:(hk * grp + g + 1) * HEAD_DIM] for g in range(grp)], axis=0)
        for hk in hks]
    sink = [jnp.concatenate([jnp.full((tq, 1), sink_ref[hk * grp + g], F32) for g in range(grp)], axis=0)
            for hk in hks]
    s_ctx = [_dot_t(qh[hk], kx_ref[0, :, ks[hk]]) for hk in hks]
    m = [jnp.maximum(jnp.max(s_ctx[hk], axis=1, keepdims=True), sink[hk]) for hk in hks]
    if local:
        s_loc = [jnp.where(mask, _dot_t(qh[hk], jnp.concatenate(
            [kp_ref[0, :, ks[hk]], kc_ref[0, :, ks[hk]], kn_ref[0, :, ks[hk]]], axis=0)), NEG_INF) for hk in hks]
        m = [jnp.maximum(m[hk], jnp.max(s_loc[hk], axis=1, keepdims=True)) for hk in hks]
    p_ctx = [jnp.exp(s_ctx[hk] - m[hk]) for hk in hks]
    den = [jnp.sum(p_ctx[hk], axis=1, keepdims=True) + jnp.exp(sink[hk] - m[hk]) for hk in hks]
    o = [_dot(_bf(p_ctx[hk]), vx_ref[0, :, ks[hk]]) for hk in hks]
    if local:
        p_loc = [jnp.exp(s_loc[hk] - m[hk]) for hk in hks]
        den = [den[hk] + jnp.sum(p_loc[hk], axis=1, keepdims=True) for hk in hks]
        o = [o[hk] + _dot(_bf(p_loc[hk]), jnp.concatenate(
            [vp_ref[0, :, ks[hk]], vc_ref[0, :, ks[hk]], vn_ref[0, :, ks[hk]]], axis=0)) for hk in hks]
    for hk in hks:
        oh = o[hk] / den[hk]
        for g in range(grp):
            hq = hk * grp + g
            o_ref[0, :, hq * HEAD_DIM:(hq + 1) * HEAD_DIM] = _bf(oh[g * tq:(g + 1) * tq])


def _win_attn(qkv, qkvc, sink, *, local, name):
    src = qkv if local else qkvc
    b, l, _ = src.shape
    c = qkvc.shape[1]
    n_heads = sink.shape[0]
    d = n_heads * HEAD_DIM
    kvw = WIN_KV_HEADS * HEAD_DIM
    kcol = d // kvw
    tq = WIN_BLOCK
    nb = l // tq
    in_specs = [pl.BlockSpec((1, tq, d), lambda bi, i: (bi, i, 0))]
    args = [src]
    if local:
        for colb in (kcol, kcol + 1):
            in_specs += [
                pl.BlockSpec((1, tq, kvw), lambda bi, i, colb=colb: (bi, jnp.maximum(i - 1, 0), colb)),
                pl.BlockSpec((1, tq, kvw), lambda bi, i, colb=colb: (bi, i, colb)),
                pl.BlockSpec((1, tq, kvw), lambda bi, i, colb=colb: (bi, jnp.minimum(i + 1, nb - 1), colb)),
            ]
            args += [qkv, qkv, qkv]
    in_specs += [
        pl.BlockSpec((1, c, kvw), lambda bi, i: (bi, 0, kcol)),
        pl.BlockSpec((1, c, kvw), lambda bi, i: (bi, 0, kcol + 1)),
        pl.BlockSpec(memory_space=pltpu.SMEM),
    ]
    args += [qkvc, qkvc, sink]
    return pl.pallas_call(
        functools.partial(_win_body, seq=l, local=local, n_heads=n_heads),
        grid=(b, nb),
        in_specs=in_specs,
        out_specs=pl.BlockSpec((1, tq, d), lambda bi, i: (bi, i, 0)),
        out_shape=jax.ShapeDtypeStruct((b, l, d), BF16),
        compiler_params=_cp(("parallel", "parallel")),
        name=name,
    )(*args)


def _diff_body(*refs, n_lat, tk, sub, lambda_init):
    if n_lat:
        lam_ref, g_ref, q_ref, kl_ref, vl_ref, kx_ref, vx_ref, o_ref, s_ref = refs
    else:
        lam_ref, g_ref, q_ref, kx_ref, vx_ref, o_ref, s_ref = refs
    hw = 2 * HEAD_DIM
    n_sub = q_ref.shape[1] // sub
    lane = lax.broadcasted_iota(jnp.int32, (sub, hw), 1)
    chunks = [(kl_ref, vl_ref, i * tk, tk, i * tk) for i in range(n_lat)] if n_lat else []
    chunks.append((kx_ref, vx_ref, 0, kx_ref.shape[1], n_lat * tk))
    lam = lam_ref[...]
    lam_full = (jnp.exp(jnp.sum(lam[0:1] * lam[1:2], axis=1, keepdims=True))
                - jnp.exp(jnp.sum(lam[2:3] * lam[3:4], axis=1, keepdims=True)) + lambda_init)

    def stacked_q(j):
        q = q_ref[0, j * sub:(j + 1) * sub, :]
        zero = jnp.zeros_like(q)
        return jnp.concatenate([jnp.where(lane < HEAD_DIM, q, zero), jnp.where(lane >= HEAD_DIM, q, zero)], axis=0)

    def score_chunk(j, qq, ch, mx):
        k_ref, _, row, n, col = ch
        s = _dot_t(qq, k_ref[0, row:row + n, :])
        s_ref[j % 2, :, col:col + n] = s
        for t in range(n // LANES):
            mx = jnp.maximum(mx, s[:, t * LANES:(t + 1) * LANES])
        return mx

    def value_chunk(j, m, ch, acc):
        _, v_ref, row, n, col = ch
        p = jnp.concatenate(
            [jnp.exp(s_ref[j % 2, :, col + t * LANES:col + (t + 1) * LANES] - m) for t in range(n // LANES)], axis=1)
        v = v_ref[0, row:row + n, :]
        return acc + _dot(_bf(p), jnp.concatenate([v, jnp.ones_like(v)], axis=1))

    def finish(j, acc):
        a = acc[:, :hw] / acc[:, hw:]
        o = a[:sub] - lam_full * a[sub:]
        ms = jnp.mean(o * o, axis=-1, keepdims=True)
        o = o * lax.rsqrt(ms + NORM_EPS) * g_ref[...] * (1.0 - lambda_init)
        o_ref[0, j * sub:(j + 1) * sub, :] = _bf(o)

    m_prev = None
    for j in range(n_sub + 1):
        if j < n_sub:
            qq = stacked_q(j)
            mx = jnp.full((2 * sub, LANES), NEG_INF, F32)
        if j > 0:
            acc = jnp.zeros((2 * sub, 2 * hw), F32)
        for ch in chunks:
            if j > 0:
                acc = value_chunk(j - 1, m_prev, ch, acc)
            if j < n_sub:
                mx = score_chunk(j, qq, ch, mx)
        if j > 0:
            finish(j - 1, acc)
        if j < n_sub:
            m_prev = jnp.broadcast_to(jnp.max(mx, axis=1, keepdims=True), (2 * sub, LANES))


def _diff_attn(qkv, qkvc, lam, subln_g, *, local, lambda_init, name):
    src = qkv if local else qkvc
    b, l, n3 = src.shape
    d = n3 // 3
    c = qkvc.shape[1]
    hw = 2 * HEAD_DIM
    nh = d // hw
    tq = _row_tile(l, DIFF_Q_ROWS)
    sub = min(DIFF_Q_SUB, tq)
    tk = 512
    in_specs = [
        pl.BlockSpec((4, HEAD_DIM), lambda bi, h, i: (0, 0)),
        pl.BlockSpec((1, hw), lambda bi, h, i: (0, 0)),
        pl.BlockSpec((1, tq, hw), lambda bi, h, i: (bi, i, h)),
    ]
    args = [lam, subln_g.reshape(1, hw), src]
    n_lat = 0
    if local:
        s = qkv.shape[1]
        n_lat = s // tk
        in_specs += [
            pl.BlockSpec((1, s, hw), lambda bi, h, i: (bi, 0, nh + h)),
            pl.BlockSpec((1, s, hw), lambda bi, h, i: (bi, 0, 2 * nh + h)),
        ]
        args += [qkv, qkv]
    in_specs += [
        pl.BlockSpec((1, c, hw), lambda bi, h, i: (bi, 0, nh + h)),
        pl.BlockSpec((1, c, hw), lambda bi, h, i: (bi, 0, 2 * nh + h)),
    ]
    args += [qkvc, qkvc]
    return pl.pallas_call(
        functools.partial(_diff_body, n_lat=n_lat, tk=tk, sub=sub, lambda_init=lambda_init),
        grid=(b, nh, l // tq),
        in_specs=in_specs,
        out_specs=pl.BlockSpec((1, tq, hw), lambda bi, h, i: (bi, i, h)),
        out_shape=jax.ShapeDtypeStruct((b, l, d), BF16),
        scratch_shapes=[pltpu.VMEM((2, 2 * sub, n_lat * tk + c), F32)],
        compiler_params=_cp(("parallel", "parallel", "parallel")),
        name=name,
    )(*args)


def _halo_specs(l, tm, d):
    nh = l // POOL_HALO
    per = tm // POOL_HALO
    return [
        pl.BlockSpec((1, POOL_HALO, d), lambda bi, i: (bi, jnp.maximum(i * per - 1, 0), 0)),
        pl.BlockSpec((1, tm, d), lambda bi, i: (bi, i, 0)),
        pl.BlockSpec((1, POOL_HALO, d), lambda bi, i: (bi, jnp.minimum((i + 1) * per, nh - 1), 0)),
    ]


def _fill_normed(h_ref, xp_ref, x_ref, xn_ref, g, scale, shift):
    tm = x_ref.shape[1]
    h_ref[0:POOL_HALO, :] = _norm_mod(xp_ref[0], g, scale, shift)
    h_ref[POOL_HALO:POOL_HALO + tm, :] = _norm_mod(x_ref[0], g, scale, shift)
    h_ref[POOL_HALO + tm:2 * POOL_HALO + tm, :] = _norm_mod(xn_ref[0], g, scale, shift)


def _pool_body(xp_ref, x_ref, xn_ref, g_ref, m_ref, w_ref, b_ref, s_ref, o_ref, h_ref, *, seq):
    tm = x_ref.shape[1]
    d = x_ref.shape[2]
    gd = d // len(POOL_WINDOWS)
    m = m_ref[0]
    _fill_normed(h_ref, xp_ref, x_ref, xn_ref, g_ref[...], m[1:2], m[0:1])
    pos = pl.program_id(1) * tm + lax.broadcasted_iota(jnp.int32, (tm, 1), 0)
    x = x_ref[0]
    for gi, w in enumerate(POOL_WINDOWS):
        cols = slice(gi * gd, (gi + 1) * gd)
        acc = jnp.zeros((tm, gd), F32)
        for off in range(-(w // 2), w - w // 2):
            valid = (pos + off >= 0) & (pos + off < seq)
            acc = acc + jnp.where(valid, h_ref[POOL_HALO + off:POOL_HALO + off + tm, cols], 0.0)
        lo = jnp.maximum(pos - w // 2, 0)
        hi = jnp.minimum(pos + w - w // 2, seq)
        y = acc / (hi - lo).astype(F32) - h_ref[POOL_HALO:POOL_HALO + tm, cols]
        y = (_dot(_bf(y), w_ref[gi]) + b_ref[:, cols]) * s_ref[:, cols]
        o_ref[0, :, cols] = x[:, cols] + m[2:3, cols] * y


def _pool_mix(x, g, mod, w_group, b_group, layer_scale, *, name):
    b, l, d = x.shape
    tm = _row_tile(l, 256)
    gd = d // len(POOL_WINDOWS)
    return pl.pallas_call(
        functools.partial(_pool_body, seq=l),
        grid=(b, l // tm),
        in_specs=_halo_specs(l, tm, d) + [
            pl.BlockSpec((1, d), lambda bi, i: (0, 0)),
            _mod_spec(mod),
            pl.BlockSpec((len(POOL_WINDOWS), gd, gd), lambda bi, i: (0, 0, 0)),
            pl.BlockSpec((1, d), lambda bi, i: (0, 0)),
            pl.BlockSpec((1, d), lambda bi, i: (0, 0)),
        ],
        out_specs=pl.BlockSpec((1, tm, d), lambda bi, i: (bi, i, 0)),
        out_shape=jax.ShapeDtypeStruct((b, l, d), F32),
        scratch_shapes=[pltpu.VMEM((tm + 2 * POOL_HALO, d), F32)],
        compiler_params=_cp(("parallel", "parallel")),
        name=name,
    )(x, x, x, g.reshape(1, d), mod, _bf(w_group), b_group.reshape(1, d), layer_scale.reshape(1, d))


def _softplus(z):
    return jnp.maximum(z, 0.0) + jnp.log1p(jnp.exp(-jnp.abs(z)))


def _rwkv_feat_body(*refs, seq, with_out):
    (xp_ref, x_ref, xn_ref, g_ref, m_ref, mu_ref, wrkv_ref, w0_ref, wa1_ref, wa2_ref, a0_ref, aa1_ref,
     aa2_ref, g1_ref, g2_ref, kk_ref, ka_ref) = refs[:17]
    if with_out:
        r_out, v_out, kk_out, lw_out, b_out, kd_out, g_out, h_ref = refs[17:]
    else:
        v_out, kk_out, lw_out, b_out, kd_out, h_ref = refs[17:]
    tm = x_ref.shape[1]
    m = m_ref[0]
    _fill_normed(h_ref, xp_ref, x_ref, xn_ref, g_ref[...], m[1:2], m[0:1])
    pos = pl.program_id(1) * tm + lax.broadcasted_iota(jnp.int32, (tm, 1), 0)
    t = h_ref[POOL_HALO:POOL_HALO + tm, :]
    dp = jnp.where(pos >= 1, h_ref[POOL_HALO - 1:POOL_HALO - 1 + tm, :], 0.0) - t
    dn = jnp.where(pos < seq - 1, h_ref[POOL_HALO + 1:POOL_HALO + 1 + tm, :], 0.0) - t

    def mix(i):
        return _bf(t + dp * mu_ref[0, i:i + 1, :] + dn * mu_ref[1, i:i + 1, :])

    k = _dot(mix(2), wrkv_ref[1])
    v_out[0] = _dot(mix(3), wrkv_ref[2])
    kk = k * kk_ref[...]
    kk = kk / jnp.maximum(jnp.sqrt(_head_sums(kk * kk)), 1e-12)
    kk_out[0] = kk
    xw, xa = mix(1), mix(4)
    for di in range(2):
        z = w0_ref[di:di + 1, :] + _dot(_bf(jnp.tanh(_dot(xw, wa1_ref[di]))), wa2_ref[di])
        w = -_softplus(-z) - 0.5
        lw_out[di, 0] = -jnp.exp(w)
        a = _sigmoid(a0_ref[di:di + 1, :] + _dot(_bf(_dot(xa, aa1_ref[di])), aa2_ref[di]))
        kd_out[di, 0] = k * (1.0 + (a - 1.0) * ka_ref[...])
        b_out[di, 0] = kk * a
    if with_out:
        r_out[0] = _dot(mix(0), wrkv_ref[0])
        g_out[0] = _dot(_bf(_sigmoid(_dot(mix(5), g1_ref[...]))), g2_ref[...])


def _rwkv_features(x, g, mod, p, *, with_out, name):
    b, l, d = x.shape
    tm = _row_tile(l, 128)
    const = lambda a: pl.BlockSpec(a.shape, lambda bi, i, nd=a.ndim: (0,) * nd)
    weights = [p["mu"], _bf(p["w_rkv"]), p["w0"], _bf(p["w_a1"]), _bf(p["w_a2"]), p["a0"], _bf(p["a_a1"]),
               _bf(p["a_a2"]), _bf(p["g1"]), _bf(p["g2"]), p["k_k"].reshape(1, d), p["k_a"].reshape(1, d)]
    one = pl.BlockSpec((1, tm, d), lambda bi, i: (bi, i, 0))
    two = pl.BlockSpec((2, 1, tm, d), lambda bi, i: (0, bi, i, 0))
    s1 = jax.ShapeDtypeStruct((b, l, d), F32)
    s2 = jax.ShapeDtypeStruct((2, b, l, d), F32)
    out_specs = [one, one, two, two, two]
    out_shape = [s1, s1, s2, s2, s2]
    if with_out:
        out_specs = [one] + out_specs + [one]
        out_shape = [s1] + out_shape + [s1]
    return pl.pallas_call(
        functools.partial(_rwkv_feat_body, seq=l, with_out=with_out),
        grid=(b, l // tm),
        in_specs=_halo_specs(l, tm, d) + [pl.BlockSpec((1, d), lambda bi, i: (0, 0)), _mod_spec(mod)]
        + [const(a) for a in weights],
        out_specs=out_specs,
        out_shape=out_shape,
        scratch_shapes=[pltpu.VMEM((tm + 2 * POOL_HALO, d), F32)],
        compiler_params=_cp(("parallel", "parallel")),
        name=name,
    )(x, x, x, g.reshape(1, d), mod, *weights)


def _scan_body(*refs, emit, n_chunks):
    if emit:
        kk_ref, v_ref, r_ref, lw_ref, b_ref, kd_ref, s0_ref, o_ref, sfin_ref, s_ref = refs
    else:
        kk_ref, v_ref, lw_ref, b_ref, kd_ref, s0_ref, sfin_ref, s_ref = refs
    hb = s_ref.shape[0]
    L = kk_ref.shape[1]
    rev = pl.program_id(0) == 1
    c = pl.program_id(2)

    @pl.when(c == 0)
    def _():
        s_ref[...] = s0_ref[0, 0]

    row = lax.broadcasted_iota(jnp.int32, (L, L), 0)
    col = lax.broadcasted_iota(jnp.int32, (L, L), 1)
    flip = rev.astype(jnp.int32)
    p_row = row + flip * (L - 1 - 2 * row)
    p_col = col + flip * (L - 1 - 2 * col)
    incl = p_col <= p_row
    strict = p_col < p_row
    tri = _bf(incl.astype(F32))
    eye = row == col
    assert L == HEAD_DIM
    levels = []
    m = 1
    while m < L:
        same = (p_row // (2 * m)) == (p_col // (2 * m))
        levels.append(same & ((p_row & (2 * m - 1)) >= m) & ((p_col & (2 * m - 1)) < m))
        m *= 2

    hs = range(hb)
    cut = lambda a: [a[:, h * HEAD_DIM:(h + 1) * HEAD_DIM] for h in hs]
    lw_all = lw_ref[0, 0]
    b_all = b_ref[0, 0]
    kd_all = kd_ref[0, 0]
    cl_all = _cumulate(tri, lw_all)
    tot_all = jnp.sum(lw_all, axis=0, keepdims=True)
    w_inv = jnp.exp(-cl_all)
    w_last = jnp.exp(tot_all - cl_all)
    kh = cut(kk_ref[0] * jnp.exp(cl_all - lw_all))
    bt, kt = cut(b_all * w_inv), cut(kd_all * w_inv)
    bt_l, kt_l = cut(b_all * w_last), cut(kd_all * w_last)
    w_tot = cut(jnp.exp(tot_all))
    v = cut(v_ref[0])
    s0 = [s_ref[h] for h in hs]
    rhs = [jnp.concatenate([bt[h], kt[h]], axis=0) for h in hs]
    if emit:
        rh = cut(r_ref[0] * jnp.exp(cl_all))
        lhs = [jnp.concatenate([kh[h], rh[h]], axis=0) for h in hs]
    else:
        lhs = kh
    big = [_mm_t(lhs[h], rhs[h]) for h in hs]
    m_b = [big[h][:L, :L] for h in hs]
    mkv = [_mm(jnp.where(strict, big[h][:L, L:], 0.0), v[h]) for h in hs]
    t = [jnp.where(eye, 1.0, 0.0) - jnp.where(levels[0], m_b[h], 0.0) for h in hs]
    for lm in levels[1:]:
        tl = [_mm(t[h], jnp.where(lm, m_b[h], 0.0)) for h in hs]
        t = [t[h] - _mm(tl[h], t[h]) for h in hs]
    y = [_mm(t[h], jnp.concatenate([kh[h], mkv[h]], axis=1)) for h in hs]
    gh = [_mm_l(y[h], bt_l[h]) for h in hs]
    vk = [_mm_l(v[h], kt_l[h]) for h in hs]
    g_mat = [jnp.where(eye, w_tot[h], 0.0) - gh[h][:HEAD_DIM] for h in hs]
    if emit:
        ab_y = [_mm(jnp.where(incl, big[h][L:, :L], 0.0), y[h]) for h in hs]
        akv = [_mm(jnp.where(incl, big[h][L:, L:], 0.0), v[h]) for h in hs]
        rs = [_mm_t(rh[h] - ab_y[h][:, :HEAD_DIM], s0[h]) for h in hs]
        o_ref[0, 0] = jnp.concatenate([rs[h] + akv[h] - ab_y[h][:, HEAD_DIM:] for h in hs], axis=1)
    sg = [_mm(s0[h], g_mat[h]) for h in hs]
    for h in hs:
        s_ref[h] = sg[h] + vk[h] - gh[h][HEAD_DIM:]

    @pl.when(c == n_chunks - 1)
    def _():
        sfin_ref[0, 0] = s_ref[...]


def _rwkv_scan(kk, v, r, lw, b, kd, s0, *, name):
    bsz, l, d = kk.shape
    dh = HEAD_DIM
    nh = d // dh
    L = RWKV_CHUNK
    nch = l // L
    emit = r is not None

    def chunk(di, c):
        return c + di * (nch - 1 - 2 * c)

    one = pl.BlockSpec((1, L, d), lambda di, bi, c: (bi, chunk(di, c), 0))
    two = pl.BlockSpec((1, 1, L, d), lambda di, bi, c: (di, bi, chunk(di, c), 0))
    st = pl.BlockSpec((1, 1, nh, dh, dh), lambda di, bi, c: (di, bi, 0, 0, 0))
    in_specs = [one, one] + ([one] if emit else []) + [two, two, two, st]
    args = [kk, v] + ([r] if emit else []) + [lw, b, kd, s0]
    s_shape = jax.ShapeDtypeStruct((2, bsz, nh, dh, dh), F32)
    if emit:
        out_specs = [two, st]
        out_shape = [jax.ShapeDtypeStruct((2, bsz, l, d), F32), s_shape]
    else:
        out_specs = [st]
        out_shape = [s_shape]
    return pl.pallas_call(
        functools.partial(_scan_body, emit=emit, n_chunks=nch),
        grid=(2, bsz, nch),
        in_specs=in_specs,
        out_specs=out_specs,
        out_shape=out_shape,
        scratch_shapes=[pltpu.VMEM((nh, dh, dh), F32)],
        compiler_params=_cp(("parallel", "parallel", "arbitrary")),
        name=name,
    )(*args)


def _rwkv_out_body(o_ref, r_ref, v_ref, kd_ref, g_ref, rk_ref, lnw_ref, lnb_ref, w_ref, x_ref, m_ref, out_ref):
    o = o_ref[0, 0] + o_ref[1, 0]
    mean = _head_sums(o) * (1.0 / HEAD_DIM)
    cen = o - mean
    var = _head_sums(cen * cen) * (1.0 / HEAD_DIM)
    on = cen * lax.rsqrt(var + GN_EPS) * lnw_ref[...] + lnb_ref[...]
    bonus = _head_sums(r_ref[0] * (kd_ref[0, 0] + kd_ref[1, 0]) * rk_ref[...]) * v_ref[0]
    a = (on + bonus) * g_ref[0]
    out_ref[0] = x_ref[0] + m_ref[0][2:3] * _dot(_bf(a), w_ref[...])


def _rwkv_output(o, r, v, kd, g, r_k, ln_w, ln_b, w_o, x, mod, *, name):
    b, l, d = x.shape
    tm = _row_tile(l, 256)
    row = pl.BlockSpec((1, tm, d), lambda bi, i: (bi, i, 0))
    row2 = pl.BlockSpec((2, 1, tm, d), lambda bi, i: (0, bi, i, 0))
    small = pl.BlockSpec((1, d), lambda bi, i: (0, 0))
    return pl.pallas_call(
        _rwkv_out_body,
        grid=(b, l // tm),
        in_specs=[row2, row, row, row2, row, small, small, small,
                  pl.BlockSpec((d, d), lambda bi, i: (0, 0)), row, _mod_spec(mod)],
        out_specs=row,
        out_shape=jax.ShapeDtypeStruct((b, l, d), F32),
        input_output_aliases={9: 0},
        compiler_params=_cp(("parallel", "parallel")),
        name=name,
    )(o, r, v, kd, g, r_k.reshape(1, d), ln_w.reshape(1, d), ln_b.reshape(1, d), _bf(w_o), x, mod)


def _router_body(*refs, lat_tiles):
    if lat_tiles is None:
        x_ref, g_ref, m_ref, whi_ref, wlo_ref, b_ref, h_ref, sel_ref, cnt_ref, run_ref = refs
        xt, m = x_ref[0], m_ref[0]
    else:
        x_ref, xc_ref, g_ref, m_ref, mc_ref, whi_ref, wlo_ref, b_ref, h_ref, sel_ref, cnt_ref, run_ref = refs
        is_ctx = pl.program_id(1) >= lat_tiles
        xt = jnp.where(is_ctx, xc_ref[0], x_ref[0])
        m = jnp.where(is_ctx, mc_ref[0], m_ref[0])

    @pl.when((pl.program_id(0) == 0) & (pl.program_id(1) == 0))
    def _():
        run_ref[...] = jnp.zeros_like(run_ref)

    h = _norm_mod(xt, g_ref[...], m[4:5], m[3:4])
    hi = _bf(h)
    h_ref[0] = h
    lo_part = _bf(h - hi.astype(F32))
    lg = _dot(hi, whi_ref[...]) + _dot(hi, wlo_ref[...]) + _dot(lo_part, whi_ref[...]) + b_ref[...]
    lane = lax.broadcasted_iota(jnp.int32, lg.shape, 1)
    big = jnp.int32(1 << 20)
    g_logit = jnp.where(lane < N_GROUPS, lg, -jnp.inf)
    g_max = jnp.max(g_logit, axis=1, keepdims=True)
    gsel = jnp.min(jnp.where(g_logit == g_max, lane, big), axis=1, keepdims=True)
    p_grp = 1.0 / jnp.sum(jnp.exp(g_logit - g_max), axis=1, keepdims=True)
    lo = N_GROUPS + gsel * EXPERTS_PER_GROUP
    e_logit = jnp.where((lane >= lo) & (lane < lo + EXPERTS_PER_GROUP), lg, -jnp.inf)
    e_max = jnp.max(e_logit, axis=1, keepdims=True)
    i1 = jnp.min(jnp.where(e_logit == e_max, lane, big), axis=1, keepdims=True)
    rest = jnp.where(lane == i1, -jnp.inf, e_logit)
    e2 = jnp.max(rest, axis=1, keepdims=True)
    i2 = jnp.min(jnp.where(rest == e2, lane, big), axis=1, keepdims=True)
    q2 = jnp.exp(e2 - e_max)
    w1 = p_grp / (1.0 + q2)
    w2 = p_grp * q2 / (1.0 + q2)
    pick1, pick2 = lane == i1, lane == i2
    both = jnp.where(pick1 | pick2, 1.0, 0.0)
    tm = lg.shape[0]
    earlier = lax.broadcasted_iota(jnp.int32, (tm, tm), 1) < lax.broadcasted_iota(jnp.int32, (tm, tm), 0)
    ahead = _dot(_bf(earlier.astype(F32)), _bf(both)) + run_ref[...]
    r1 = jnp.sum(jnp.where(pick1, ahead, 0.0), axis=1, keepdims=True)
    r2 = jnp.sum(jnp.where(pick2, ahead, 0.0), axis=1, keepdims=True)
    run_ref[...] = run_ref[...] + jnp.sum(both, axis=0, keepdims=True)
    cnt_ref[...] = run_ref[...]
    cols = ((i1 - N_GROUPS).astype(F32), (i2 - N_GROUPS).astype(F32), w1, w2, r1, r2)
    sel = jnp.zeros_like(lg)
    for j, val in enumerate(cols):
        sel = jnp.where(lane == j, val, sel)
    sel_ref[0] = sel


def _router(x, xc, g, mod, modc, w_hi, w_lo, b_r, *, name):
    b, s, d = x.shape
    c = 0 if xc is None else xc.shape[1]
    tm = _row_tile(s, 256)
    assert c % tm == 0
    lat_tiles = s // tm
    l = s + c
    wide = pl.BlockSpec((1, LANES), lambda bi, i: (0, 0))
    weights = [pl.BlockSpec((d, LANES), lambda bi, i: (0, 0))] * 2 + [wide]
    gspec = pl.BlockSpec((1, d), lambda bi, i: (0, 0))
    if xc is None:
        in_specs = [pl.BlockSpec((1, tm, d), lambda bi, i: (bi, i, 0)), gspec, _mod_spec(mod)] + weights
        args = [x, g.reshape(1, d), mod, w_hi, w_lo, b_r]
    else:
        in_specs = [
            pl.BlockSpec((1, tm, d), lambda bi, i: (bi, jnp.minimum(i, lat_tiles - 1), 0)),
            pl.BlockSpec((1, tm, d), lambda bi, i: (bi, jnp.maximum(i - lat_tiles, 0), 0)),
            gspec, _mod_spec(mod), _mod_spec(modc)] + weights
        args = [x, xc, g.reshape(1, d), mod, modc, w_hi, w_lo, b_r]
    return pl.pallas_call(
        functools.partial(_router_body, lat_tiles=None if xc is None else lat_tiles),
        grid=(b, l // tm),
        in_specs=in_specs,
        out_specs=[pl.BlockSpec((1, tm, d), lambda bi, i: (bi, i, 0)),
                   pl.BlockSpec((1, tm, LANES), lambda bi, i: (bi, i, 0)),
                   wide],
        out_shape=[jax.ShapeDtypeStruct((b, l, d), F32), jax.ShapeDtypeStruct((b, l, LANES), F32),
                   jax.ShapeDtypeStruct((1, LANES), F32)],
        scratch_shapes=[pltpu.VMEM((1, LANES), F32)],
        compiler_params=_cp(("arbitrary", "arbitrary")),
        name=name,
    )(*args)


def _expert_body(be_ref, nu_ref, x_ref, wu_ref, wd_ref, o_ref, wub_ref, wdb_ref):
    i = pl.program_id(0)
    ff = wd_ref.shape[1]

    @pl.when((i == 0) | (be_ref[i] != be_ref[jnp.maximum(i - 1, 0)]))
    def _():
        wub_ref[...] = _bf(wu_ref[0])
        wdb_ref[...] = _bf(wd_ref[0])

    @pl.when(i < nu_ref[0])
    def _():
        u = _dot(_bf(x_ref[...]), wub_ref[...])
        gate = u[:, :ff]
        act = gate * _sigmoid(gate) * u[:, ff:]
        o_ref[...] = _dot(_bf(act), wdb_ref[...])

    @pl.when(i >= nu_ref[0])
    def _():
        o_ref[...] = jnp.zeros_like(o_ref)


def _experts(xb, blk_expert, n_used, w_up, w_down):
    rows, d = xb.shape
    nb = rows // MOE_BLOCK
    ff2 = w_up.shape[2]
    ff = w_down.shape[1]
    return pl.pallas_call(
        _expert_body,
        grid_spec=pltpu.PrefetchScalarGridSpec(
            num_scalar_prefetch=2,
            grid=(nb,),
            in_specs=[
                pl.BlockSpec((MOE_BLOCK, d), lambda i, be, nu: (i, 0)),
                pl.BlockSpec((1, d, ff2), lambda i, be, nu: (be[i], 0, 0)),
                pl.BlockSpec((1, ff, d), lambda i, be, nu: (be[i], 0, 0)),
            ],
            out_specs=pl.BlockSpec((MOE_BLOCK, d), lambda i, be, nu: (i, 0)),
            scratch_shapes=[pltpu.VMEM((d, ff2), BF16), pltpu.VMEM((ff, d), BF16)],
        ),
        out_shape=jax.ShapeDtypeStruct((rows, d), F32),
        compiler_params=_cp(("arbitrary",)),
        name="moe_experts",
    )(blk_expert, n_used, xb, w_up, w_down)


def _moe_res_body(*refs, final):
    if final:
        x_ref, y0_ref, y1_ref, sel_ref, m_ref, fg_ref, o_ref = refs
    else:
        x_ref, y0_ref, y1_ref, sel_ref, m_ref, o_ref = refs
    sel = sel_ref[0]
    y = y0_ref[0] * sel[:, 2:3] + y1_ref[0] * sel[:, 3:4]
    x = x_ref[0] + m_ref[0][5:6] * y
    if final:
        ms = jnp.mean(x * x, axis=-1, keepdims=True)
        x = x * lax.rsqrt(ms + NORM_EPS) * fg_ref[...]
    o_ref[0] = x


def _moe_res(x, y0, y1, sel, mod, final_g=None, *, sel_row0=0, name):
    b, l, d = x.shape
    tm = _row_tile(l, 512)
    assert sel_row0 % tm == 0
    off = sel_row0 // tm
    row = pl.BlockSpec((1, tm, d), lambda bi, i: (bi, i, 0))
    in_specs = [row, row, row, pl.BlockSpec((1, tm, LANES), lambda bi, i: (bi, off + i, 0)), _mod_spec(mod)]
    args = [x, y0, y1, sel, mod]
    if final_g is not None:
        in_specs.append(pl.BlockSpec((1, d), lambda bi, i: (0, 0)))
        args.append(final_g.reshape(1, d))
    return pl.pallas_call(
        functools.partial(_moe_res_body, final=final_g is not None),
        grid=(b, l // tm),
        in_specs=in_specs,
        out_specs=row,
        out_shape=jax.ShapeDtypeStruct((b, l, d), F32),
        input_output_aliases={0: 0},
        compiler_params=_cp(("parallel", "parallel")),
        name=name,
    )(*args)


def _dispatch_tables(eids, counts):
    flat_e = eids.reshape(-1)
    a = flat_e.shape[0]
    order = jnp.argsort(flat_e)
    start = jnp.cumsum(counts) - counts
    nblk = (counts + MOE_BLOCK - 1) // MOE_BLOCK
    blk_end = jnp.cumsum(nblk)
    blk_start = blk_end - nblk
    nb = -(-a // MOE_BLOCK) + N_EXPERTS
    blk = jnp.arange(nb, dtype=jnp.int32)
    blk_expert = jnp.minimum(jnp.sum((blk[:, None] >= blk_end[None, :]).astype(jnp.int32), axis=1), N_EXPERTS - 1)
    local = jnp.arange(MOE_BLOCK, dtype=jnp.int32)[None, :] + ((blk - blk_start[blk_expert]) * MOE_BLOCK)[:, None]
    sorted_idx = jnp.clip(start[blk_expert][:, None] + local, 0, a - 1)
    spread = (jnp.arange(nb * MOE_BLOCK, dtype=jnp.int32) % (a // 2)).reshape(nb, MOE_BLOCK)
    src = jnp.where(local < counts[blk_expert][:, None], order[sorted_idx] // 2, spread).reshape(nb * MOE_BLOCK)
    return src, blk_start, blk_expert, blk_end[-1:].astype(jnp.int32)


def _row_positions(sel, blk_start):
    eid = sel[..., 0:2].astype(jnp.int32)
    first = jnp.sum(jnp.where(eid[..., None] == jnp.arange(N_EXPERTS, dtype=jnp.int32), blk_start, 0), axis=-1)
    return first * MOE_BLOCK + sel[..., 4:6].astype(jnp.int32)


def _hier_moe(x, xc, mod, modc, g2, wg, bg, we, be, w_up, w_down, li, final_g=None):
    b, s, d = x.shape
    w_r = jnp.zeros((d, LANES), F32).at[:, :N_GROUPS].set(wg).at[:, N_GROUPS:N_GROUPS + N_EXPERTS].set(we)
    b_r = jnp.zeros((1, LANES), F32).at[0, :N_GROUPS].set(bg).at[0, N_GROUPS:N_GROUPS + N_EXPERTS].set(be)
    w_hi = _bf(w_r)
    w_lo = _bf(w_r - w_hi.astype(F32))
    h, sel, cnt = _router(x, xc, g2, mod, modc, w_hi, w_lo, b_r, name=f"router{li}")
    n_tok = h.shape[0] * h.shape[1]
    counts = cnt[0, N_GROUPS:N_GROUPS + N_EXPERTS].astype(jnp.int32)
    eids = sel[..., 0:2].reshape(n_tok, 2).astype(jnp.int32)
    src, blk_start, blk_expert, n_used = _dispatch_tables(eids, counts)
    yb = _experts(h.reshape(n_tok, d)[src], blk_expert, n_used, w_up, w_down)
    pos = _row_positions(sel, blk_start)
    x = _moe_res(x, yb[pos[:, :s, 0]], yb[pos[:, :s, 1]], sel, mod, final_g, name=f"moe_res{li}")
    if xc is not None:
        xc = _moe_res(xc, yb[pos[:, s:, 0]], yb[pos[:, s:, 1]], sel, modc, sel_row0=s, name=f"moe_res_ctx{li}")
    return x, xc


def _rope_tables(seq):
    rows = seq // GRID_W
    row = jnp.repeat(jnp.arange(rows), GRID_W).astype(F32)
    col = jnp.tile(jnp.arange(GRID_W), rows).astype(F32)
    nf = HEAD_DIM // 4
    inv = ROPE_BASE ** (-jnp.arange(nf, dtype=F32) / nf)
    ar, ac = row[:, None] * inv, col[:, None] * inv
    ang = jnp.concatenate([ar, ar, ac, ac] * (LANES // HEAD_DIM), axis=-1)
    return jnp.cos(ang), jnp.sin(ang)


def kernel(x, c, ctx, c_ctx, mod_w, mod_b, norm_g, final_g, win_w_qkv, win_sink, win_w_o, diff_w_qkv, diff_lambda, diff_subln_g, diff_w_o, pool_w_group, pool_b_group, pool_scale, rwkv_mu, rwkv_w_rkv, rwkv_w0, rwkv_w_a1, rwkv_w_a2, rwkv_a0, rwkv_a_a1, rwkv_a_a2, rwkv_g1, rwkv_g2, rwkv_k_k, rwkv_k_a, rwkv_r_k, rwkv_ln_w, rwkv_ln_b, rwkv_w_o, moe_wg, moe_bg, moe_we, moe_be, moe_w_up, moe_w_down):
    b, s, d = x.shape
    depth = mod_w.shape[0]
    n_mixers = 4
    cos, sin = _rope_tables(s)
    rows = -(-(b + 1) // 8) * 8
    c_all = jnp.zeros((rows, d), F32).at[:b].set(c).at[b].set(c_ctx)
    mods = _mod_all(c_all, mod_w, mod_b).reshape(depth, rows, 6, d)
    xc = ctx
    for i in range(depth):
        m, occ = i % n_mixers, i // n_mixers
        last = i == depth - 1
        mod = mods[i, :b]
        modc = mods[i, b:b + 1]
        g1 = norm_g[i, 0]
        if m == 0:
            nq = win_sink.shape[1] * HEAD_DIM
            nk = WIN_KV_HEADS * HEAD_DIM
            w = _bf(win_w_qkv[occ])
            qkv = _proj(x, g1, mod, w, cos, sin, n_rope=nq + nk, n_q=nq, scale_i=1, shift_i=0, name="win_qkv")
            qkvc = _proj(xc, g1, modc, w, None, None, n_rope=0, n_q=nq, scale_i=1, shift_i=0, name="win_qkv_ctx")
            o = _win_attn(qkv, qkvc, win_sink[occ], local=True, name="win_attn")
            wo = _bf(win_w_o[occ])
            x = _res_proj(o, wo, x, mod, gate_i=2, name="win_out")
            if not last:
                oc = _win_attn(None, qkvc, win_sink[occ], local=False, name="win_attn_ctx")
                xc = _res_proj(oc, wo, xc, modc, gate_i=2, name="win_out_ctx")
        elif m == 1:
            lambda_init = 0.8 - 0.6 * math.exp(-0.3 * i)
            w = _bf(diff_w_qkv[occ])
            qkv = _proj(x, g1, mod, w, cos, sin, n_rope=2 * d, n_q=d, scale_i=1, shift_i=0, name="diff_qkv")
            qkvc = _proj(xc, g1, modc, w, None, None, n_rope=0, n_q=d, scale_i=1, shift_i=0, name="diff_qkv_ctx")
            o = _diff_attn(qkv, qkvc, diff_lambda[occ], diff_subln_g[occ], local=True,
                           lambda_init=lambda_init, name="diff_attn")
            wo = _bf(diff_w_o[occ])
            x = _res_proj(o, wo, x, mod, gate_i=2, name="diff_out")
            if not last:
                oc = _diff_attn(None, qkvc, diff_lambda[occ], diff_subln_g[occ], local=False,
                                lambda_init=lambda_init, name="diff_attn_ctx")
                xc = _res_proj(oc, wo, xc, modc, gate_i=2, name="diff_out_ctx")
        elif m == 2:
            x = _pool_mix(x, g1, mod, pool_w_group[occ], pool_b_group[occ], pool_scale[occ], name="pool")
            if not last:
                xc = _pool_mix(xc, g1, modc, pool_w_group[occ], pool_b_group[occ], pool_scale[occ], name="pool_ctx")
        else:
            p = dict(mu=rwkv_mu[occ], w_rkv=rwkv_w_rkv[occ], w0=rwkv_w0[occ], w_a1=rwkv_w_a1[occ],
                     w_a2=rwkv_w_a2[occ], a0=rwkv_a0[occ], a_a1=rwkv_a_a1[occ], a_a2=rwkv_a_a2[occ],
                     g1=rwkv_g1[occ], g2=rwkv_g2[occ], k_k=rwkv_k_k[occ], k_a=rwkv_k_a[occ])
            assert last, "the context stream's RWKV output path is only needed for non-final layers"
            r, v, kk, lw, bb, kd, g = _rwkv_features(x, g1, mod, p, with_out=True, name="rwkv_feat")
            vc, kkc, lwc, bc, kdc = _rwkv_features(xc, g1, modc, p, with_out=False, name="rwkv_feat_ctx")
            nh = d // HEAD_DIM
            s0 = jnp.zeros((2, b, nh, HEAD_DIM, HEAD_DIM), F32)
            (s_ctx,) = _rwkv_scan(kkc, vc, None, lwc, bc, kdc, s0, name="rwkv_scan_ctx")
            o, _ = _rwkv_scan(kk, v, r, lw, bb, kd, s_ctx, name="rwkv_scan")
            x = _rwkv_output(o, r, v, kd, g, rwkv_r_k[occ], rwkv_ln_w[occ], rwkv_ln_b[occ], rwkv_w_o[occ],
                             x, mod, name="rwkv_out")
        x, xc = _hier_moe(x, None if last else xc, mod, modc, norm_g[i, 1], moe_wg[i], moe_bg[i], moe_we[i],
                          moe_be[i], moe_w_up[i], moe_w_down[i], i, final_g if last else None)
    return x
```

```python
import functools
import math

import jax
import jax.numpy as jnp
from jax import lax
from jax.experimental import pallas as pl
from jax.experimental.pallas import tpu as pltpu

F32 = jnp.float32
BF16 = jnp.bfloat16

HEAD_DIM = 64
GRID_W = 64
ROPE_BASE = 10000.0
NORM_EPS = 1e-6
NEG_INF = -1e30
WIN_KV_HEADS = 4
WIN_BLOCK = 128
POOL_WINDOWS = (2, 4, 8, 16)
POOL_HALO = 8
GN_EPS = 64e-5
N_GROUPS = 4
EXPERTS_PER_GROUP = 8
N_EXPERTS = N_GROUPS * EXPERTS_PER_GROUP
MOE_BLOCK = 512
SEL_ROWS = 8
DIFF_Q_ROWS = 1024
DIFF_Q_SUB = 256
RWKV_CHUNK = 64
LANES = 128
V7X_VMEM_LIMIT = 48 * 1024 * 1024
HI = lax.Precision.HIGHEST


def _cp(sem, vmem=V7X_VMEM_LIMIT):
    return pltpu.CompilerParams(dimension_semantics=sem, vmem_limit_bytes=vmem)


def _bf(x):
    return x.astype(BF16)


def _dot(a, b, precision=None):
    return jnp.dot(a, b, preferred_element_type=F32, precision=precision)


def _dot_t(a, b, precision=None):
    return lax.dot_general(a, b, (((1,), (1,)), ((), ())), preferred_element_type=F32, precision=precision)


def _dot_l(a, b, precision=None):
    return lax.dot_general(a, b, (((0,), (0,)), ((), ())), preferred_element_type=F32, precision=precision)


def _mm(a, b):
    return _dot(_bf(a), _bf(b))


def _mm_t(a, b):
    return _dot_t(_bf(a), _bf(b))


def _mm_l(a, b):
    return _dot_l(_bf(a), _bf(b))


def _split3(x):
    hi = _bf(x)
    r1 = x - hi.astype(F32)
    mid = _bf(r1)
    return hi, mid, _bf(r1 - mid.astype(F32))


def _cumulate(tri, x):
    hi, mid, lo = _split3(x)
    return _dot(tri, hi) + _dot(tri, mid) + _dot(tri, lo)


def _head_sums(x):
    i = lax.broadcasted_iota(jnp.int32, (LANES, LANES), 0) // HEAD_DIM
    j = lax.broadcasted_iota(jnp.int32, (LANES, LANES), 1) // HEAD_DIM
    ones = _bf(jnp.where(i == j, 1.0, 0.0))
    out = []
    for t in range(x.shape[1] // LANES):
        hi, mid, lo = _split3(x[:, t * LANES:(t + 1) * LANES])
        out.append(_dot(hi, ones) + _dot(mid, ones) + _dot(lo, ones))
    return jnp.concatenate(out, axis=1)


def _norm_mod(x, g, scale, shift):
    ms = jnp.mean(x * x, axis=-1, keepdims=True)
    y = x * lax.rsqrt(ms + NORM_EPS) * g
    return y * (1.0 + scale) + shift


def _sigmoid(x):
    return 1.0 / (1.0 + jnp.exp(-x))


def _row_tile(n, pref):
    t = min(pref, n)
    assert n % t == 0
    return t


def _mod_body(c_ref, w_ref, b_ref, o_ref):
    c = c_ref[...]
    s = c * _sigmoid(c)
    o_ref[0] = _dot(_bf(s), _bf(w_ref[0])) + b_ref[0]


def _mod_all(c_all, mod_w, mod_b):
    depth, d, n = mod_w.shape
    r = c_all.shape[0]
    tn = 1536
    return pl.pallas_call(
        _mod_body,
        grid=(depth, n // tn),
        in_specs=[
            pl.BlockSpec((r, d), lambda i, j: (0, 0)),
            pl.BlockSpec((1, d, tn), lambda i, j: (i, 0, j)),
            pl.BlockSpec((1, 1, tn), lambda i, j: (i, 0, j)),
        ],
        out_specs=pl.BlockSpec((1, r, tn), lambda i, j: (i, 0, j)),
        out_shape=jax.ShapeDtypeStruct((depth, r, n), F32),
        compiler_params=_cp(("parallel", "parallel")),
        name="adaln_mod",
    )(c_all, mod_w, mod_b.reshape(depth, 1, n))


def _mod_spec(mod):
    if mod.shape[0] == 1:
        return pl.BlockSpec((1,) + mod.shape[1:], lambda b, i: (0, 0, 0))
    return pl.BlockSpec((1,) + mod.shape[1:], lambda b, i: (b, 0, 0))


def _rope_tile(y, cos, sin, first_half):
    fwd = pltpu.roll(y, LANES - 16, 1)
    bwd = pltpu.roll(y, 16, 1)
    rot = jnp.where(first_half, -fwd, bwd)
    return y * cos + rot * sin


def _proj_body(*refs, n_rope, n_q, scale_i, shift_i):
    if n_rope:
        x_ref, g_ref, m_ref, w_ref, cos_ref, sin_ref, o_ref = refs
    else:
        x_ref, g_ref, m_ref, w_ref, o_ref = refs
    m = m_ref[0]
    h = _norm_mod(x_ref[0], g_ref[...], m[scale_i:scale_i + 1], m[shift_i:shift_i + 1])
    y = _dot(_bf(h), w_ref[...])
    n = y.shape[1]
    if n_rope:
        cos, sin = cos_ref[...], sin_ref[...]
        first_half = (lax.broadcasted_iota(jnp.int32, cos.shape, 1) & 31) < 16
    for j in range(n // LANES):
        blk = y[:, j * LANES:(j + 1) * LANES]
        if j * LANES < n_q:
            blk = blk * (HEAD_DIM ** -0.5)
        if j * LANES < n_rope:
            blk = _rope_tile(blk, cos, sin, first_half)
        o_ref[0, :, j * LANES:(j + 1) * LANES] = _bf(blk)


def _proj(x, g, mod, w, cos, sin, *, n_rope, n_q, scale_i, shift_i, name):
    b, l, d = x.shape
    n = w.shape[1]
    tm = _row_tile(l, 256)
    in_specs = [
        pl.BlockSpec((1, tm, d), lambda bi, i: (bi, i, 0)),
        pl.BlockSpec((1, d), lambda bi, i: (0, 0)),
        _mod_spec(mod),
        pl.BlockSpec((d, n), lambda bi, i: (0, 0)),
    ]
    args = [x, g.reshape(1, d), mod, w]
    if n_rope:
        in_specs += [pl.BlockSpec((tm, LANES), lambda bi, i: (i, 0))] * 2
        args += [cos, sin]
    return pl.pallas_call(
        functools.partial(_proj_body, n_rope=n_rope, n_q=n_q, scale_i=scale_i, shift_i=shift_i),
        grid=(b, l // tm),
        in_specs=in_specs,
        out_specs=pl.BlockSpec((1, tm, n), lambda bi, i: (bi, i, 0)),
        out_shape=jax.ShapeDtypeStruct((b, l, n), BF16),
        compiler_params=_cp(("parallel", "parallel")),
        name=name,
    )(*args)


def _res_body(a_ref, w_ref, x_ref, m_ref, o_ref, *, gate_i):
    y = _dot(a_ref[0], w_ref[...])
    o_ref[0] = x_ref[0] + m_ref[0][gate_i:gate_i + 1] * y


def _res_proj(a, w, x, mod, *, gate_i, name):
    b, l, d = x.shape
    k = a.shape[-1]
    tm = _row_tile(l, 512)
    in_specs = [
        pl.BlockSpec((1, tm, k), lambda bi, i: (bi, i, 0)),
        pl.BlockSpec((k, d), lambda bi, i: (0, 0)),
        pl.BlockSpec((1, tm, d), lambda bi, i: (bi, i, 0)),
        _mod_spec(mod),
    ]
    args = [a, w, x, mod]
    return pl.pallas_call(
        functools.partial(_res_body, gate_i=gate_i),
        grid=(b, l // tm),
        in_specs=in_specs,
        out_specs=pl.BlockSpec((1, tm, d), lambda bi, i: (bi, i, 0)),
        out_shape=jax.ShapeDtypeStruct((b, l, d), F32),
        input_output_aliases={2: 0},
        compiler_params=_cp(("parallel", "parallel")),
        name=name,
    )(*args)


def _win_body(*refs, seq, local, n_heads):
    if local:
        q_ref, kp_ref, kc_ref, kn_ref, vp_ref, vc_ref, vn_ref, kx_ref, vx_ref, sink_ref, o_ref = refs
    else:
        q_ref, kx_ref, vx_ref, sink_ref, o_ref = refs
    tq = q_ref.shape[1]
    grp = n_heads // WIN_KV_HEADS
    n = pl.program_id(1)
    if local:
        span = 3 * tq
        row = lax.broadcasted_iota(jnp.int32, (grp * tq, span), 0) & (tq - 1)
        col = lax.broadcasted_iota(jnp.int32, (grp * tq, span), 1)
        rel = col - tq - row
        key_pos = n * tq - tq + col
        mask = (jnp.abs(rel) <= tq) & (key_pos >= 0) & (key_pos < seq)
    hks = range(WIN_KV_HEADS)
    ks = [slice(hk * HEAD_DIM, (hk + 1) * HEAD_DIM) for hk in hks]
    qh = [jnp.concatenate(
        [q_ref[0, :, (hk * grp + g) * HEAD_DIM:(hk * grp + g + 1) * HEAD_DIM] for g in range(grp)], axis=0)
        for hk in hks]
    sink = [jnp.concatenate([jnp.full((tq, 1), sink_ref[hk * grp + g], F32) for g in range(grp)], axis=0)
            for hk in hks]
    s_ctx = [_dot_t(qh[hk], kx_ref[0, :, ks[hk]]) for hk in hks]
    m = [jnp.maximum(jnp.max(s_ctx[hk], axis=1, keepdims=True), sink[hk]) for hk in hks]
    if local:
        s_loc = [jnp.where(mask, _dot_t(qh[hk], jnp.concatenate(
            [kp_ref[0, :, ks[hk]], kc_ref[0, :, ks[hk]], kn_ref[0, :, ks[hk]]], axis=0)), NEG_INF) for hk in hks]
        m = [jnp.maximum(m[hk], jnp.max(s_loc[hk], axis=1, keepdims=True)) for hk in hks]
    p_ctx = [jnp.exp(s_ctx[hk] - m[hk]) for hk in hks]
    den = [jnp.sum(p_ctx[hk], axis=1, keepdims=True) + jnp.exp(sink[hk] - m[hk]) for hk in hks]
    o = [_dot(_bf(p_ctx[hk]), vx_ref[0, :, ks[hk]]) for hk in hks]
    if local:
        p_loc = [jnp.exp(s_loc[hk] - m[hk]) for hk in hks]
        den = [den[hk] + jnp.sum(p_loc[hk], axis=1, keepdims=True) for hk in hks]
        o = [o[hk] + _dot(_bf(p_loc[hk]), jnp.concatenate(
            [vp_ref[0, :, ks[hk]], vc_ref[0, :, ks[hk]], vn_ref[0, :, ks[hk]]], axis=0)) for hk in hks]
    for hk in hks:
        oh = o[hk] / den[hk]
        for g in range(grp):
            hq = hk * grp + g
            o_ref[0, :, hq * HEAD_DIM:(hq + 1) * HEAD_DIM] = _bf(oh[g * tq:(g + 1) * tq])


def _win_attn(qkv, qkvc, sink, *, local, name):
    src = qkv if local else qkvc
    b, l, _ = src.shape
    c = qkvc.shape[1]
    n_heads = sink.shape[0]
    d = n_heads * HEAD_DIM
    kvw = WIN_KV_HEADS * HEAD_DIM
    kcol = d // kvw
    tq = WIN_BLOCK
    nb = l // tq
    in_specs = [pl.BlockSpec((1, tq, d), lambda bi, i: (bi, i, 0))]
    args = [src]
    if local:
        for colb in (kcol, kcol + 1):
            in_specs += [
                pl.BlockSpec((1, tq, kvw), lambda bi, i, colb=colb: (bi, jnp.maximum(i - 1, 0), colb)),
                pl.BlockSpec((1, tq, kvw), lambda bi, i, colb=colb: (bi, i, colb)),
                pl.BlockSpec((1, tq, kvw), lambda bi, i, colb=colb: (bi, jnp.minimum(i + 1, nb - 1), colb)),
            ]
            args += [qkv, qkv, qkv]
    in_specs += [
        pl.BlockSpec((1, c, kvw), lambda bi, i: (bi, 0, kcol)),
        pl.BlockSpec((1, c, kvw), lambda bi, i: (bi, 0, kcol + 1)),
        pl.BlockSpec(memory_space=pltpu.SMEM),
    ]
    args += [qkvc, qkvc, sink]
    return pl.pallas_call(
        functools.partial(_win_body, seq=l, local=local, n_heads=n_heads),
        grid=(b, nb),
        in_specs=in_specs,
        out_specs=pl.BlockSpec((1, tq, d), lambda bi, i: (bi, i, 0)),
        out_shape=jax.ShapeDtypeStruct((b, l, d), BF16),
        compiler_params=_cp(("parallel", "parallel")),
        name=name,
    )(*args)


def _diff_body(*refs, n_lat, tk, sub, lambda_init):
    if n_lat:
        lam_ref, g_ref, q_ref, kl_ref, vl_ref, kx_ref, vx_ref, o_ref, s_ref = refs
    else:
        lam_ref, g_ref, q_ref, kx_ref, vx_ref, o_ref, s_ref = refs
    hw = 2 * HEAD_DIM
    n_sub = q_ref.shape[1] // sub
    lane = lax.broadcasted_iota(jnp.int32, (sub, hw), 1)
    chunks = [(kl_ref, vl_ref, i * tk, tk, i * tk) for i in range(n_lat)] if n_lat else []
    chunks.append((kx_ref, vx_ref, 0, kx_ref.shape[1], n_lat * tk))
    lam = lam_ref[...]
    lam_full = (jnp.exp(jnp.sum(lam[0:1] * lam[1:2], axis=1, keepdims=True))
                - jnp.exp(jnp.sum(lam[2:3] * lam[3:4], axis=1, keepdims=True)) + lambda_init)

    def stacked_q(j):
        q = q_ref[0, j * sub:(j + 1) * sub, :]
        zero = jnp.zeros_like(q)
        return jnp.concatenate([jnp.where(lane < HEAD_DIM, q, zero), jnp.where(lane >= HEAD_DIM, q, zero)], axis=0)

    def score_chunk(j, qq, ch, mx):
        k_ref, _, row, n, col = ch
        s = _dot_t(qq, k_ref[0, row:row + n, :])
        s_ref[j % 2, :, col:col + n] = s
        for t in range(n // LANES):
            mx = jnp.maximum(mx, s[:, t * LANES:(t + 1) * LANES])
        return mx

    def value_chunk(j, m, ch, acc):
        _, v_ref, row, n, col = ch
        p = jnp.concatenate(
            [jnp.exp(s_ref[j % 2, :, col + t * LANES:col + (t + 1) * LANES] - m) for t in range(n // LANES)], axis=1)
        v = v_ref[0, row:row + n, :]
        return acc + _dot(_bf(p), jnp.concatenate([v, jnp.ones_like(v)], axis=1))

    def finish(j, acc):
        a = acc[:, :hw] / acc[:, hw:]
        o = a[:sub] - lam_full * a[sub:]
        ms = jnp.mean(o * o, axis=-1, keepdims=True)
        o = o * lax.rsqrt(ms + NORM_EPS) * g_ref[...] * (1.0 - lambda_init)
        o_ref[0, j * sub:(j + 1) * sub, :] = _bf(o)

    m_prev = None
    for j in range(n_sub + 1):
        if j < n_sub:
            qq = stacked_q(j)
            mx = jnp.full((2 * sub, LANES), NEG_INF, F32)
        if j > 0:
            acc = jnp.zeros((2 * sub, 2 * hw), F32)
        for ch in chunks:
            if j > 0:
                acc = value_chunk(j - 1, m_prev, ch, acc)
            if j < n_sub:
                mx = score_chunk(j, qq, ch, mx)
        if j > 0:
            finish(j - 1, acc)
        if j < n_sub:
            m_prev = jnp.broadcast_to(jnp.max(mx, axis=1, keepdims=True), (2 * sub, LANES))


def _diff_attn(qkv, qkvc, lam, subln_g, *, local, lambda_init, name):
    src = qkv if local else qkvc
    b, l, n3 = src.shape
    d = n3 // 3
    c = qkvc.shape[1]
    hw = 2 * HEAD_DIM
    nh = d // hw
    tq = _row_tile(l, DIFF_Q_ROWS)
    sub = min(DIFF_Q_SUB, tq)
    tk = 512
    in_specs = [
        pl.BlockSpec((4, HEAD_DIM), lambda bi, h, i: (0, 0)),
        pl.BlockSpec((1, hw), lambda bi, h, i: (0, 0)),
        pl.BlockSpec((1, tq, hw), lambda bi, h, i: (bi, i, h)),
    ]
    args = [lam, subln_g.reshape(1, hw), src]
    n_lat = 0
    if local:
        s = qkv.shape[1]
        n_lat = s // tk
        in_specs += [
            pl.BlockSpec((1, s, hw), lambda bi, h, i: (bi, 0, nh + h)),
            pl.BlockSpec((1, s, hw), lambda bi, h, i: (bi, 0, 2 * nh + h)),
        ]
        args += [qkv, qkv]
    in_specs += [
        pl.BlockSpec((1, c, hw), lambda bi, h, i: (bi, 0, nh + h)),
        pl.BlockSpec((1, c, hw), lambda bi, h, i: (bi, 0, 2 * nh + h)),
    ]
    args += [qkvc, qkvc]
    return pl.pallas_call(
        functools.partial(_diff_body, n_lat=n_lat, tk=tk, sub=sub, lambda_init=lambda_init),
        grid=(b, nh, l // tq),
        in_specs=in_specs,
        out_specs=pl.BlockSpec((1, tq, hw), lambda bi, h, i: (bi, i, h)),
        out_shape=jax.ShapeDtypeStruct((b, l, d), BF16),
        scratch_shapes=[pltpu.VMEM((2, 2 * sub, n_lat * tk + c), F32)],
        compiler_params=_cp(("parallel", "parallel", "parallel")),
        name=name,
    )(*args)


def _halo_specs(l, tm, d):
    nh = l // POOL_HALO
    per = tm // POOL_HALO
    return [
        pl.BlockSpec((1, POOL_HALO, d), lambda bi, i: (bi, jnp.maximum(i * per - 1, 0), 0)),
        pl.BlockSpec((1, tm, d), lambda bi, i: (bi, i, 0)),
        pl.BlockSpec((1, POOL_HALO, d), lambda bi, i: (bi, jnp.minimum((i + 1) * per, nh - 1), 0)),
    ]


def _fill_normed(h_ref, xp_ref, x_ref, xn_ref, g, scale, shift):
    tm = x_ref.shape[1]
    h_ref[0:POOL_HALO, :] = _norm_mod(xp_ref[0], g, scale, shift)
    h_ref[POOL_HALO:POOL_HALO + tm, :] = _norm_mod(x_ref[0], g, scale, shift)
    h_ref[POOL_HALO + tm:2 * POOL_HALO + tm, :] = _norm_mod(xn_ref[0], g, scale, shift)


def _pool_body(xp_ref, x_ref, xn_ref, g_ref, m_ref, w_ref, b_ref, s_ref, o_ref, h_ref, *, seq):
    tm = x_ref.shape[1]
    d = x_ref.shape[2]
    gd = d // len(POOL_WINDOWS)
    m = m_ref[0]
    _fill_normed(h_ref, xp_ref, x_ref, xn_ref, g_ref[...], m[1:2], m[0:1])
    pos = pl.program_id(1) * tm + lax.broadcasted_iota(jnp.int32, (tm, 1), 0)
    x = x_ref[0]
    for gi, w in enumerate(POOL_WINDOWS):
        cols = slice(gi * gd, (gi + 1) * gd)
        acc = jnp.zeros((tm, gd), F32)
        for off in range(-(w // 2), w - w // 2):
            valid = (pos + off >= 0) & (pos + off < seq)
            acc = acc + jnp.where(valid, h_ref[POOL_HALO + off:POOL_HALO + off + tm, cols], 0.0)
        lo = jnp.maximum(pos - w // 2, 0)
        hi = jnp.minimum(pos + w - w // 2, seq)
        y = acc / (hi - lo).astype(F32) - h_ref[POOL_HALO:POOL_HALO + tm, cols]
        y = (_dot(_bf(y), w_ref[gi]) + b_ref[:, cols]) * s_ref[:, cols]
        o_ref[0, :, cols] = x[:, cols] + m[2:3, cols] * y


def _pool_mix(x, g, mod, w_group, b_group, layer_scale, *, name):
    b, l, d = x.shape
    tm = _row_tile(l, 256)
    gd = d // len(POOL_WINDOWS)
    return pl.pallas_call(
        functools.partial(_pool_body, seq=l),
        grid=(b, l // tm),
        in_specs=_halo_specs(l, tm, d) + [
            pl.BlockSpec((1, d), lambda bi, i: (0, 0)),
            _mod_spec(mod),
            pl.BlockSpec((len(POOL_WINDOWS), gd, gd), lambda bi, i: (0, 0, 0)),
            pl.BlockSpec((1, d), lambda bi, i: (0, 0)),
            pl.BlockSpec((1, d), lambda bi, i: (0, 0)),
        ],
        out_specs=pl.BlockSpec((1, tm, d), lambda bi, i: (bi, i, 0)),
        out_shape=jax.ShapeDtypeStruct((b, l, d), F32),
        scratch_shapes=[pltpu.VMEM((tm + 2 * POOL_HALO, d), F32)],
        compiler_params=_cp(("parallel", "parallel")),
        name=name,
    )(x, x, x, g.reshape(1, d), mod, _bf(w_group), b_group.reshape(1, d), layer_scale.reshape(1, d))


def _softplus(z):
    return jnp.maximum(z, 0.0) + jnp.log1p(jnp.exp(-jnp.abs(z)))


def _rwkv_feat_body(*refs, seq, with_out):
    (xp_ref, x_ref, xn_ref, g_ref, m_ref, mu_ref, wrkv_ref, w0_ref, wa1_ref, wa2_ref, a0_ref, aa1_ref,
     aa2_ref, g1_ref, g2_ref, kk_ref, ka_ref) = refs[:17]
    if with_out:
        r_out, v_out, kk_out, lw_out, b_out, kd_out, g_out, h_ref = refs[17:]
    else:
        v_out, kk_out, lw_out, b_out, kd_out, h_ref = refs[17:]
    tm = x_ref.shape[1]
    m = m_ref[0]
    _fill_normed(h_ref, xp_ref, x_ref, xn_ref, g_ref[...], m[1:2], m[0:1])
    pos = pl.program_id(1) * tm + lax.broadcasted_iota(jnp.int32, (tm, 1), 0)
    t = h_ref[POOL_HALO:POOL_HALO + tm, :]
    dp = jnp.where(pos >= 1, h_ref[POOL_HALO - 1:POOL_HALO - 1 + tm, :], 0.0) - t
    dn = jnp.where(pos < seq - 1, h_ref[POOL_HALO + 1:POOL_HALO + 1 + tm, :], 0.0) - t

    def mix(i):
        return _bf(t + dp * mu_ref[0, i:i + 1, :] + dn * mu_ref[1, i:i + 1, :])

    k = _dot(mix(2), wrkv_ref[1])
    v_out[0] = _dot(mix(3), wrkv_ref[2])
    kk = k * kk_ref[...]
    kk = kk / jnp.maximum(jnp.sqrt(_head_sums(kk * kk)), 1e-12)
    kk_out[0] = kk
    xw, xa = mix(1), mix(4)
    for di in range(2):
        z = w0_ref[di:di + 1, :] + _dot(_bf(jnp.tanh(_dot(xw, wa1_ref[di]))), wa2_ref[di])
        w = -_softplus(-z) - 0.5
        lw_out[di, 0] = -jnp.exp(w)
        a = _sigmoid(a0_ref[di:di + 1, :] + _dot(_bf(_dot(xa, aa1_ref[di])), aa2_ref[di]))
        kd_out[di, 0] = k * (1.0 + (a - 1.0) * ka_ref[...])
        b_out[di, 0] = kk * a
    if with_out:
        r_out[0] = _dot(mix(0), wrkv_ref[0])
        g_out[0] = _dot(_bf(_sigmoid(_dot(mix(5), g1_ref[...]))), g2_ref[...])


def _rwkv_features(x, g, mod, p, *, with_out, name):
    b, l, d = x.shape
    tm = _row_tile(l, 128)
    const = lambda a: pl.BlockSpec(a.shape, lambda bi, i, nd=a.ndim: (0,) * nd)
    weights = [p["mu"], _bf(p["w_rkv"]), p["w0"], _bf(p["w_a1"]), _bf(p["w_a2"]), p["a0"], _bf(p["a_a1"]),
               _bf(p["a_a2"]), _bf(p["g1"]), _bf(p["g2"]), p["k_k"].reshape(1, d), p["k_a"].reshape(1, d)]
    one = pl.BlockSpec((1, tm, d), lambda bi, i: (bi, i, 0))
    two = pl.BlockSpec((2, 1, tm, d), lambda bi, i: (0, bi, i, 0))
    s1 = jax.ShapeDtypeStruct((b, l, d), F32)
    s2 = jax.ShapeDtypeStruct((2, b, l, d), F32)
    out_specs = [one, one, two, two, two]
    out_shape = [s1, s1, s2, s2, s2]
    if with_out:
        out_specs = [one] + out_specs + [one]
        out_shape = [s1] + out_shape + [s1]
    return pl.pallas_call(
        functools.partial(_rwkv_feat_body, seq=l, with_out=with_out),
        grid=(b, l // tm),
        in_specs=_halo_specs(l, tm, d) + [pl.BlockSpec((1, d), lambda bi, i: (0, 0)), _mod_spec(mod)]
        + [const(a) for a in weights],
        out_specs=out_specs,
        out_shape=out_shape,
        scratch_shapes=[pltpu.VMEM((tm + 2 * POOL_HALO, d), F32)],
        compiler_params=_cp(("parallel", "parallel")),
        name=name,
    )(x, x, x, g.reshape(1, d), mod, *weights)


def _scan_body(*refs, emit, n_chunks):
    if emit:
        kk_ref, v_ref, r_ref, lw_ref, b_ref, kd_ref, s0_ref, o_ref, sfin_ref, s_ref = refs
    else:
        kk_ref, v_ref, lw_ref, b_ref, kd_ref, s0_ref, sfin_ref, s_ref = refs
    hb = s_ref.shape[0]
    L = kk_ref.shape[1]
    rev = pl.program_id(0) == 1
    c = pl.program_id(2)

    @pl.when(c == 0)
    def _():
        s_ref[...] = s0_ref[0, 0]

    row = lax.broadcasted_iota(jnp.int32, (L, L), 0)
    col = lax.broadcasted_iota(jnp.int32, (L, L), 1)
    flip = rev.astype(jnp.int32)
    p_row = row + flip * (L - 1 - 2 * row)
    p_col = col + flip * (L - 1 - 2 * col)
    incl = p_col <= p_row
    strict = p_col < p_row
    tri = _bf(incl.astype(F32))
    eye = row == col
    assert L == HEAD_DIM
    levels = []
    m = 1
    while m < L:
        same = (p_row // (2 * m)) == (p_col // (2 * m))
        levels.append(same & ((p_row & (2 * m - 1)) >= m) & ((p_col & (2 * m - 1)) < m))
        m *= 2

    hs = range(hb)
    cut = lambda a: [a[:, h * HEAD_DIM:(h + 1) * HEAD_DIM] for h in hs]
    lw_all = lw_ref[0, 0]
    b_all = b_ref[0, 0]
    kd_all = kd_ref[0, 0]
    cl_all = _cumulate(tri, lw_all)
    tot_all = jnp.sum(lw_all, axis=0, keepdims=True)
    w_inv = jnp.exp(-cl_all)
    w_last = jnp.exp(tot_all - cl_all)
    kh = cut(kk_ref[0] * jnp.exp(cl_all - lw_all))
    bt, kt = cut(b_all * w_inv), cut(kd_all * w_inv)
    bt_l, kt_l = cut(b_all * w_last), cut(kd_all * w_last)
    w_tot = cut(jnp.exp(tot_all))
    v = cut(v_ref[0])
    s0 = [s_ref[h] for h in hs]
    rhs = [jnp.concatenate([bt[h], kt[h]], axis=0) for h in hs]
    if emit:
        rh = cut(r_ref[0] * jnp.exp(cl_all))
        lhs = [jnp.concatenate([kh[h], rh[h]], axis=0) for h in hs]
    else:
        lhs = kh
    big = [_mm_t(lhs[h], rhs[h]) for h in hs]
    m_b = [big[h][:L, :L] for h in hs]
    mkv = [_mm(jnp.where(strict, big[h][:L, L:], 0.0), v[h]) for h in hs]
    t = [jnp.where(eye, 1.0, 0.0) - jnp.where(levels[0], m_b[h], 0.0) for h in hs]
    for lm in levels[1:]:
        tl = [_mm(t[h], jnp.where(lm, m_b[h], 0.0)) for h in hs]
        t = [t[h] - _mm(tl[h], t[h]) for h in hs]
    y = [_mm(t[h], jnp.concatenate([kh[h], mkv[h]], axis=1)) for h in hs]
    gh = [_mm_l(y[h], bt_l[h]) for h in hs]
    vk = [_mm_l(v[h], kt_l[h]) for h in hs]
    g_mat = [jnp.where(eye, w_tot[h], 0.0) - gh[h][:HEAD_DIM] for h in hs]
    if emit:
        ab_y = [_mm(jnp.where(incl, big[h][L:, :L], 0.0), y[h]) for h in hs]
        akv = [_mm(jnp.where(incl, big[h][L:, L:], 0.0), v[h]) for h in hs]
        rs = [_mm_t(rh[h] - ab_y[h][:, :HEAD_DIM], s0[h]) for h in hs]
        o_ref[0, 0] = jnp.concatenate([rs[h] + akv[h] - ab_y[h][:, HEAD_DIM:] for h in hs], axis=1)
    sg = [_mm(s0[h], g_mat[h]) for h in hs]
    for h in hs:
        s_ref[h] = sg[h] + vk[h] - gh[h][HEAD_DIM:]

    @pl.when(c == n_chunks - 1)
    def _():
        sfin_ref[0, 0] = s_ref[...]


def _rwkv_scan(kk, v, r, lw, b, kd, s0, *, name):
    bsz, l, d = kk.shape
    dh = HEAD_DIM
    nh = d // dh
    L = RWKV_CHUNK
    nch = l // L
    emit = r is not None

    def chunk(di, c):
        return c + di * (nch - 1 - 2 * c)

    one = pl.BlockSpec((1, L, d), lambda di, bi, c: (bi, chunk(di, c), 0))
    two = pl.BlockSpec((1, 1, L, d), lambda di, bi, c: (di, bi, chunk(di, c), 0))
    st = pl.BlockSpec((1, 1, nh, dh, dh), lambda di, bi, c: (di, bi, 0, 0, 0))
    in_specs = [one, one] + ([one] if emit else []) + [two, two, two, st]
    args = [kk, v] + ([r] if emit else []) + [lw, b, kd, s0]
    s_shape = jax.ShapeDtypeStruct((2, bsz, nh, dh, dh), F32)
    if emit:
        out_specs = [two, st]
        out_shape = [jax.ShapeDtypeStruct((2, bsz, l, d), F32), s_shape]
    else:
        out_specs = [st]
        out_shape = [s_shape]
    return pl.pallas_call(
        functools.partial(_scan_body, emit=emit, n_chunks=nch),
        grid=(2, bsz, nch),
        in_specs=in_specs,
        out_specs=out_specs,
        out_shape=out_shape,
        scratch_shapes=[pltpu.VMEM((nh, dh, dh), F32)],
        compiler_params=_cp(("parallel", "parallel", "arbitrary")),
        name=name,
    )(*args)


def _rwkv_out_body(o_ref, r_ref, v_ref, kd_ref, g_ref, rk_ref, lnw_ref, lnb_ref, w_ref, x_ref, m_ref, out_ref):
    o = o_ref[0, 0] + o_ref[1, 0]
    mean = _head_sums(o) * (1.0 / HEAD_DIM)
    cen = o - mean
    var = _head_sums(cen * cen) * (1.0 / HEAD_DIM)
    on = cen * lax.rsqrt(var + GN_EPS) * lnw_ref[...] + lnb_ref[...]
    bonus = _head_sums(r_ref[0] * (kd_ref[0, 0] + kd_ref[1, 0]) * rk_ref[...]) * v_ref[0]
    a = (on + bonus) * g_ref[0]
    out_ref[0] = x_ref[0] + m_ref[0][2:3] * _dot(_bf(a), w_ref[...])


def _rwkv_output(o, r, v, kd, g, r_k, ln_w, ln_b, w_o, x, mod, *, name):
    b, l, d = x.shape
    tm = _row_tile(l, 256)
    row = pl.BlockSpec((1, tm, d), lambda bi, i: (bi, i, 0))
    row2 = pl.BlockSpec((2, 1, tm, d), lambda bi, i: (0, bi, i, 0))
    small = pl.BlockSpec((1, d), lambda bi, i: (0, 0))
    return pl.pallas_call(
        _rwkv_out_body,
        grid=(b, l // tm),
        in_specs=[row2, row, row, row2, row, small, small, small,
                  pl.BlockSpec((d, d), lambda bi, i: (0, 0)), row, _mod_spec(mod)],
        out_specs=row,
        out_shape=jax.ShapeDtypeStruct((b, l, d), F32),
        input_output_aliases={9: 0},
        compiler_params=_cp(("parallel", "parallel")),
        name=name,
    )(o, r, v, kd, g, r_k.reshape(1, d), ln_w.reshape(1, d), ln_b.reshape(1, d), _bf(w_o), x, mod)


def _router_body(*refs, lat_tiles):
    if lat_tiles is None:
        x_ref, g_ref, m_ref, whi_ref, wlo_ref, b_ref, h_ref, sel_ref, selt_ref, cnt_ref, run_ref = refs
        xt, m = x_ref[0], m_ref[0]
    else:
        (x_ref, xc_ref, g_ref, m_ref, mc_ref, whi_ref, wlo_ref, b_ref, h_ref, sel_ref, selt_ref, cnt_ref,
         run_ref) = refs
        is_ctx = pl.program_id(1) >= lat_tiles
        xt = jnp.where(is_ctx, xc_ref[0], x_ref[0])
        m = jnp.where(is_ctx, mc_ref[0], m_ref[0])

    @pl.when((pl.program_id(0) == 0) & (pl.program_id(1) == 0))
    def _():
        run_ref[...] = jnp.zeros_like(run_ref)

    h = _norm_mod(xt, g_ref[...], m[4:5], m[3:4])
    hi = _bf(h)
    h_ref[0] = h
    lo_part = _bf(h - hi.astype(F32))
    lg = _dot(hi, whi_ref[...]) + _dot(hi, wlo_ref[...]) + _dot(lo_part, whi_ref[...]) + b_ref[...]
    lane = lax.broadcasted_iota(jnp.int32, lg.shape, 1)
    big = jnp.int32(1 << 20)
    g_logit = jnp.where(lane < N_GROUPS, lg, -jnp.inf)
    g_max = jnp.max(g_logit, axis=1, keepdims=True)
    gsel = jnp.min(jnp.where(g_logit == g_max, lane, big), axis=1, keepdims=True)
    p_grp = 1.0 / jnp.sum(jnp.exp(g_logit - g_max), axis=1, keepdims=True)
    lo = N_GROUPS + gsel * EXPERTS_PER_GROUP
    e_logit = jnp.where((lane >= lo) & (lane < lo + EXPERTS_PER_GROUP), lg, -jnp.inf)
    e_max = jnp.max(e_logit, axis=1, keepdims=True)
    i1 = jnp.min(jnp.where(e_logit == e_max, lane, big), axis=1, keepdims=True)
    rest = jnp.where(lane == i1, -jnp.inf, e_logit)
    e2 = jnp.max(rest, axis=1, keepdims=True)
    i2 = jnp.min(jnp.where(rest == e2, lane, big), axis=1, keepdims=True)
    q2 = jnp.exp(e2 - e_max)
    w1 = p_grp / (1.0 + q2)
    w2 = p_grp * q2 / (1.0 + q2)
    pick1, pick2 = lane == i1, lane == i2
    both = jnp.where(pick1 | pick2, 1.0, 0.0)
    tm = lg.shape[0]
    earlier = lax.broadcasted_iota(jnp.int32, (tm, tm), 1) < lax.broadcasted_iota(jnp.int32, (tm, tm), 0)
    ahead = _dot(_bf(earlier.astype(F32)), _bf(both)) + run_ref[...]
    r1 = jnp.sum(jnp.where(pick1, ahead, 0.0), axis=1, keepdims=True)
    r2 = jnp.sum(jnp.where(pick2, ahead, 0.0), axis=1, keepdims=True)
    run_ref[...] = run_ref[...] + jnp.sum(both, axis=0, keepdims=True)
    cnt_ref[...] = run_ref[...]
    cols = ((i1 - N_GROUPS).astype(F32), (i2 - N_GROUPS).astype(F32), w1, w2, r1, r2)
    sel = jnp.zeros_like(lg)
    for j, val in enumerate(cols):
        sel = jnp.where(lane == j, val, sel)
    sel_ref[0] = sel
    selt_ref[0] = sel.T[:SEL_ROWS, :]


def _router(x, xc, g, mod, modc, w_hi, w_lo, b_r, *, name):
    b, s, d = x.shape
    c = 0 if xc is None else xc.shape[1]
    tm = _row_tile(s, 256)
    assert c % tm == 0
    lat_tiles = s // tm
    l = s + c
    wide = pl.BlockSpec((1, LANES), lambda bi, i: (0, 0))
    weights = [pl.BlockSpec((d, LANES), lambda bi, i: (0, 0))] * 2 + [wide]
    gspec = pl.BlockSpec((1, d), lambda bi, i: (0, 0))
    if xc is None:
        in_specs = [pl.BlockSpec((1, tm, d), lambda bi, i: (bi, i, 0)), gspec, _mod_spec(mod)] + weights
        args = [x, g.reshape(1, d), mod, w_hi, w_lo, b_r]
    else:
        in_specs = [
            pl.BlockSpec((1, tm, d), lambda bi, i: (bi, jnp.minimum(i, lat_tiles - 1), 0)),
            pl.BlockSpec((1, tm, d), lambda bi, i: (bi, jnp.maximum(i - lat_tiles, 0), 0)),
            gspec, _mod_spec(mod), _mod_spec(modc)] + weights
        args = [x, xc, g.reshape(1, d), mod, modc, w_hi, w_lo, b_r]
    return pl.pallas_call(
        functools.partial(_router_body, lat_tiles=None if xc is None else lat_tiles),
        grid=(b, l // tm),
        in_specs=in_specs,
        out_specs=[pl.BlockSpec((1, tm, d), lambda bi, i: (bi, i, 0)),
                   pl.BlockSpec((1, tm, LANES), lambda bi, i: (bi, i, 0)),
                   pl.BlockSpec((1, SEL_ROWS, tm), lambda bi, i: (bi, 0, i)),
                   wide],
        out_shape=[jax.ShapeDtypeStruct((b, l, d), F32), jax.ShapeDtypeStruct((b, l, LANES), F32),
                   jax.ShapeDtypeStruct((b, SEL_ROWS, l), F32), jax.ShapeDtypeStruct((1, LANES), F32)],
        scratch_shapes=[pltpu.VMEM((1, LANES), F32)],
        compiler_params=_cp(("arbitrary", "arbitrary")),
        name=name,
    )(*args)


def _expert_body(be_ref, nu_ref, x_ref, wu_ref, wd_ref, o_ref, wub_ref, wdb_ref):
    i = pl.program_id(0)
    ff = wd_ref.shape[1]

    @pl.when((i == 0) | (be_ref[i] != be_ref[jnp.maximum(i - 1, 0)]))
    def _():
        wub_ref[...] = _bf(wu_ref[0])
        wdb_ref[...] = _bf(wd_ref[0])

    @pl.when(i < nu_ref[0])
    def _():
        u = _dot(_bf(x_ref[...]), wub_ref[...])
        gate = u[:, :ff]
        act = gate * _sigmoid(gate) * u[:, ff:]
        o_ref[...] = _dot(_bf(act), wdb_ref[...])

    @pl.when(i >= nu_ref[0])
    def _():
        o_ref[...] = jnp.zeros_like(o_ref)


def _experts(xb, blk_expert, n_used, w_up, w_down):
    rows, d = xb.shape
    nb = rows // MOE_BLOCK
    ff2 = w_up.shape[2]
    ff = w_down.shape[1]
    return pl.pallas_call(
        _expert_body,
        grid_spec=pltpu.PrefetchScalarGridSpec(
            num_scalar_prefetch=2,
            grid=(nb,),
            in_specs=[
                pl.BlockSpec((MOE_BLOCK, d), lambda i, be, nu: (i, 0)),
                pl.BlockSpec((1, d, ff2), lambda i, be, nu: (be[i], 0, 0)),
                pl.BlockSpec((1, ff, d), lambda i, be, nu: (be[i], 0, 0)),
            ],
            out_specs=pl.BlockSpec((MOE_BLOCK, d), lambda i, be, nu: (i, 0)),
            scratch_shapes=[pltpu.VMEM((d, ff2), BF16), pltpu.VMEM((ff, d), BF16)],
        ),
        out_shape=jax.ShapeDtypeStruct((rows, d), F32),
        compiler_params=_cp(("arbitrary",)),
        name="moe_experts",
    )(blk_expert, n_used, xb, w_up, w_down)


def _moe_res_body(*refs, final):
    if final:
        x_ref, y0_ref, y1_ref, sel_ref, m_ref, fg_ref, o_ref = refs
    else:
        x_ref, y0_ref, y1_ref, sel_ref, m_ref, o_ref = refs
    sel = sel_ref[0]
    y = y0_ref[0] * sel[:, 2:3] + y1_ref[0] * sel[:, 3:4]
    x = x_ref[0] + m_ref[0][5:6] * y
    if final:
        ms = jnp.mean(x * x, axis=-1, keepdims=True)
        x = x * lax.rsqrt(ms + NORM_EPS) * fg_ref[...]
    o_ref[0] = x


def _moe_res(x, y0, y1, sel, mod, final_g=None, *, sel_row0=0, name):
    b, l, d = x.shape
    tm = _row_tile(l, 512)
    assert sel_row0 % tm == 0
    off = sel_row0 // tm
    row = pl.BlockSpec((1, tm, d), lambda bi, i: (bi, i, 0))
    in_specs = [row, row, row, pl.BlockSpec((1, tm, LANES), lambda bi, i: (bi, off + i, 0)), _mod_spec(mod)]
    args = [x, y0, y1, sel, mod]
    if final_g is not None:
        in_specs.append(pl.BlockSpec((1, d), lambda bi, i: (0, 0)))
        args.append(final_g.reshape(1, d))
    return pl.pallas_call(
        functools.partial(_moe_res_body, final=final_g is not None),
        grid=(b, l // tm),
        in_specs=in_specs,
        out_specs=row,
        out_shape=jax.ShapeDtypeStruct((b, l, d), F32),
        input_output_aliases={0: 0},
        compiler_params=_cp(("parallel", "parallel")),
        name=name,
    )(*args)


def _dispatch_tables(eids, counts):
    l = eids.shape[2]
    a = eids.size
    bi = lax.broadcasted_iota(jnp.int32, eids.shape, 0)
    ji = lax.broadcasted_iota(jnp.int32, eids.shape, 1)
    li = lax.broadcasted_iota(jnp.int32, eids.shape, 2)
    tok = bi * l + li
    by_expert = jnp.argsort((eids * a + tok * 2 + ji).reshape(a))
    tok_sorted = tok.reshape(a)[by_expert]
    start = jnp.cumsum(counts) - counts
    nblk = (counts + MOE_BLOCK - 1) // MOE_BLOCK
    blk_end = jnp.cumsum(nblk)
    blk_start = blk_end - nblk
    nb = -(-a // MOE_BLOCK) + N_EXPERTS
    blk = jnp.arange(nb, dtype=jnp.int32)
    blk_expert = jnp.minimum(jnp.sum((blk[:, None] >= blk_end[None, :]).astype(jnp.int32), axis=1), N_EXPERTS - 1)
    local = jnp.arange(MOE_BLOCK, dtype=jnp.int32)[None, :] + ((blk - blk_start[blk_expert]) * MOE_BLOCK)[:, None]
    sorted_idx = jnp.clip(start[blk_expert][:, None] + local, 0, a - 1)
    spread = (jnp.arange(nb * MOE_BLOCK, dtype=jnp.int32) % (a // 2)).reshape(nb, MOE_BLOCK)
    src = jnp.where(local < counts[blk_expert][:, None], tok_sorted[sorted_idx], spread).reshape(nb * MOE_BLOCK)
    return src, blk_start, blk_expert, blk_end[-1:].astype(jnp.int32)


def _row_positions(eids, ranks, blk_start):
    first = jnp.sum(jnp.where(eids[..., None] == jnp.arange(N_EXPERTS, dtype=jnp.int32), blk_start, 0), axis=-1)
    return first * MOE_BLOCK + ranks


def _hier_moe(x, xc, mod, modc, g2, wg, bg, we, be, w_up, w_down, li, final_g=None):
    b, s, d = x.shape
    w_r = jnp.zeros((d, LANES), F32).at[:, :N_GROUPS].set(wg).at[:, N_GROUPS:N_GROUPS + N_EXPERTS].set(we)
    b_r = jnp.zeros((1, LANES), F32).at[0, :N_GROUPS].set(bg).at[0, N_GROUPS:N_GROUPS + N_EXPERTS].set(be)
    w_hi = _bf(w_r)
    w_lo = _bf(w_r - w_hi.astype(F32))
    h, sel, sel_t, cnt = _router(x, xc, g2, mod, modc, w_hi, w_lo, b_r, name=f"router{li}")
    n_tok = h.shape[0] * h.shape[1]
    counts = cnt[0, N_GROUPS:N_GROUPS + N_EXPERTS].astype(jnp.int32)
    eids = sel_t[:, 0:2, :].astype(jnp.int32)
    src, blk_start, blk_expert, n_used = _dispatch_tables(eids, counts)
    yb = _experts(h.reshape(n_tok, d)[src], blk_expert, n_used, w_up, w_down)
    pos = _row_positions(eids, sel_t[:, 4:6, :].astype(jnp.int32), blk_start)
    x = _moe_res(x, yb[pos[:, 0, :s]], yb[pos[:, 1, :s]], sel, mod, final_g, name=f"moe_res{li}")
    if xc is not None:
        xc = _moe_res(xc, yb[pos[:, 0, s:]], yb[pos[:, 1, s:]], sel, modc, sel_row0=s, name=f"moe_res_ctx{li}")
    return x, xc


def _rope_tables(seq):
    rows = seq // GRID_W
    row = jnp.repeat(jnp.arange(rows), GRID_W).astype(F32)
    col = jnp.tile(jnp.arange(GRID_W), rows).astype(F32)
    nf = HEAD_DIM // 4
    inv = ROPE_BASE ** (-jnp.arange(nf, dtype=F32) / nf)
    ar, ac = row[:, None] * inv, col[:, None] * inv
    ang = jnp.concatenate([ar, ar, ac, ac] * (LANES // HEAD_DIM), axis=-1)
    return jnp.cos(ang), jnp.sin(ang)


def kernel(x, c, ctx, c_ctx, mod_w, mod_b, norm_g, final_g, win_w_qkv, win_sink, win_w_o, diff_w_qkv, diff_lambda, diff_subln_g, diff_w_o, pool_w_group, pool_b_group, pool_scale, rwkv_mu, rwkv_w_rkv, rwkv_w0, rwkv_w_a1, rwkv_w_a2, rwkv_a0, rwkv_a_a1, rwkv_a_a2, rwkv_g1, rwkv_g2, rwkv_k_k, rwkv_k_a, rwkv_r_k, rwkv_ln_w, rwkv_ln_b, rwkv_w_o, moe_wg, moe_bg, moe_we, moe_be, moe_w_up, moe_w_down):
    b, s, d = x.shape
    depth = mod_w.shape[0]
    n_mixers = 4
    cos, sin = _rope_tables(s)
    rows = -(-(b + 1) // 8) * 8
    c_all = jnp.zeros((rows, d), F32).at[:b].set(c).at[b].set(c_ctx)
    mods = _mod_all(c_all, mod_w, mod_b).reshape(depth, rows, 6, d)
    xc = ctx
    for i in range(depth):
        m, occ = i % n_mixers, i // n_mixers
        last = i == depth - 1
        mod = mods[i, :b]
        modc = mods[i, b:b + 1]
        g1 = norm_g[i, 0]
        if m == 0:
            nq = win_sink.shape[1] * HEAD_DIM
            nk = WIN_KV_HEADS * HEAD_DIM
            w = _bf(win_w_qkv[occ])
            qkv = _proj(x, g1, mod, w, cos, sin, n_rope=nq + nk, n_q=nq, scale_i=1, shift_i=0, name="win_qkv")
            qkvc = _proj(xc, g1, modc, w, None, None, n_rope=0, n_q=nq, scale_i=1, shift_i=0, name="win_qkv_ctx")
            o = _win_attn(qkv, qkvc, win_sink[occ], local=True, name="win_attn")
            wo = _bf(win_w_o[occ])
            x = _res_proj(o, wo, x, mod, gate_i=2, name="win_out")
            if not last:
                oc = _win_attn(None, qkvc, win_sink[occ], local=False, name="win_attn_ctx")
                xc = _res_proj(oc, wo, xc, modc, gate_i=2, name="win_out_ctx")
        elif m == 1:
            lambda_init = 0.8 - 0.6 * math.exp(-0.3 * i)
            w = _bf(diff_w_qkv[occ])
            qkv = _proj(x, g1, mod, w, cos, sin, n_rope=2 * d, n_q=d, scale_i=1, shift_i=0, name="diff_qkv")
            qkvc = _proj(xc, g1, modc, w, None, None, n_rope=0, n_q=d, scale_i=1, shift_i=0, name="diff_qkv_ctx")
            o = _diff_attn(qkv, qkvc, diff_lambda[occ], diff_subln_g[occ], local=True,
                           lambda_init=lambda_init, name="diff_attn")
            wo = _bf(diff_w_o[occ])
            x = _res_proj(o, wo, x, mod, gate_i=2, name="diff_out")
            if not last:
                oc = _diff_attn(None, qkvc, diff_lambda[occ], diff_subln_g[occ], local=False,
                                lambda_init=lambda_init, name="diff_attn_ctx")
                xc = _res_proj(oc, wo, xc, modc, gate_i=2, name="diff_out_ctx")
        elif m == 2:
            x = _pool_mix(x, g1, mod, pool_w_group[occ], pool_b_group[occ], pool_scale[occ], name="pool")
            if not last:
                xc = _pool_mix(xc, g1, modc, pool_w_group[occ], pool_b_group[occ], pool_scale[occ], name="pool_ctx")
        else:
            p = dict(mu=rwkv_mu[occ], w_rkv=rwkv_w_rkv[occ], w0=rwkv_w0[occ], w_a1=rwkv_w_a1[occ],
                     w_a2=rwkv_w_a2[occ], a0=rwkv_a0[occ], a_a1=rwkv_a_a1[occ], a_a2=rwkv_a_a2[occ],
                     g1=rwkv_g1[occ], g2=rwkv_g2[occ], k_k=rwkv_k_k[occ], k_a=rwkv_k_a[occ])
            assert last, "the context stream's RWKV output path is only needed for non-final layers"
            r, v, kk, lw, bb, kd, g = _rwkv_features(x, g1, mod, p, with_out=True, name="rwkv_feat")
            vc, kkc, lwc, bc, kdc = _rwkv_features(xc, g1, modc, p, with_out=False, name="rwkv_feat_ctx")
            nh = d // HEAD_DIM
            s0 = jnp.zeros((2, b, nh, HEAD_DIM, HEAD_DIM), F32)
            (s_ctx,) = _rwkv_scan(kkc, vc, None, lwc, bc, kdc, s0, name="rwkv_scan_ctx")
            o, _ = _rwkv_scan(kk, v, r, lw, bb, kd, s_ctx, name="rwkv_scan")
            x = _rwkv_output(o, r, v, kd, g, rwkv_r_k[occ], rwkv_ln_w[occ], rwkv_ln_b[occ], rwkv_w_o[occ],
                             x, mod, name="rwkv_out")
        x, xc = _hier_moe(x, None if last else xc, mod, modc, norm_g[i, 1], moe_wg[i], moe_bg[i], moe_we[i],
                          moe_be[i], moe_w_up[i], moe_w_down[i], i, final_g if last else None)
    return x
```

```python
import functools
import math

import jax
import jax.numpy as jnp
from jax import lax
from jax.experimental import pallas as pl
from jax.experimental.pallas import tpu as pltpu

F32 = jnp.float32
BF16 = jnp.bfloat16

HEAD_DIM = 64
GRID_W = 64
ROPE_BASE = 10000.0
NORM_EPS = 1e-6
NEG_INF = -1e30
WIN_KV_HEADS = 4
WIN_BLOCK = 128
POOL_WINDOWS = (2, 4, 8, 16)
POOL_HALO = 8
GN_EPS = 64e-5
N_GROUPS = 4
EXPERTS_PER_GROUP = 8
N_EXPERTS = N_GROUPS * EXPERTS_PER_GROUP
MOE_BLOCK = 512
SEL_ROWS = 8
DIFF_Q_ROWS = 1024
DIFF_Q_SUB = 256
RWKV_CHUNK = 64
LANES = 128
V7X_VMEM_LIMIT = 48 * 1024 * 1024
HI = lax.Precision.HIGHEST


def _cp(sem, vmem=V7X_VMEM_LIMIT):
    return pltpu.CompilerParams(dimension_semantics=sem, vmem_limit_bytes=vmem)


def _bf(x):
    return x.astype(BF16)


def _dot(a, b, precision=None):
    return jnp.dot(a, b, preferred_element_type=F32, precision=precision)


def _dot_t(a, b, precision=None):
    return lax.dot_general(a, b, (((1,), (1,)), ((), ())), preferred_element_type=F32, precision=precision)


def _dot_l(a, b, precision=None):
    return lax.dot_general(a, b, (((0,), (0,)), ((), ())), preferred_element_type=F32, precision=precision)


def _mm(a, b):
    return _dot(_bf(a), _bf(b))


def _mm_t(a, b):
    return _dot_t(_bf(a), _bf(b))


def _mm_l(a, b):
    return _dot_l(_bf(a), _bf(b))


def _split3(x):
    hi = _bf(x)
    r1 = x - hi.astype(F32)
    mid = _bf(r1)
    return hi, mid, _bf(r1 - mid.astype(F32))


def _cumulate(tri, x):
    hi, mid, lo = _split3(x)
    return _dot(tri, hi) + _dot(tri, mid) + _dot(tri, lo)


def _head_sums(x):
    i = lax.broadcasted_iota(jnp.int32, (LANES, LANES), 0) // HEAD_DIM
    j = lax.broadcasted_iota(jnp.int32, (LANES, LANES), 1) // HEAD_DIM
    ones = _bf(jnp.where(i == j, 1.0, 0.0))
    out = []
    for t in range(x.shape[1] // LANES):
        hi, mid, lo = _split3(x[:, t * LANES:(t + 1) * LANES])
        out.append(_dot(hi, ones) + _dot(mid, ones) + _dot(lo, ones))
    return jnp.concatenate(out, axis=1)


def _norm_mod(x, g, scale, shift):
    ms = jnp.mean(x * x, axis=-1, keepdims=True)
    y = x * lax.rsqrt(ms + NORM_EPS) * g
    return y * (1.0 + scale) + shift


def _sigmoid(x):
    return 1.0 / (1.0 + jnp.exp(-x))


def _row_tile(n, pref):
    t = min(pref, n)
    assert n % t == 0
    return t


def _mod_body(c_ref, w_ref, b_ref, o_ref):
    c = c_ref[...]
    s = c * _sigmoid(c)
    o_ref[0] = _dot(_bf(s), _bf(w_ref[0])) + b_ref[0]


def _mod_all(c_all, mod_w, mod_b):
    depth, d, n = mod_w.shape
    r = c_all.shape[0]
    tn = 1536
    return pl.pallas_call(
        _mod_body,
        grid=(depth, n // tn),
        in_specs=[
            pl.BlockSpec((r, d), lambda i, j: (0, 0)),
            pl.BlockSpec((1, d, tn), lambda i, j: (i, 0, j)),
            pl.BlockSpec((1, 1, tn), lambda i, j: (i, 0, j)),
        ],
        out_specs=pl.BlockSpec((1, r, tn), lambda i, j: (i, 0, j)),
        out_shape=jax.ShapeDtypeStruct((depth, r, n), F32),
        compiler_params=_cp(("parallel", "parallel")),
        name="adaln_mod",
    )(c_all, mod_w, mod_b.reshape(depth, 1, n))


def _mod_spec(mod):
    if mod.shape[0] == 1:
        return pl.BlockSpec((1,) + mod.shape[1:], lambda b, i: (0, 0, 0))
    return pl.BlockSpec((1,) + mod.shape[1:], lambda b, i: (b, 0, 0))


def _rope_tile(y, cos, sin, first_half):
    fwd = pltpu.roll(y, LANES - 16, 1)
    bwd = pltpu.roll(y, 16, 1)
    rot = jnp.where(first_half, -fwd, bwd)
    return y * cos + rot * sin


def _proj_body(*refs, n_rope, n_q, scale_i, shift_i):
    if n_rope:
        x_ref, g_ref, m_ref, w_ref, cos_ref, sin_ref, o_ref = refs
    else:
        x_ref, g_ref, m_ref, w_ref, o_ref = refs
    m = m_ref[0]
    h = _norm_mod(x_ref[0], g_ref[...], m[scale_i:scale_i + 1], m[shift_i:shift_i + 1])
    y = _dot(_bf(h), w_ref[...])
    n = y.shape[1]
    if n_rope:
        cos, sin = cos_ref[...], sin_ref[...]
        first_half = (lax.broadcasted_iota(jnp.int32, cos.shape, 1) & 31) < 16
    for j in range(n // LANES):
        blk = y[:, j * LANES:(j + 1) * LANES]
        if j * LANES < n_q:
            blk = blk * (HEAD_DIM ** -0.5)
        if j * LANES < n_rope:
            blk = _rope_tile(blk, cos, sin, first_half)
        o_ref[0, :, j * LANES:(j + 1) * LANES] = _bf(blk)


def _proj(x, g, mod, w, cos, sin, *, n_rope, n_q, scale_i, shift_i, name):
    b, l, d = x.shape
    n = w.shape[1]
    tm = _row_tile(l, 256)
    in_specs = [
        pl.BlockSpec((1, tm, d), lambda bi, i: (bi, i, 0)),
        pl.BlockSpec((1, d), lambda bi, i: (0, 0)),
        _mod_spec(mod),
        pl.BlockSpec((d, n), lambda bi, i: (0, 0)),
    ]
    args = [x, g.reshape(1, d), mod, w]
    if n_rope:
        in_specs += [pl.BlockSpec((tm, LANES), lambda bi, i: (i, 0))] * 2
        args += [cos, sin]
    return pl.pallas_call(
        functools.partial(_proj_body, n_rope=n_rope, n_q=n_q, scale_i=scale_i, shift_i=shift_i),
        grid=(b, l // tm),
        in_specs=in_specs,
        out_specs=pl.BlockSpec((1, tm, n), lambda bi, i: (bi, i, 0)),
        out_shape=jax.ShapeDtypeStruct((b, l, n), BF16),
        compiler_params=_cp(("parallel", "parallel")),
        name=name,
    )(*args)


def _res_body(a_ref, w_ref, x_ref, m_ref, o_ref, *, gate_i):
    y = _dot(a_ref[0], w_ref[...])
    o_ref[0] = x_ref[0] + m_ref[0][gate_i:gate_i + 1] * y


def _res_proj(a, w, x, mod, *, gate_i, inplace, name):
    b, l, d = x.shape
    k = a.shape[-1]
    tm = _row_tile(l, 512)
    in_specs = [
        pl.BlockSpec((1, tm, k), lambda bi, i: (bi, i, 0)),
        pl.BlockSpec((k, d), lambda bi, i: (0, 0)),
        pl.BlockSpec((1, tm, d), lambda bi, i: (bi, i, 0)),
        _mod_spec(mod),
    ]
    args = [a, w, x, mod]
    return pl.pallas_call(
        functools.partial(_res_body, gate_i=gate_i),
        grid=(b, l // tm),
        in_specs=in_specs,
        out_specs=pl.BlockSpec((1, tm, d), lambda bi, i: (bi, i, 0)),
        out_shape=jax.ShapeDtypeStruct((b, l, d), F32),
        input_output_aliases={2: 0} if inplace else {},
        compiler_params=_cp(("parallel", "parallel")),
        name=name,
    )(*args)


def _win_body(*refs, seq, local, n_heads):
    if local:
        q_ref, kp_ref, kc_ref, kn_ref, vp_ref, vc_ref, vn_ref, kx_ref, vx_ref, sink_ref, o_ref = refs
    else:
        q_ref, kx_ref, vx_ref, sink_ref, o_ref = refs
    tq = q_ref.shape[1]
    grp = n_heads // WIN_KV_HEADS
    n = pl.program_id(1)
    if local:
        span = 3 * tq
        row = lax.broadcasted_iota(jnp.int32, (grp * tq, span), 0) & (tq - 1)
        col = lax.broadcasted_iota(jnp.int32, (grp * tq, span), 1)
        rel = col - tq - row
        key_pos = n * tq - tq + col
        mask = (jnp.abs(rel) <= tq) & (key_pos >= 0) & (key_pos < seq)
    hks = range(WIN_KV_HEADS)
    ks = [slice(hk * HEAD_DIM, (hk + 1) * HEAD_DIM) for hk in hks]
    qh = [jnp.concatenate(
        [q_ref[0, :, (hk * grp + g) * HEAD_DIM:(hk * grp + g + 1) * HEAD_DIM] for g in range(grp)], axis=0)
        for hk in hks]
    sink = [jnp.concatenate([jnp.full((tq, 1), sink_ref[hk * grp + g], F32) for g in range(grp)], axis=0)
            for hk in hks]
    s_ctx = [_dot_t(qh[hk], kx_ref[0, :, ks[hk]]) for hk in hks]
    m = [jnp.maximum(jnp.max(s_ctx[hk], axis=1, keepdims=True), sink[hk]) for hk in hks]
    if local:
        s_loc = [jnp.where(mask, _dot_t(qh[hk], jnp.concatenate(
            [kp_ref[0, :, ks[hk]], kc_ref[0, :, ks[hk]], kn_ref[0, :, ks[hk]]], axis=0)), NEG_INF) for hk in hks]
        m = [jnp.maximum(m[hk], jnp.max(s_loc[hk], axis=1, keepdims=True)) for hk in hks]
    p_ctx = [jnp.exp(s_ctx[hk] - m[hk]) for hk in hks]
    den = [jnp.sum(p_ctx[hk], axis=1, keepdims=True) + jnp.exp(sink[hk] - m[hk]) for hk in hks]
    o = [_dot(_bf(p_ctx[hk]), vx_ref[0, :, ks[hk]]) for hk in hks]
    if local:
        p_loc = [jnp.exp(s_loc[hk] - m[hk]) for hk in hks]
        den = [den[hk] + jnp.sum(p_loc[hk], axis=1, keepdims=True) for hk in hks]
        o = [o[hk] + _dot(_bf(p_loc[hk]), jnp.concatenate(
            [vp_ref[0, :, ks[hk]], vc_ref[0, :, ks[hk]], vn_ref[0, :, ks[hk]]], axis=0)) for hk in hks]
    for hk in hks:
        oh = o[hk] / den[hk]
        for g in range(grp):
            hq = hk * grp + g
            o_ref[0, :, hq * HEAD_DIM:(hq + 1) * HEAD_DIM] = _bf(oh[g * tq:(g + 1) * tq])


def _win_attn(qkv, qkvc, sink, *, local, name):
    src = qkv if local else qkvc
    b, l, _ = src.shape
    c = qkvc.shape[1]
    n_heads = sink.shape[0]
    d = n_heads * HEAD_DIM
    kvw = WIN_KV_HEADS * HEAD_DIM
    kcol = d // kvw
    tq = WIN_BLOCK
    nb = l // tq
    in_specs = [pl.BlockSpec((1, tq, d), lambda bi, i: (bi, i, 0))]
    args = [src]
    if local:
        for colb in (kcol, kcol + 1):
            in_specs += [
                pl.BlockSpec((1, tq, kvw), lambda bi, i, colb=colb: (bi, jnp.maximum(i - 1, 0), colb)),
                pl.BlockSpec((1, tq, kvw), lambda bi, i, colb=colb: (bi, i, colb)),
                pl.BlockSpec((1, tq, kvw), lambda bi, i, colb=colb: (bi, jnp.minimum(i + 1, nb - 1), colb)),
            ]
            args += [qkv, qkv, qkv]
    in_specs += [
        pl.BlockSpec((1, c, kvw), lambda bi, i: (bi, 0, kcol)),
        pl.BlockSpec((1, c, kvw), lambda bi, i: (bi, 0, kcol + 1)),
        pl.BlockSpec(memory_space=pltpu.SMEM),
    ]
    args += [qkvc, qkvc, sink]
    return pl.pallas_call(
        functools.partial(_win_body, seq=l, local=local, n_heads=n_heads),
        grid=(b, nb),
        in_specs=in_specs,
        out_specs=pl.BlockSpec((1, tq, d), lambda bi, i: (bi, i, 0)),
        out_shape=jax.ShapeDtypeStruct((b, l, d), BF16),
        compiler_params=_cp(("parallel", "parallel")),
        name=name,
    )(*args)


def _diff_body(*refs, n_lat, tk, sub, lambda_init):
    if n_lat:
        lam_ref, g_ref, q_ref, kl_ref, vl_ref, kx_ref, vx_ref, o_ref, s_ref = refs
    else:
        lam_ref, g_ref, q_ref, kx_ref, vx_ref, o_ref, s_ref = refs
    hw = 2 * HEAD_DIM
    n_sub = q_ref.shape[1] // sub
    lane = lax.broadcasted_iota(jnp.int32, (sub, hw), 1)
    chunks = [(kl_ref, vl_ref, i * tk, tk, i * tk) for i in range(n_lat)] if n_lat else []
    chunks.append((kx_ref, vx_ref, 0, kx_ref.shape[1], n_lat * tk))
    lam = lam_ref[...]
    lam_full = (jnp.exp(jnp.sum(lam[0:1] * lam[1:2], axis=1, keepdims=True))
                - jnp.exp(jnp.sum(lam[2:3] * lam[3:4], axis=1, keepdims=True)) + lambda_init)

    def stacked_q(j):
        q = q_ref[0, j * sub:(j + 1) * sub, :]
        zero = jnp.zeros_like(q)
        return jnp.concatenate([jnp.where(lane < HEAD_DIM, q, zero), jnp.where(lane >= HEAD_DIM, q, zero)], axis=0)

    def score_chunk(j, qq, ch, mx):
        k_ref, _, row, n, col = ch
        s = _dot_t(qq, k_ref[0, row:row + n, :])
        s_ref[j % 2, :, col:col + n] = s
        for t in range(n // LANES):
            mx = jnp.maximum(mx, s[:, t * LANES:(t + 1) * LANES])
        return mx

    def value_chunk(j, m, ch, acc):
        _, v_ref, row, n, col = ch
        p = jnp.concatenate(
            [jnp.exp(s_ref[j % 2, :, col + t * LANES:col + (t + 1) * LANES] - m) for t in range(n // LANES)], axis=1)
        v = v_ref[0, row:row + n, :]
        return acc + _dot(_bf(p), jnp.concatenate([v, jnp.ones_like(v)], axis=1))

    def finish(j, acc):
        a = acc[:, :hw] / acc[:, hw:]
        o = a[:sub] - lam_full * a[sub:]
        ms = jnp.mean(o * o, axis=-1, keepdims=True)
        o = o * lax.rsqrt(ms + NORM_EPS) * g_ref[...] * (1.0 - lambda_init)
        o_ref[0, j * sub:(j + 1) * sub, :] = _bf(o)

    m_prev = None
    for j in range(n_sub + 1):
        if j < n_sub:
            qq = stacked_q(j)
            mx = jnp.full((2 * sub, LANES), NEG_INF, F32)
        if j > 0:
            acc = jnp.zeros((2 * sub, 2 * hw), F32)
        for ch in chunks:
            if j > 0:
                acc = value_chunk(j - 1, m_prev, ch, acc)
            if j < n_sub:
                mx = score_chunk(j, qq, ch, mx)
        if j > 0:
            finish(j - 1, acc)
        if j < n_sub:
            m_prev = jnp.broadcast_to(jnp.max(mx, axis=1, keepdims=True), (2 * sub, LANES))


def _diff_attn(qkv, qkvc, lam, subln_g, *, local, lambda_init, name):
    src = qkv if local else qkvc
    b, l, n3 = src.shape
    d = n3 // 3
    c = qkvc.shape[1]
    hw = 2 * HEAD_DIM
    nh = d // hw
    tq = _row_tile(l, DIFF_Q_ROWS)
    sub = min(DIFF_Q_SUB, tq)
    tk = 512
    in_specs = [
        pl.BlockSpec((4, HEAD_DIM), lambda bi, h, i: (0, 0)),
        pl.BlockSpec((1, hw), lambda bi, h, i: (0, 0)),
        pl.BlockSpec((1, tq, hw), lambda bi, h, i: (bi, i, h)),
    ]
    args = [lam, subln_g.reshape(1, hw), src]
    n_lat = 0
    if local:
        s = qkv.shape[1]
        n_lat = s // tk
        in_specs += [
            pl.BlockSpec((1, s, hw), lambda bi, h, i: (bi, 0, nh + h)),
            pl.BlockSpec((1, s, hw), lambda bi, h, i: (bi, 0, 2 * nh + h)),
        ]
        args += [qkv, qkv]
    in_specs += [
        pl.BlockSpec((1, c, hw), lambda bi, h, i: (bi, 0, nh + h)),
        pl.BlockSpec((1, c, hw), lambda bi, h, i: (bi, 0, 2 * nh + h)),
    ]
    args += [qkvc, qkvc]
    return pl.pallas_call(
        functools.partial(_diff_body, n_lat=n_lat, tk=tk, sub=sub, lambda_init=lambda_init),
        grid=(b, nh, l // tq),
        in_specs=in_specs,
        out_specs=pl.BlockSpec((1, tq, hw), lambda bi, h, i: (bi, i, h)),
        out_shape=jax.ShapeDtypeStruct((b, l, d), BF16),
        scratch_shapes=[pltpu.VMEM((2, 2 * sub, n_lat * tk + c), F32)],
        compiler_params=_cp(("parallel", "parallel", "parallel")),
        name=name,
    )(*args)


def _halo_specs(l, tm, d):
    nh = l // POOL_HALO
    per = tm // POOL_HALO
    return [
        pl.BlockSpec((1, POOL_HALO, d), lambda bi, i: (bi, jnp.maximum(i * per - 1, 0), 0)),
        pl.BlockSpec((1, tm, d), lambda bi, i: (bi, i, 0)),
        pl.BlockSpec((1, POOL_HALO, d), lambda bi, i: (bi, jnp.minimum((i + 1) * per, nh - 1), 0)),
    ]


def _fill_normed(h_ref, xp_ref, x_ref, xn_ref, g, scale, shift):
    tm = x_ref.shape[1]
    h_ref[0:POOL_HALO, :] = _norm_mod(xp_ref[0], g, scale, shift)
    h_ref[POOL_HALO:POOL_HALO + tm, :] = _norm_mod(x_ref[0], g, scale, shift)
    h_ref[POOL_HALO + tm:2 * POOL_HALO + tm, :] = _norm_mod(xn_ref[0], g, scale, shift)


def _pool_body(xp_ref, x_ref, xn_ref, g_ref, m_ref, w_ref, b_ref, s_ref, o_ref, h_ref, *, seq):
    tm = x_ref.shape[1]
    d = x_ref.shape[2]
    gd = d // len(POOL_WINDOWS)
    m = m_ref[0]
    _fill_normed(h_ref, xp_ref, x_ref, xn_ref, g_ref[...], m[1:2], m[0:1])
    pos = pl.program_id(1) * tm + lax.broadcasted_iota(jnp.int32, (tm, 1), 0)
    x = x_ref[0]
    for gi, w in enumerate(POOL_WINDOWS):
        cols = slice(gi * gd, (gi + 1) * gd)
        acc = jnp.zeros((tm, gd), F32)
        for off in range(-(w // 2), w - w // 2):
            valid = (pos + off >= 0) & (pos + off < seq)
            acc = acc + jnp.where(valid, h_ref[POOL_HALO + off:POOL_HALO + off + tm, cols], 0.0)
        lo = jnp.maximum(pos - w // 2, 0)
        hi = jnp.minimum(pos + w - w // 2, seq)
        y = acc / (hi - lo).astype(F32) - h_ref[POOL_HALO:POOL_HALO + tm, cols]
        y = (_dot(_bf(y), w_ref[gi]) + b_ref[:, cols]) * s_ref[:, cols]
        o_ref[0, :, cols] = x[:, cols] + m[2:3, cols] * y


def _pool_mix(x, g, mod, w_group, b_group, layer_scale, *, name):
    b, l, d = x.shape
    tm = _row_tile(l, 256)
    gd = d // len(POOL_WINDOWS)
    return pl.pallas_call(
        functools.partial(_pool_body, seq=l),
        grid=(b, l // tm),
        in_specs=_halo_specs(l, tm, d) + [
            pl.BlockSpec((1, d), lambda bi, i: (0, 0)),
            _mod_spec(mod),
            pl.BlockSpec((len(POOL_WINDOWS), gd, gd), lambda bi, i: (0, 0, 0)),
            pl.BlockSpec((1, d), lambda bi, i: (0, 0)),
            pl.BlockSpec((1, d), lambda bi, i: (0, 0)),
        ],
        out_specs=pl.BlockSpec((1, tm, d), lambda bi, i: (bi, i, 0)),
        out_shape=jax.ShapeDtypeStruct((b, l, d), F32),
        scratch_shapes=[pltpu.VMEM((tm + 2 * POOL_HALO, d), F32)],
        compiler_params=_cp(("parallel", "parallel")),
        name=name,
    )(x, x, x, g.reshape(1, d), mod, _bf(w_group), b_group.reshape(1, d), layer_scale.reshape(1, d))


def _softplus(z):
    return jnp.maximum(z, 0.0) + jnp.log1p(jnp.exp(-jnp.abs(z)))


def _rwkv_feat_body(*refs, seq, with_out):
    (xp_ref, x_ref, xn_ref, g_ref, m_ref, mu_ref, wrkv_ref, w0_ref, wa1_ref, wa2_ref, a0_ref, aa1_ref,
     aa2_ref, g1_ref, g2_ref, kk_ref, ka_ref) = refs[:17]
    if with_out:
        r_out, v_out, kk_out, lw_out, b_out, kd_out, g_out, h_ref = refs[17:]
    else:
        v_out, kk_out, lw_out, b_out, kd_out, h_ref = refs[17:]
    tm = x_ref.shape[1]
    m = m_ref[0]
    _fill_normed(h_ref, xp_ref, x_ref, xn_ref, g_ref[...], m[1:2], m[0:1])
    pos = pl.program_id(1) * tm + lax.broadcasted_iota(jnp.int32, (tm, 1), 0)
    t = h_ref[POOL_HALO:POOL_HALO + tm, :]
    dp = jnp.where(pos >= 1, h_ref[POOL_HALO - 1:POOL_HALO - 1 + tm, :], 0.0) - t
    dn = jnp.where(pos < seq - 1, h_ref[POOL_HALO + 1:POOL_HALO + 1 + tm, :], 0.0) - t

    def mix(i):
        return _bf(t + dp * mu_ref[0, i:i + 1, :] + dn * mu_ref[1, i:i + 1, :])

    k = _dot(mix(2), wrkv_ref[1])
    v_out[0] = _dot(mix(3), wrkv_ref[2])
    kk = k * kk_ref[...]
    kk = kk / jnp.maximum(jnp.sqrt(_head_sums(kk * kk)), 1e-12)
    kk_out[0] = kk
    xw, xa = mix(1), mix(4)
    for di in range(2):
        z = w0_ref[di:di + 1, :] + _dot(_bf(jnp.tanh(_dot(xw, wa1_ref[di]))), wa2_ref[di])
        w = -_softplus(-z) - 0.5
        lw_out[di, 0] = -jnp.exp(w)
        a = _sigmoid(a0_ref[di:di + 1, :] + _dot(_bf(_dot(xa, aa1_ref[di])), aa2_ref[di]))
        kd_out[di, 0] = k * (1.0 + (a - 1.0) * ka_ref[...])
        b_out[di, 0] = kk * a
    if with_out:
        r_out[0] = _dot(mix(0), wrkv_ref[0])
        g_out[0] = _dot(_bf(_sigmoid(_dot(mix(5), g1_ref[...]))), g2_ref[...])


def _rwkv_features(x, g, mod, p, *, with_out, name):
    b, l, d = x.shape
    tm = _row_tile(l, 128)
    const = lambda a: pl.BlockSpec(a.shape, lambda bi, i, nd=a.ndim: (0,) * nd)
    weights = [p["mu"], _bf(p["w_rkv"]), p["w0"], _bf(p["w_a1"]), _bf(p["w_a2"]), p["a0"], _bf(p["a_a1"]),
               _bf(p["a_a2"]), _bf(p["g1"]), _bf(p["g2"]), p["k_k"].reshape(1, d), p["k_a"].reshape(1, d)]
    one = pl.BlockSpec((1, tm, d), lambda bi, i: (bi, i, 0))
    two = pl.BlockSpec((2, 1, tm, d), lambda bi, i: (0, bi, i, 0))
    s1 = jax.ShapeDtypeStruct((b, l, d), F32)
    s2 = jax.ShapeDtypeStruct((2, b, l, d), F32)
    out_specs = [one, one, two, two, two]
    out_shape = [s1, s1, s2, s2, s2]
    if with_out:
        out_specs = [one] + out_specs + [one]
        out_shape = [s1] + out_shape + [s1]
    return pl.pallas_call(
        functools.partial(_rwkv_feat_body, seq=l, with_out=with_out),
        grid=(b, l // tm),
        in_specs=_halo_specs(l, tm, d) + [pl.BlockSpec((1, d), lambda bi, i: (0, 0)), _mod_spec(mod)]
        + [const(a) for a in weights],
        out_specs=out_specs,
        out_shape=out_shape,
        scratch_shapes=[pltpu.VMEM((tm + 2 * POOL_HALO, d), F32)],
        compiler_params=_cp(("parallel", "parallel")),
        name=name,
    )(x, x, x, g.reshape(1, d), mod, *weights)


def _scan_body(*refs, emit, n_chunks):
    if emit:
        kk_ref, v_ref, r_ref, lw_ref, b_ref, kd_ref, s0_ref, o_ref, sfin_ref, s_ref = refs
    else:
        kk_ref, v_ref, lw_ref, b_ref, kd_ref, s0_ref, sfin_ref, s_ref = refs
    hb = s_ref.shape[0]
    L = kk_ref.shape[1]
    rev = pl.program_id(0) == 1
    c = pl.program_id(2)

    @pl.when(c == 0)
    def _():
        s_ref[...] = s0_ref[0, 0]

    row = lax.broadcasted_iota(jnp.int32, (L, L), 0)
    col = lax.broadcasted_iota(jnp.int32, (L, L), 1)
    flip = rev.astype(jnp.int32)
    p_row = row + flip * (L - 1 - 2 * row)
    p_col = col + flip * (L - 1 - 2 * col)
    incl = p_col <= p_row
    strict = p_col < p_row
    tri = _bf(incl.astype(F32))
    eye = row == col
    assert L == HEAD_DIM
    levels = []
    m = 1
    while m < L:
        same = (p_row // (2 * m)) == (p_col // (2 * m))
        levels.append(same & ((p_row & (2 * m - 1)) >= m) & ((p_col & (2 * m - 1)) < m))
        m *= 2

    hs = range(hb)
    cut = lambda a: [a[:, h * HEAD_DIM:(h + 1) * HEAD_DIM] for h in hs]
    lw_all = lw_ref[0, 0]
    b_all = b_ref[0, 0]
    kd_all = kd_ref[0, 0]
    cl_all = _cumulate(tri, lw_all)
    tot_all = jnp.sum(lw_all, axis=0, keepdims=True)
    w_inv = jnp.exp(-cl_all)
    w_last = jnp.exp(tot_all - cl_all)
    kh = cut(kk_ref[0] * jnp.exp(cl_all - lw_all))
    bt, kt = cut(b_all * w_inv), cut(kd_all * w_inv)
    bt_l, kt_l = cut(b_all * w_last), cut(kd_all * w_last)
    w_tot = cut(jnp.exp(tot_all))
    v = cut(v_ref[0])
    s0 = [s_ref[h] for h in hs]
    rhs = [jnp.concatenate([bt[h], kt[h]], axis=0) for h in hs]
    if emit:
        rh = cut(r_ref[0] * jnp.exp(cl_all))
        lhs = [jnp.concatenate([kh[h], rh[h]], axis=0) for h in hs]
    else:
        lhs = kh
    big = [_mm_t(lhs[h], rhs[h]) for h in hs]
    m_b = [big[h][:L, :L] for h in hs]
    mkv = [_mm(jnp.where(strict, big[h][:L, L:], 0.0), v[h]) for h in hs]
    t = [jnp.where(eye, 1.0, 0.0) - jnp.where(levels[0], m_b[h], 0.0) for h in hs]
    for lm in levels[1:]:
        tl = [_mm(t[h], jnp.where(lm, m_b[h], 0.0)) for h in hs]
        t = [t[h] - _mm(tl[h], t[h]) for h in hs]
    y = [_mm(t[h], jnp.concatenate([kh[h], mkv[h]], axis=1)) for h in hs]
    gh = [_mm_l(y[h], bt_l[h]) for h in hs]
    vk = [_mm_l(v[h], kt_l[h]) for h in hs]
    g_mat = [jnp.where(eye, w_tot[h], 0.0) - gh[h][:HEAD_DIM] for h in hs]
    if emit:
        ab_y = [_mm(jnp.where(incl, big[h][L:, :L], 0.0), y[h]) for h in hs]
        akv = [_mm(jnp.where(incl, big[h][L:, L:], 0.0), v[h]) for h in hs]
        rs = [_mm_t(rh[h] - ab_y[h][:, :HEAD_DIM], s0[h]) for h in hs]
        o_ref[0, 0] = jnp.concatenate([rs[h] + akv[h] - ab_y[h][:, HEAD_DIM:] for h in hs], axis=1)
    sg = [_mm(s0[h], g_mat[h]) for h in hs]
    for h in hs:
        s_ref[h] = sg[h] + vk[h] - gh[h][HEAD_DIM:]

    @pl.when(c == n_chunks - 1)
    def _():
        sfin_ref[0, 0] = s_ref[...]


def _rwkv_scan(kk, v, r, lw, b, kd, s0, *, name):
    bsz, l, d = kk.shape
    dh = HEAD_DIM
    nh = d // dh
    L = RWKV_CHUNK
    nch = l // L
    emit = r is not None

    def chunk(di, c):
        return c + di * (nch - 1 - 2 * c)

    one = pl.BlockSpec((1, L, d), lambda di, bi, c: (bi, chunk(di, c), 0))
    two = pl.BlockSpec((1, 1, L, d), lambda di, bi, c: (di, bi, chunk(di, c), 0))
    st = pl.BlockSpec((1, 1, nh, dh, dh), lambda di, bi, c: (di, bi, 0, 0, 0))
    in_specs = [one, one] + ([one] if emit else []) + [two, two, two, st]
    args = [kk, v] + ([r] if emit else []) + [lw, b, kd, s0]
    s_shape = jax.ShapeDtypeStruct((2, bsz, nh, dh, dh), F32)
    if emit:
        out_specs = [two, st]
        out_shape = [jax.ShapeDtypeStruct((2, bsz, l, d), F32), s_shape]
    else:
        out_specs = [st]
        out_shape = [s_shape]
    return pl.pallas_call(
        functools.partial(_scan_body, emit=emit, n_chunks=nch),
        grid=(2, bsz, nch),
        in_specs=in_specs,
        out_specs=out_specs,
        out_shape=out_shape,
        scratch_shapes=[pltpu.VMEM((nh, dh, dh), F32)],
        compiler_params=_cp(("parallel", "parallel", "arbitrary")),
        name=name,
    )(*args)


def _rwkv_out_body(o_ref, r_ref, v_ref, kd_ref, g_ref, rk_ref, lnw_ref, lnb_ref, w_ref, x_ref, m_ref, out_ref):
    o = o_ref[0, 0] + o_ref[1, 0]
    mean = _head_sums(o) * (1.0 / HEAD_DIM)
    cen = o - mean
    var = _head_sums(cen * cen) * (1.0 / HEAD_DIM)
    on = cen * lax.rsqrt(var + GN_EPS) * lnw_ref[...] + lnb_ref[...]
    bonus = _head_sums(r_ref[0] * (kd_ref[0, 0] + kd_ref[1, 0]) * rk_ref[...]) * v_ref[0]
    a = (on + bonus) * g_ref[0]
    out_ref[0] = x_ref[0] + m_ref[0][2:3] * _dot(_bf(a), w_ref[...])


def _rwkv_output(o, r, v, kd, g, r_k, ln_w, ln_b, w_o, x, mod, *, name):
    b, l, d = x.shape
    tm = _row_tile(l, 256)
    row = pl.BlockSpec((1, tm, d), lambda bi, i: (bi, i, 0))
    row2 = pl.BlockSpec((2, 1, tm, d), lambda bi, i: (0, bi, i, 0))
    small = pl.BlockSpec((1, d), lambda bi, i: (0, 0))
    return pl.pallas_call(
        _rwkv_out_body,
        grid=(b, l // tm),
        in_specs=[row2, row, row, row2, row, small, small, small,
                  pl.BlockSpec((d, d), lambda bi, i: (0, 0)), row, _mod_spec(mod)],
        out_specs=row,
        out_shape=jax.ShapeDtypeStruct((b, l, d), F32),
        input_output_aliases={9: 0},
        compiler_params=_cp(("parallel", "parallel")),
        name=name,
    )(o, r, v, kd, g, r_k.reshape(1, d), ln_w.reshape(1, d), ln_b.reshape(1, d), _bf(w_o), x, mod)


def _router_body(*refs, lat_tiles):
    if lat_tiles is None:
        x_ref, g_ref, m_ref, whi_ref, wlo_ref, b_ref, h_ref, sel_ref, selt_ref, cnt_ref, run_ref = refs
        xt, m = x_ref[0], m_ref[0]
    else:
        (x_ref, xc_ref, g_ref, m_ref, mc_ref, whi_ref, wlo_ref, b_ref, h_ref, sel_ref, selt_ref, cnt_ref,
         run_ref) = refs
        is_ctx = pl.program_id(1) >= lat_tiles
        xt = jnp.where(is_ctx, xc_ref[0], x_ref[0])
        m = jnp.where(is_ctx, mc_ref[0], m_ref[0])

    @pl.when((pl.program_id(0) == 0) & (pl.program_id(1) == 0))
    def _():
        run_ref[...] = jnp.zeros_like(run_ref)

    h = _norm_mod(xt, g_ref[...], m[4:5], m[3:4])
    hi = _bf(h)
    h_ref[0] = h
    lo_part = _bf(h - hi.astype(F32))
    lg = _dot(hi, whi_ref[...]) + _dot(hi, wlo_ref[...]) + _dot(lo_part, whi_ref[...]) + b_ref[...]
    lane = lax.broadcasted_iota(jnp.int32, lg.shape, 1)
    big = jnp.int32(1 << 20)
    g_logit = jnp.where(lane < N_GROUPS, lg, -jnp.inf)
    g_max = jnp.max(g_logit, axis=1, keepdims=True)
    gsel = jnp.min(jnp.where(g_logit == g_max, lane, big), axis=1, keepdims=True)
    p_grp = 1.0 / jnp.sum(jnp.exp(g_logit - g_max), axis=1, keepdims=True)
    lo = N_GROUPS + gsel * EXPERTS_PER_GROUP
    e_logit = jnp.where((lane >= lo) & (lane < lo + EXPERTS_PER_GROUP), lg, -jnp.inf)
    e_max = jnp.max(e_logit, axis=1, keepdims=True)
    i1 = jnp.min(jnp.where(e_logit == e_max, lane, big), axis=1, keepdims=True)
    rest = jnp.where(lane == i1, -jnp.inf, e_logit)
    e2 = jnp.max(rest, axis=1, keepdims=True)
    i2 = jnp.min(jnp.where(rest == e2, lane, big), axis=1, keepdims=True)
    q2 = jnp.exp(e2 - e_max)
    w1 = p_grp / (1.0 + q2)
    w2 = p_grp * q2 / (1.0 + q2)
    pick1, pick2 = lane == i1, lane == i2
    both = jnp.where(pick1 | pick2, 1.0, 0.0)
    tm = lg.shape[0]
    earlier = lax.broadcasted_iota(jnp.int32, (tm, tm), 1) < lax.broadcasted_iota(jnp.int32, (tm, tm), 0)
    ahead = _dot(_bf(earlier.astype(F32)), _bf(both)) + run_ref[...]
    r1 = jnp.sum(jnp.where(pick1, ahead, 0.0), axis=1, keepdims=True)
    r2 = jnp.sum(jnp.where(pick2, ahead, 0.0), axis=1, keepdims=True)
    run_ref[...] = run_ref[...] + jnp.sum(both, axis=0, keepdims=True)
    cnt_ref[...] = run_ref[...]
    cols = ((i1 - N_GROUPS).astype(F32), (i2 - N_GROUPS).astype(F32), w1, w2, r1, r2)
    sel = jnp.zeros_like(lg)
    for j, val in enumerate(cols):
        sel = jnp.where(lane == j, val, sel)
    sel_ref[0] = sel
    selt_ref[0] = sel.T[:SEL_ROWS, :]


def _router(x, xc, g, mod, modc, w_hi, w_lo, b_r, *, name):
    b, s, d = x.shape
    c = 0 if xc is None else xc.shape[1]
    tm = _row_tile(s, 256)
    assert c % tm == 0
    lat_tiles = s // tm
    l = s + c
    wide = pl.BlockSpec((1, LANES), lambda bi, i: (0, 0))
    weights = [pl.BlockSpec((d, LANES), lambda bi, i: (0, 0))] * 2 + [wide]
    gspec = pl.BlockSpec((1, d), lambda bi, i: (0, 0))
    if xc is None:
        in_specs = [pl.BlockSpec((1, tm, d), lambda bi, i: (bi, i, 0)), gspec, _mod_spec(mod)] + weights
        args = [x, g.reshape(1, d), mod, w_hi, w_lo, b_r]
    else:
        in_specs = [
            pl.BlockSpec((1, tm, d), lambda bi, i: (bi, jnp.minimum(i, lat_tiles - 1), 0)),
            pl.BlockSpec((1, tm, d), lambda bi, i: (bi, jnp.maximum(i - lat_tiles, 0), 0)),
            gspec, _mod_spec(mod), _mod_spec(modc)] + weights
        args = [x, xc, g.reshape(1, d), mod, modc, w_hi, w_lo, b_r]
    return pl.pallas_call(
        functools.partial(_router_body, lat_tiles=None if xc is None else lat_tiles),
        grid=(b, l // tm),
        in_specs=in_specs,
        out_specs=[pl.BlockSpec((1, tm, d), lambda bi, i: (bi, i, 0)),
                   pl.BlockSpec((1, tm, LANES), lambda bi, i: (bi, i, 0)),
                   pl.BlockSpec((1, SEL_ROWS, tm), lambda bi, i: (bi, 0, i)),
                   wide],
        out_shape=[jax.ShapeDtypeStruct((b, l, d), F32), jax.ShapeDtypeStruct((b, l, LANES), F32),
                   jax.ShapeDtypeStruct((b, SEL_ROWS, l), F32), jax.ShapeDtypeStruct((1, LANES), F32)],
        scratch_shapes=[pltpu.VMEM((1, LANES), F32)],
        compiler_params=_cp(("arbitrary", "arbitrary")),
        name=name,
    )(*args)


def _expert_body(be_ref, nu_ref, x_ref, wu_ref, wd_ref, o_ref, wub_ref, wdb_ref):
    i = pl.program_id(0)
    ff = wd_ref.shape[2]

    @pl.when((i == 0) | (be_ref[i] != be_ref[jnp.maximum(i - 1, 0)]))
    def _():
        wub_ref[...] = _bf(wu_ref[0, 0])
        wdb_ref[...] = _bf(wd_ref[0, 0])

    @pl.when(i < nu_ref[0])
    def _():
        u = _dot(_bf(x_ref[...]), wub_ref[...])
        gate = u[:, :ff]
        act = gate * _sigmoid(gate) * u[:, ff:]
        o_ref[...] = _dot(_bf(act), wdb_ref[...])

    @pl.when(i >= nu_ref[0])
    def _():
        o_ref[...] = jnp.zeros_like(o_ref)


def _experts(xb, blk_expert, n_used, w_up, w_down, layer):
    rows, d = xb.shape
    nb = rows // MOE_BLOCK
    ff2 = w_up.shape[3]
    ff = w_down.shape[2]
    return pl.pallas_call(
        _expert_body,
        grid_spec=pltpu.PrefetchScalarGridSpec(
            num_scalar_prefetch=2,
            grid=(nb,),
            in_specs=[
                pl.BlockSpec((MOE_BLOCK, d), lambda i, be, nu: (i, 0)),
                pl.BlockSpec((1, 1, d, ff2), lambda i, be, nu: (layer, be[i], 0, 0)),
                pl.BlockSpec((1, 1, ff, d), lambda i, be, nu: (layer, be[i], 0, 0)),
            ],
            out_specs=pl.BlockSpec((MOE_BLOCK, d), lambda i, be, nu: (i, 0)),
            scratch_shapes=[pltpu.VMEM((d, ff2), BF16), pltpu.VMEM((ff, d), BF16)],
        ),
        out_shape=jax.ShapeDtypeStruct((rows, d), F32),
        compiler_params=_cp(("arbitrary",)),
        name="moe_experts",
    )(blk_expert, n_used, xb, w_up, w_down)


def _moe_res_body(*refs, final):
    if final:
        x_ref, y0_ref, y1_ref, sel_ref, m_ref, fg_ref, o_ref = refs
    else:
        x_ref, y0_ref, y1_ref, sel_ref, m_ref, o_ref = refs
    sel = sel_ref[0]
    y = y0_ref[0] * sel[:, 2:3] + y1_ref[0] * sel[:, 3:4]
    x = x_ref[0] + m_ref[0][5:6] * y
    if final:
        ms = jnp.mean(x * x, axis=-1, keepdims=True)
        x = x * lax.rsqrt(ms + NORM_EPS) * fg_ref[...]
    o_ref[0] = x


def _moe_res(x, y0, y1, sel, mod, final_g=None, *, sel_row0=0, name):
    b, l, d = x.shape
    tm = _row_tile(l, 512)
    assert sel_row0 % tm == 0
    off = sel_row0 // tm
    row = pl.BlockSpec((1, tm, d), lambda bi, i: (bi, i, 0))
    in_specs = [row, row, row, pl.BlockSpec((1, tm, LANES), lambda bi, i: (bi, off + i, 0)), _mod_spec(mod)]
    args = [x, y0, y1, sel, mod]
    if final_g is not None:
        in_specs.append(pl.BlockSpec((1, d), lambda bi, i: (0, 0)))
        args.append(final_g.reshape(1, d))
    return pl.pallas_call(
        functools.partial(_moe_res_body, final=final_g is not None),
        grid=(b, l // tm),
        in_specs=in_specs,
        out_specs=row,
        out_shape=jax.ShapeDtypeStruct((b, l, d), F32),
        input_output_aliases={0: 0},
        compiler_params=_cp(("parallel", "parallel")),
        name=name,
    )(*args)


def _dispatch_tables(eids, counts):
    l = eids.shape[2]
    a = eids.size
    bi = lax.broadcasted_iota(jnp.int32, eids.shape, 0)
    ji = lax.broadcasted_iota(jnp.int32, eids.shape, 1)
    li = lax.broadcasted_iota(jnp.int32, eids.shape, 2)
    tok = bi * l + li
    by_expert = jnp.argsort((eids * a + tok * 2 + ji).reshape(a))
    tok_sorted = tok.reshape(a)[by_expert]
    start = jnp.cumsum(counts) - counts
    nblk = (counts + MOE_BLOCK - 1) // MOE_BLOCK
    blk_end = jnp.cumsum(nblk)
    blk_start = blk_end - nblk
    nb = -(-a // MOE_BLOCK) + N_EXPERTS
    blk = jnp.arange(nb, dtype=jnp.int32)
    blk_expert = jnp.minimum(jnp.sum((blk[:, None] >= blk_end[None, :]).astype(jnp.int32), axis=1), N_EXPERTS - 1)
    local = jnp.arange(MOE_BLOCK, dtype=jnp.int32)[None, :] + ((blk - blk_start[blk_expert]) * MOE_BLOCK)[:, None]
    sorted_idx = jnp.clip(start[blk_expert][:, None] + local, 0, a - 1)
    spread = (jnp.arange(nb * MOE_BLOCK, dtype=jnp.int32) % (a // 2)).reshape(nb, MOE_BLOCK)
    src = jnp.where(local < counts[blk_expert][:, None], tok_sorted[sorted_idx], spread).reshape(nb * MOE_BLOCK)
    return src, blk_start, blk_expert, blk_end[-1:].astype(jnp.int32)


def _row_positions(eids, ranks, blk_start):
    first = jnp.sum(jnp.where(eids[..., None] == jnp.arange(N_EXPERTS, dtype=jnp.int32), blk_start, 0), axis=-1)
    return first * MOE_BLOCK + ranks


def _hier_moe(x, xc, mod, modc, g2, wg, bg, we, be, w_up, w_down, li, final_g=None):
    b, s, d = x.shape
    w_r = jnp.zeros((d, LANES), F32).at[:, :N_GROUPS].set(wg).at[:, N_GROUPS:N_GROUPS + N_EXPERTS].set(we)
    b_r = jnp.zeros((1, LANES), F32).at[0, :N_GROUPS].set(bg).at[0, N_GROUPS:N_GROUPS + N_EXPERTS].set(be)
    w_hi = _bf(w_r)
    w_lo = _bf(w_r - w_hi.astype(F32))
    h, sel, sel_t, cnt = _router(x, xc, g2, mod, modc, w_hi, w_lo, b_r, name=f"router{li}")
    n_tok = h.shape[0] * h.shape[1]
    counts = cnt[0, N_GROUPS:N_GROUPS + N_EXPERTS].astype(jnp.int32)
    eids = sel_t[:, 0:2, :].astype(jnp.int32)
    src, blk_start, blk_expert, n_used = _dispatch_tables(eids, counts)
    yb = _experts(h.reshape(n_tok, d)[src], blk_expert, n_used, w_up, w_down, li)
    pos = _row_positions(eids, sel_t[:, 4:6, :].astype(jnp.int32), blk_start)
    x = _moe_res(x, yb[pos[:, 0, :s]], yb[pos[:, 1, :s]], sel, mod, final_g, name=f"moe_res{li}")
    if xc is not None:
        xc = _moe_res(xc, yb[pos[:, 0, s:]], yb[pos[:, 1, s:]], sel, modc, sel_row0=s, name=f"moe_res_ctx{li}")
    return x, xc


def _rope_tables(seq):
    rows = seq // GRID_W
    row = jnp.repeat(jnp.arange(rows), GRID_W).astype(F32)
    col = jnp.tile(jnp.arange(GRID_W), rows).astype(F32)
    nf = HEAD_DIM // 4
    inv = ROPE_BASE ** (-jnp.arange(nf, dtype=F32) / nf)
    ar, ac = row[:, None] * inv, col[:, None] * inv
    ang = jnp.concatenate([ar, ar, ac, ac] * (LANES // HEAD_DIM), axis=-1)
    return jnp.cos(ang), jnp.sin(ang)


def kernel(x, c, ctx, c_ctx, mod_w, mod_b, norm_g, final_g, win_w_qkv, win_sink, win_w_o, diff_w_qkv, diff_lambda, diff_subln_g, diff_w_o, pool_w_group, pool_b_group, pool_scale, rwkv_mu, rwkv_w_rkv, rwkv_w0, rwkv_w_a1, rwkv_w_a2, rwkv_a0, rwkv_a_a1, rwkv_a_a2, rwkv_g1, rwkv_g2, rwkv_k_k, rwkv_k_a, rwkv_r_k, rwkv_ln_w, rwkv_ln_b, rwkv_w_o, moe_wg, moe_bg, moe_we, moe_be, moe_w_up, moe_w_down):
    b, s, d = x.shape
    depth = mod_w.shape[0]
    n_mixers = 4
    cos, sin = _rope_tables(s)
    rows = -(-(b + 1) // 8) * 8
    c_all = jnp.zeros((rows, d), F32).at[:b].set(c).at[b].set(c_ctx)
    mods = _mod_all(c_all, mod_w, mod_b).reshape(depth, rows, 6, d)
    xc = ctx
    for i in range(depth):
        m, occ = i % n_mixers, i // n_mixers
        last = i == depth - 1
        mod = mods[i, :b]
        modc = mods[i, b:b + 1]
        g1 = norm_g[i, 0]
        if m == 0:
            nq = win_sink.shape[1] * HEAD_DIM
            nk = WIN_KV_HEADS * HEAD_DIM
            w = _bf(win_w_qkv[occ])
            qkv = _proj(x, g1, mod, w, cos, sin, n_rope=nq + nk, n_q=nq, scale_i=1, shift_i=0, name="win_qkv")
            qkvc = _proj(xc, g1, modc, w, None, None, n_rope=0, n_q=nq, scale_i=1, shift_i=0, name="win_qkv_ctx")
            o = _win_attn(qkv, qkvc, win_sink[occ], local=True, name="win_attn")
            wo = _bf(win_w_o[occ])
            x = _res_proj(o, wo, x, mod, gate_i=2, inplace=i > 0, name="win_out")
            if not last:
                oc = _win_attn(None, qkvc, win_sink[occ], local=False, name="win_attn_ctx")
                xc = _res_proj(oc, wo, xc, modc, gate_i=2, inplace=i > 0, name="win_out_ctx")
        elif m == 1:
            lambda_init = 0.8 - 0.6 * math.exp(-0.3 * i)
            w = _bf(diff_w_qkv[occ])
            qkv = _proj(x, g1, mod, w, cos, sin, n_rope=2 * d, n_q=d, scale_i=1, shift_i=0, name="diff_qkv")
            qkvc = _proj(xc, g1, modc, w, None, None, n_rope=0, n_q=d, scale_i=1, shift_i=0, name="diff_qkv_ctx")
            o = _diff_attn(qkv, qkvc, diff_lambda[occ], diff_subln_g[occ], local=True,
                           lambda_init=lambda_init, name="diff_attn")
            wo = _bf(diff_w_o[occ])
            x = _res_proj(o, wo, x, mod, gate_i=2, inplace=i > 0, name="diff_out")
            if not last:
                oc = _diff_attn(None, qkvc, diff_lambda[occ], diff_subln_g[occ], local=False,
                                lambda_init=lambda_init, name="diff_attn_ctx")
                xc = _res_proj(oc, wo, xc, modc, gate_i=2, inplace=i > 0, name="diff_out_ctx")
        elif m == 2:
            x = _pool_mix(x, g1, mod, pool_w_group[occ], pool_b_group[occ], pool_scale[occ], name="pool")
            if not last:
                xc = _pool_mix(xc, g1, modc, pool_w_group[occ], pool_b_group[occ], pool_scale[occ], name="pool_ctx")
        else:
            p = dict(mu=rwkv_mu[occ], w_rkv=rwkv_w_rkv[occ], w0=rwkv_w0[occ], w_a1=rwkv_w_a1[occ],
                     w_a2=rwkv_w_a2[occ], a0=rwkv_a0[occ], a_a1=rwkv_a_a1[occ], a_a2=rwkv_a_a2[occ],
                     g1=rwkv_g1[occ], g2=rwkv_g2[occ], k_k=rwkv_k_k[occ], k_a=rwkv_k_a[occ])
            assert last, "the context stream's RWKV output path is only needed for non-final layers"
            r, v, kk, lw, bb, kd, g = _rwkv_features(x, g1, mod, p, with_out=True, name="rwkv_feat")
            vc, kkc, lwc, bc, kdc = _rwkv_features(xc, g1, modc, p, with_out=False, name="rwkv_feat_ctx")
            nh = d // HEAD_DIM
            s0 = jnp.zeros((2, b, nh, HEAD_DIM, HEAD_DIM), F32)
            (s_ctx,) = _rwkv_scan(kkc, vc, None, lwc, bc, kdc, s0, name="rwkv_scan_ctx")
            o, _ = _rwkv_scan(kk, v, r, lw, bb, kd, s_ctx, name="rwkv_scan")
            x = _rwkv_output(o, r, v, kd, g, rwkv_r_k[occ], rwkv_ln_w[occ], rwkv_ln_b[occ], rwkv_w_o[occ],
                             x, mod, name="rwkv_out")
        x, xc = _hier_moe(x, None if last else xc, mod, modc, norm_g[i, 1], moe_wg[i], moe_bg[i], moe_we[i],
                          moe_be[i], moe_w_up, moe_w_down, i, final_g if last else None)
    return x
```

```python
import functools
import math

import jax
import jax.numpy as jnp
from jax import lax
from jax.experimental import pallas as pl
from jax.experimental.pallas import tpu as pltpu

F32 = jnp.float32
BF16 = jnp.bfloat16

HEAD_DIM = 64
GRID_W = 64
ROPE_BASE = 10000.0
NORM_EPS = 1e-6
NEG_INF = -1e30
WIN_KV_HEADS = 4
WIN_BLOCK = 128
POOL_WINDOWS = (2, 4, 8, 16)
POOL_HALO = 8
GN_EPS = 64e-5
N_GROUPS = 4
EXPERTS_PER_GROUP = 8
N_EXPERTS = N_GROUPS * EXPERTS_PER_GROUP
MOE_BLOCK = 512
SEL_ROWS = 8
DIFF_Q_ROWS = 1024
DIFF_Q_SUB = 256
RWKV_CHUNK = 64
LANES = 128
V7X_VMEM_LIMIT = 48 * 1024 * 1024
HI = lax.Precision.HIGHEST


def _cp(sem, vmem=V7X_VMEM_LIMIT):
    return pltpu.CompilerParams(dimension_semantics=sem, vmem_limit_bytes=vmem)


def _bf(x):
    return x.astype(BF16)


def _dot(a, b, precision=None):
    return jnp.dot(a, b, preferred_element_type=F32, precision=precision)


def _dot_t(a, b, precision=None):
    return lax.dot_general(a, b, (((1,), (1,)), ((), ())), preferred_element_type=F32, precision=precision)


def _dot_l(a, b, precision=None):
    return lax.dot_general(a, b, (((0,), (0,)), ((), ())), preferred_element_type=F32, precision=precision)


def _mm(a, b):
    return _dot(_bf(a), _bf(b))


def _mm_t(a, b):
    return _dot_t(_bf(a), _bf(b))


def _mm_l(a, b):
    return _dot_l(_bf(a), _bf(b))


def _split3(x):
    hi = _bf(x)
    r1 = x - hi.astype(F32)
    mid = _bf(r1)
    return hi, mid, _bf(r1 - mid.astype(F32))


def _cumulate(tri, x):
    hi, mid, lo = _split3(x)
    return _dot(tri, hi) + _dot(tri, mid) + _dot(tri, lo)


def _head_sums(x):
    i = lax.broadcasted_iota(jnp.int32, (LANES, LANES), 0) // HEAD_DIM
    j = lax.broadcasted_iota(jnp.int32, (LANES, LANES), 1) // HEAD_DIM
    ones = _bf(jnp.where(i == j, 1.0, 0.0))
    out = []
    for t in range(x.shape[1] // LANES):
        hi, mid, lo = _split3(x[:, t * LANES:(t + 1) * LANES])
        out.append(_dot(hi, ones) + _dot(mid, ones) + _dot(lo, ones))
    return jnp.concatenate(out, axis=1)


def _norm_mod(x, g, scale, shift):
    ms = jnp.mean(x * x, axis=-1, keepdims=True)
    y = x * lax.rsqrt(ms + NORM_EPS) * g
    return y * (1.0 + scale) + shift


def _sigmoid(x):
    return 1.0 / (1.0 + jnp.exp(-x))


def _row_tile(n, pref):
    t = min(pref, n)
    assert n % t == 0
    return t


def _mod_body(c_ref, w_ref, b_ref, o_ref):
    c = c_ref[...]
    s = c * _sigmoid(c)
    o_ref[0] = _dot(_bf(s), _bf(w_ref[0])) + b_ref[0]


def _mod_all(c_all, mod_w, mod_b):
    depth, d, n = mod_w.shape
    r = c_all.shape[0]
    tn = 1536
    return pl.pallas_call(
        _mod_body,
        grid=(depth, n // tn),
        in_specs=[
            pl.BlockSpec((r, d), lambda i, j: (0, 0)),
            pl.BlockSpec((1, d, tn), lambda i, j: (i, 0, j)),
            pl.BlockSpec((1, 1, tn), lambda i, j: (i, 0, j)),
        ],
        out_specs=pl.BlockSpec((1, r, tn), lambda i, j: (i, 0, j)),
        out_shape=jax.ShapeDtypeStruct((depth, r, n), F32),
        compiler_params=_cp(("parallel", "parallel")),
        name="adaln_mod",
    )(c_all, mod_w, mod_b.reshape(depth, 1, n))


def _mod_spec(mod):
    if mod.shape[0] == 1:
        return pl.BlockSpec((1,) + mod.shape[1:], lambda b, i: (0, 0, 0))
    return pl.BlockSpec((1,) + mod.shape[1:], lambda b, i: (b, 0, 0))


def _rope_tile(y, cos, sin, first_half):
    fwd = pltpu.roll(y, LANES - 16, 1)
    bwd = pltpu.roll(y, 16, 1)
    rot = jnp.where(first_half, -fwd, bwd)
    return y * cos + rot * sin


def _proj_body(*refs, n_rope, n_q, scale_i, shift_i):
    if n_rope:
        x_ref, g_ref, m_ref, w_ref, cos_ref, sin_ref, o_ref = refs
    else:
        x_ref, g_ref, m_ref, w_ref, o_ref = refs
    m = m_ref[0]
    h = _norm_mod(x_ref[0], g_ref[...], m[scale_i:scale_i + 1], m[shift_i:shift_i + 1])
    y = _dot(_bf(h), w_ref[...])
    n = y.shape[1]
    if n_rope:
        cos, sin = cos_ref[...], sin_ref[...]
        first_half = (lax.broadcasted_iota(jnp.int32, cos.shape, 1) & 31) < 16
    for j in range(n // LANES):
        blk = y[:, j * LANES:(j + 1) * LANES]
        if j * LANES < n_q:
            blk = blk * (HEAD_DIM ** -0.5)
        if j * LANES < n_rope:
            blk = _rope_tile(blk, cos, sin, first_half)
        o_ref[0, :, j * LANES:(j + 1) * LANES] = _bf(blk)


def _proj(x, g, mod, w, cos, sin, *, n_rope, n_q, scale_i, shift_i, name):
    b, l, d = x.shape
    n = w.shape[1]
    tm = _row_tile(l, 256)
    in_specs = [
        pl.BlockSpec((1, tm, d), lambda bi, i: (bi, i, 0)),
        pl.BlockSpec((1, d), lambda bi, i: (0, 0)),
        _mod_spec(mod),
        pl.BlockSpec((d, n), lambda bi, i: (0, 0)),
    ]
    args = [x, g.reshape(1, d), mod, w]
    if n_rope:
        in_specs += [pl.BlockSpec((tm, LANES), lambda bi, i: (i, 0))] * 2
        args += [cos, sin]
    return pl.pallas_call(
        functools.partial(_proj_body, n_rope=n_rope, n_q=n_q, scale_i=scale_i, shift_i=shift_i),
        grid=(b, l // tm),
        in_specs=in_specs,
        out_specs=pl.BlockSpec((1, tm, n), lambda bi, i: (bi, i, 0)),
        out_shape=jax.ShapeDtypeStruct((b, l, n), BF16),
        compiler_params=_cp(("parallel", "parallel")),
        name=name,
    )(*args)


def _res_body(a_ref, w_ref, x_ref, m_ref, o_ref, *, gate_i):
    y = _dot(a_ref[0], w_ref[...])
    o_ref[0] = x_ref[0] + m_ref[0][gate_i:gate_i + 1] * y


def _res_proj(a, w, x, mod, *, gate_i, inplace, name):
    b, l, d = x.shape
    k = a.shape[-1]
    tm = _row_tile(l, 512)
    in_specs = [
        pl.BlockSpec((1, tm, k), lambda bi, i: (bi, i, 0)),
        pl.BlockSpec((k, d), lambda bi, i: (0, 0)),
        pl.BlockSpec((1, tm, d), lambda bi, i: (bi, i, 0)),
        _mod_spec(mod),
    ]
    args = [a, w, x, mod]
    return pl.pallas_call(
        functools.partial(_res_body, gate_i=gate_i),
        grid=(b, l // tm),
        in_specs=in_specs,
        out_specs=pl.BlockSpec((1, tm, d), lambda bi, i: (bi, i, 0)),
        out_shape=jax.ShapeDtypeStruct((b, l, d), F32),
        input_output_aliases={2: 0} if inplace else {},
        compiler_params=_cp(("parallel", "parallel")),
        name=name,
    )(*args)


def _win_body(*refs, seq, local, n_heads):
    if local:
        q_ref, kp_ref, kc_ref, kn_ref, vp_ref, vc_ref, vn_ref, kx_ref, vx_ref, sink_ref, o_ref = refs
    else:
        q_ref, kx_ref, vx_ref, sink_ref, o_ref = refs
    tq = q_ref.shape[1]
    grp = n_heads // WIN_KV_HEADS
    n = pl.program_id(1)
    if local:
        span = 3 * tq
        row = lax.broadcasted_iota(jnp.int32, (grp * tq, span), 0) & (tq - 1)
        col = lax.broadcasted_iota(jnp.int32, (grp * tq, span), 1)
        rel = col - tq - row
        key_pos = n * tq - tq + col
        mask = (jnp.abs(rel) <= tq) & (key_pos >= 0) & (key_pos < seq)
    hks = range(WIN_KV_HEADS)
    ks = [slice(hk * HEAD_DIM, (hk + 1) * HEAD_DIM) for hk in hks]
    qh = [jnp.concatenate(
        [q_ref[0, :, (hk * grp + g) * HEAD_DIM:(hk * grp + g + 1) * HEAD_DIM] for g in range(grp)], axis=0)
        for hk in hks]
    sink = [jnp.concatenate([jnp.full((tq, 1), sink_ref[hk * grp + g], F32) for g in range(grp)], axis=0)
            for hk in hks]
    s_ctx = [_dot_t(qh[hk], kx_ref[0, :, ks[hk]]) for hk in hks]
    m = [jnp.maximum(jnp.max(s_ctx[hk], axis=1, keepdims=True), sink[hk]) for hk in hks]
    if local:
        s_loc = [jnp.where(mask, _dot_t(qh[hk], jnp.concatenate(
            [kp_ref[0, :, ks[hk]], kc_ref[0, :, ks[hk]], kn_ref[0, :, ks[hk]]], axis=0)), NEG_INF) for hk in hks]
        m = [jnp.maximum(m[hk], jnp.max(s_loc[hk], axis=1, keepdims=True)) for hk in hks]
    p_ctx = [jnp.exp(s_ctx[hk] - m[hk]) for hk in hks]
    den = [jnp.sum(p_ctx[hk], axis=1, keepdims=True) + jnp.exp(sink[hk] - m[hk]) for hk in hks]
    o = [_dot(_bf(p_ctx[hk]), vx_ref[0, :, ks[hk]]) for hk in hks]
    if local:
        p_loc = [jnp.exp(s_loc[hk] - m[hk]) for hk in hks]
        den = [den[hk] + jnp.sum(p_loc[hk], axis=1, keepdims=True) for hk in hks]
        o = [o[hk] + _dot(_bf(p_loc[hk]), jnp.concatenate(
            [vp_ref[0, :, ks[hk]], vc_ref[0, :, ks[hk]], vn_ref[0, :, ks[hk]]], axis=0)) for hk in hks]
    for hk in hks:
        oh = o[hk] / den[hk]
        for g in range(grp):
            hq = hk * grp + g
            o_ref[0, :, hq * HEAD_DIM:(hq + 1) * HEAD_DIM] = _bf(oh[g * tq:(g + 1) * tq])


def _win_attn(qkv, qkvc, sink, *, local, name):
    src = qkv if local else qkvc
    b, l, _ = src.shape
    c = qkvc.shape[1]
    n_heads = sink.shape[0]
    d = n_heads * HEAD_DIM
    kvw = WIN_KV_HEADS * HEAD_DIM
    kcol = d // kvw
    tq = WIN_BLOCK
    nb = l // tq
    in_specs = [pl.BlockSpec((1, tq, d), lambda bi, i: (bi, i, 0))]
    args = [src]
    if local:
        for colb in (kcol, kcol + 1):
            in_specs += [
                pl.BlockSpec((1, tq, kvw), lambda bi, i, colb=colb: (bi, jnp.maximum(i - 1, 0), colb)),
                pl.BlockSpec((1, tq, kvw), lambda bi, i, colb=colb: (bi, i, colb)),
                pl.BlockSpec((1, tq, kvw), lambda bi, i, colb=colb: (bi, jnp.minimum(i + 1, nb - 1), colb)),
            ]
            args += [qkv, qkv, qkv]
    in_specs += [
        pl.BlockSpec((1, c, kvw), lambda bi, i: (bi, 0, kcol)),
        pl.BlockSpec((1, c, kvw), lambda bi, i: (bi, 0, kcol + 1)),
        pl.BlockSpec(memory_space=pltpu.SMEM),
    ]
    args += [qkvc, qkvc, sink]
    return pl.pallas_call(
        functools.partial(_win_body, seq=l, local=local, n_heads=n_heads),
        grid=(b, nb),
        in_specs=in_specs,
        out_specs=pl.BlockSpec((1, tq, d), lambda bi, i: (bi, i, 0)),
        out_shape=jax.ShapeDtypeStruct((b, l, d), BF16),
        compiler_params=_cp(("parallel", "parallel")),
        name=name,
    )(*args)


def _diff_body(*refs, n_lat, tk, sub, lambda_init):
    if n_lat:
        lam_ref, g_ref, q_ref, kl_ref, vl_ref, kx_ref, vx_ref, o_ref, s_ref = refs
    else:
        lam_ref, g_ref, q_ref, kx_ref, vx_ref, o_ref, s_ref = refs
    hw = 2 * HEAD_DIM
    n_sub = q_ref.shape[1] // sub
    lane = lax.broadcasted_iota(jnp.int32, (sub, hw), 1)
    chunks = [(kl_ref, vl_ref, i * tk, tk, i * tk) for i in range(n_lat)] if n_lat else []
    chunks.append((kx_ref, vx_ref, 0, kx_ref.shape[1], n_lat * tk))
    lam = lam_ref[...]
    lam_full = (jnp.exp(jnp.sum(lam[0:1] * lam[1:2], axis=1, keepdims=True))
                - jnp.exp(jnp.sum(lam[2:3] * lam[3:4], axis=1, keepdims=True)) + lambda_init)

    def stacked_q(j):
        q = q_ref[0, j * sub:(j + 1) * sub, :]
        zero = jnp.zeros_like(q)
        return jnp.concatenate([jnp.where(lane < HEAD_DIM, q, zero), jnp.where(lane >= HEAD_DIM, q, zero)], axis=0)

    def score_chunk(j, qq, ch, mx):
        k_ref, _, row, n, col = ch
        s = _dot_t(qq, k_ref[0, row:row + n, :])
        s_ref[j % 2, :, col:col + n] = s
        for t in range(n // LANES):
            mx = jnp.maximum(mx, s[:, t * LANES:(t + 1) * LANES])
        return mx

    def value_chunk(j, m, ch, acc):
        _, v_ref, row, n, col = ch
        p = jnp.concatenate(
            [jnp.exp(s_ref[j % 2, :, col + t * LANES:col + (t + 1) * LANES] - m) for t in range(n // LANES)], axis=1)
        v = v_ref[0, row:row + n, :]
        return acc + _dot(_bf(p), jnp.concatenate([v, jnp.ones_like(v)], axis=1))

    def finish(j, acc):
        a = acc[:, :hw] / acc[:, hw:]
        o = a[:sub] - lam_full * a[sub:]
        ms = jnp.mean(o * o, axis=-1, keepdims=True)
        o = o * lax.rsqrt(ms + NORM_EPS) * g_ref[...] * (1.0 - lambda_init)
        o_ref[0, j * sub:(j + 1) * sub, :] = _bf(o)

    m_prev = None
    for j in range(n_sub + 1):
        if j < n_sub:
            qq = stacked_q(j)
            mx = jnp.full((2 * sub, LANES), NEG_INF, F32)
        if j > 0:
            acc = jnp.zeros((2 * sub, 2 * hw), F32)
        for ch in chunks:
            if j > 0:
                acc = value_chunk(j - 1, m_prev, ch, acc)
            if j < n_sub:
                mx = score_chunk(j, qq, ch, mx)
        if j > 0:
            finish(j - 1, acc)
        if j < n_sub:
            m_prev = jnp.broadcast_to(jnp.max(mx, axis=1, keepdims=True), (2 * sub, LANES))


def _diff_attn(qkv, qkvc, lam, subln_g, *, local, lambda_init, name):
    src = qkv if local else qkvc
    b, l, n3 = src.shape
    d = n3 // 3
    c = qkvc.shape[1]
    hw = 2 * HEAD_DIM
    nh = d // hw
    tq = _row_tile(l, DIFF_Q_ROWS)
    sub = min(DIFF_Q_SUB, tq)
    tk = 512
    in_specs = [
        pl.BlockSpec((4, HEAD_DIM), lambda bi, h, i: (0, 0)),
        pl.BlockSpec((1, hw), lambda bi, h, i: (0, 0)),
        pl.BlockSpec((1, tq, hw), lambda bi, h, i: (bi, i, h)),
    ]
    args = [lam, subln_g.reshape(1, hw), src]
    n_lat = 0
    if local:
        s = qkv.shape[1]
        n_lat = s // tk
        in_specs += [
            pl.BlockSpec((1, s, hw), lambda bi, h, i: (bi, 0, nh + h)),
            pl.BlockSpec((1, s, hw), lambda bi, h, i: (bi, 0, 2 * nh + h)),
        ]
        args += [qkv, qkv]
    in_specs += [
        pl.BlockSpec((1, c, hw), lambda bi, h, i: (bi, 0, nh + h)),
        pl.BlockSpec((1, c, hw), lambda bi, h, i: (bi, 0, 2 * nh + h)),
    ]
    args += [qkvc, qkvc]
    return pl.pallas_call(
        functools.partial(_diff_body, n_lat=n_lat, tk=tk, sub=sub, lambda_init=lambda_init),
        grid=(b, nh, l // tq),
        in_specs=in_specs,
        out_specs=pl.BlockSpec((1, tq, hw), lambda bi, h, i: (bi, i, h)),
        out_shape=jax.ShapeDtypeStruct((b, l, d), BF16),
        scratch_shapes=[pltpu.VMEM((2, 2 * sub, n_lat * tk + c), F32)],
        compiler_params=_cp(("parallel", "parallel", "parallel")),
        name=name,
    )(*args)


def _halo_specs(l, tm, d):
    nh = l // POOL_HALO
    per = tm // POOL_HALO
    return [
        pl.BlockSpec((1, POOL_HALO, d), lambda bi, i: (bi, jnp.maximum(i * per - 1, 0), 0)),
        pl.BlockSpec((1, tm, d), lambda bi, i: (bi, i, 0)),
        pl.BlockSpec((1, POOL_HALO, d), lambda bi, i: (bi, jnp.minimum((i + 1) * per, nh - 1), 0)),
    ]


def _fill_normed(h_ref, xp_ref, x_ref, xn_ref, g, scale, shift):
    tm = x_ref.shape[1]
    h_ref[0:POOL_HALO, :] = _norm_mod(xp_ref[0], g, scale, shift)
    h_ref[POOL_HALO:POOL_HALO + tm, :] = _norm_mod(x_ref[0], g, scale, shift)
    h_ref[POOL_HALO + tm:2 * POOL_HALO + tm, :] = _norm_mod(xn_ref[0], g, scale, shift)


def _pool_body(xp_ref, x_ref, xn_ref, g_ref, m_ref, w_ref, b_ref, s_ref, o_ref, h_ref, *, seq):
    tm = x_ref.shape[1]
    d = x_ref.shape[2]
    gd = d // len(POOL_WINDOWS)
    m = m_ref[0]
    _fill_normed(h_ref, xp_ref, x_ref, xn_ref, g_ref[...], m[1:2], m[0:1])
    pos = pl.program_id(1) * tm + lax.broadcasted_iota(jnp.int32, (tm, 1), 0)
    x = x_ref[0]
    for gi, w in enumerate(POOL_WINDOWS):
        cols = slice(gi * gd, (gi + 1) * gd)
        acc = jnp.zeros((tm, gd), F32)
        for off in range(-(w // 2), w - w // 2):
            valid = (pos + off >= 0) & (pos + off < seq)
            acc = acc + jnp.where(valid, h_ref[POOL_HALO + off:POOL_HALO + off + tm, cols], 0.0)
        lo = jnp.maximum(pos - w // 2, 0)
        hi = jnp.minimum(pos + w - w // 2, seq)
        y = acc / (hi - lo).astype(F32) - h_ref[POOL_HALO:POOL_HALO + tm, cols]
        y = (_dot(_bf(y), w_ref[gi]) + b_ref[:, cols]) * s_ref[:, cols]
        o_ref[0, :, cols] = x[:, cols] + m[2:3, cols] * y


def _pool_mix(x, g, mod, w_group, b_group, layer_scale, *, name):
    b, l, d = x.shape
    tm = _row_tile(l, 256)
    gd = d // len(POOL_WINDOWS)
    return pl.pallas_call(
        functools.partial(_pool_body, seq=l),
        grid=(b, l // tm),
        in_specs=_halo_specs(l, tm, d) + [
            pl.BlockSpec((1, d), lambda bi, i: (0, 0)),
            _mod_spec(mod),
            pl.BlockSpec((len(POOL_WINDOWS), gd, gd), lambda bi, i: (0, 0, 0)),
            pl.BlockSpec((1, d), lambda bi, i: (0, 0)),
            pl.BlockSpec((1, d), lambda bi, i: (0, 0)),
        ],
        out_specs=pl.BlockSpec((1, tm, d), lambda bi, i: (bi, i, 0)),
        out_shape=jax.ShapeDtypeStruct((b, l, d), F32),
        scratch_shapes=[pltpu.VMEM((tm + 2 * POOL_HALO, d), F32)],
        compiler_params=_cp(("parallel", "parallel")),
        name=name,
    )(x, x, x, g.reshape(1, d), mod, _bf(w_group), b_group.reshape(1, d), layer_scale.reshape(1, d))


def _softplus(z):
    return jnp.maximum(z, 0.0) + jnp.log1p(jnp.exp(-jnp.abs(z)))


def _rwkv_feat_body(*refs, seq, with_out):
    (xp_ref, x_ref, xn_ref, g_ref, m_ref, mu_ref, wrkv_ref, w0_ref, wa1_ref, wa2_ref, a0_ref, aa1_ref,
     aa2_ref, g1_ref, g2_ref, kk_ref, ka_ref) = refs[:17]
    if with_out:
        r_out, v_out, kk_out, lw_out, b_out, kd_out, g_out, h_ref = refs[17:]
    else:
        v_out, kk_out, lw_out, b_out, kd_out, h_ref = refs[17:]
    tm = x_ref.shape[1]
    m = m_ref[0]
    _fill_normed(h_ref, xp_ref, x_ref, xn_ref, g_ref[...], m[1:2], m[0:1])
    pos = pl.program_id(1) * tm + lax.broadcasted_iota(jnp.int32, (tm, 1), 0)
    t = h_ref[POOL_HALO:POOL_HALO + tm, :]
    dp = jnp.where(pos >= 1, h_ref[POOL_HALO - 1:POOL_HALO - 1 + tm, :], 0.0) - t
    dn = jnp.where(pos < seq - 1, h_ref[POOL_HALO + 1:POOL_HALO + 1 + tm, :], 0.0) - t

    def mix(i):
        return _bf(t + dp * mu_ref[0, i:i + 1, :] + dn * mu_ref[1, i:i + 1, :])

    k = _dot(mix(2), wrkv_ref[1])
    v_out[0] = _dot(mix(3), wrkv_ref[2])
    kk = k * kk_ref[...]
    kk = kk / jnp.maximum(jnp.sqrt(_head_sums(kk * kk)), 1e-12)
    kk_out[0] = kk
    xw, xa = mix(1), mix(4)
    for di in range(2):
        z = w0_ref[di:di + 1, :] + _dot(_bf(jnp.tanh(_dot(xw, wa1_ref[di]))), wa2_ref[di])
        w = -_softplus(-z) - 0.5
        lw_out[di, 0] = -jnp.exp(w)
        a = _sigmoid(a0_ref[di:di + 1, :] + _dot(_bf(_dot(xa, aa1_ref[di])), aa2_ref[di]))
        kd_out[di, 0] = k * (1.0 + (a - 1.0) * ka_ref[...])
        b_out[di, 0] = kk * a
    if with_out:
        r_out[0] = _dot(mix(0), wrkv_ref[0])
        g_out[0] = _dot(_bf(_sigmoid(_dot(mix(5), g1_ref[...]))), g2_ref[...])


def _rwkv_features(x, g, mod, p, *, with_out, name):
    b, l, d = x.shape
    tm = _row_tile(l, 256)
    const = lambda a: pl.BlockSpec(a.shape, lambda bi, i, nd=a.ndim: (0,) * nd)
    weights = [p["mu"], _bf(p["w_rkv"]), p["w0"], _bf(p["w_a1"]), _bf(p["w_a2"]), p["a0"], _bf(p["a_a1"]),
               _bf(p["a_a2"]), _bf(p["g1"]), _bf(p["g2"]), p["k_k"].reshape(1, d), p["k_a"].reshape(1, d)]
    one = pl.BlockSpec((1, tm, d), lambda bi, i: (bi, i, 0))
    two = pl.BlockSpec((2, 1, tm, d), lambda bi, i: (0, bi, i, 0))
    s1 = jax.ShapeDtypeStruct((b, l, d), F32)
    s2 = jax.ShapeDtypeStruct((2, b, l, d), F32)
    out_specs = [one, one, two, two, two]
    out_shape = [s1, s1, s2, s2, s2]
    if with_out:
        out_specs = [one] + out_specs + [one]
        out_shape = [s1] + out_shape + [s1]
    return pl.pallas_call(
        functools.partial(_rwkv_feat_body, seq=l, with_out=with_out),
        grid=(b, l // tm),
        in_specs=_halo_specs(l, tm, d) + [pl.BlockSpec((1, d), lambda bi, i: (0, 0)), _mod_spec(mod)]
        + [const(a) for a in weights],
        out_specs=out_specs,
        out_shape=out_shape,
        scratch_shapes=[pltpu.VMEM((tm + 2 * POOL_HALO, d), F32)],
        compiler_params=_cp(("parallel", "parallel")),
        name=name,
    )(x, x, x, g.reshape(1, d), mod, *weights)


def _scan_body(*refs, emit, n_chunks):
    if emit:
        kk_ref, v_ref, r_ref, lw_ref, b_ref, kd_ref, s0_ref, o_ref, sfin_ref, s_ref = refs
    else:
        kk_ref, v_ref, lw_ref, b_ref, kd_ref, s0_ref, sfin_ref, s_ref = refs
    hb = s_ref.shape[0]
    L = kk_ref.shape[1]
    rev = pl.program_id(0) == 1
    c = pl.program_id(2)

    @pl.when(c == 0)
    def _():
        s_ref[...] = s0_ref[0, 0]

    row = lax.broadcasted_iota(jnp.int32, (L, L), 0)
    col = lax.broadcasted_iota(jnp.int32, (L, L), 1)
    flip = rev.astype(jnp.int32)
    p_row = row + flip * (L - 1 - 2 * row)
    p_col = col + flip * (L - 1 - 2 * col)
    incl = p_col <= p_row
    strict = p_col < p_row
    tri = _bf(incl.astype(F32))
    eye = row == col
    assert L == HEAD_DIM
    levels = []
    m = 1
    while m < L:
        same = (p_row // (2 * m)) == (p_col // (2 * m))
        levels.append(same & ((p_row & (2 * m - 1)) >= m) & ((p_col & (2 * m - 1)) < m))
        m *= 2

    hs = range(hb)
    cut = lambda a: [a[:, h * HEAD_DIM:(h + 1) * HEAD_DIM] for h in hs]
    lw_all = lw_ref[0, 0]
    b_all = b_ref[0, 0]
    kd_all = kd_ref[0, 0]
    cl_all = _cumulate(tri, lw_all)
    tot_all = jnp.sum(lw_all, axis=0, keepdims=True)
    w_inv = jnp.exp(-cl_all)
    w_last = jnp.exp(tot_all - cl_all)
    kh = cut(kk_ref[0] * jnp.exp(cl_all - lw_all))
    bt, kt = cut(b_all * w_inv), cut(kd_all * w_inv)
    bt_l, kt_l = cut(b_all * w_last), cut(kd_all * w_last)
    w_tot = cut(jnp.exp(tot_all))
    v = cut(v_ref[0])
    s0 = [s_ref[h] for h in hs]
    rhs = [jnp.concatenate([bt[h], kt[h]], axis=0) for h in hs]
    if emit:
        rh = cut(r_ref[0] * jnp.exp(cl_all))
        lhs = [jnp.concatenate([kh[h], rh[h]], axis=0) for h in hs]
    else:
        lhs = kh
    big = [_mm_t(lhs[h], rhs[h]) for h in hs]
    m_b = [big[h][:L, :L] for h in hs]
    mkv = [_mm(jnp.where(strict, big[h][:L, L:], 0.0), v[h]) for h in hs]
    t = [jnp.where(eye, 1.0, 0.0) - jnp.where(levels[0], m_b[h], 0.0) for h in hs]
    for lm in levels[1:]:
        tl = [_mm(t[h], jnp.where(lm, m_b[h], 0.0)) for h in hs]
        t = [t[h] - _mm(tl[h], t[h]) for h in hs]
    y = [_mm(t[h], jnp.concatenate([kh[h], mkv[h]], axis=1)) for h in hs]
    gh = [_mm_l(y[h], bt_l[h]) for h in hs]
    vk = [_mm_l(v[h], kt_l[h]) for h in hs]
    g_mat = [jnp.where(eye, w_tot[h], 0.0) - gh[h][:HEAD_DIM] for h in hs]
    if emit:
        ab_y = [_mm(jnp.where(incl, big[h][L:, :L], 0.0), y[h]) for h in hs]
        akv = [_mm(jnp.where(incl, big[h][L:, L:], 0.0), v[h]) for h in hs]
        rs = [_mm_t(rh[h] - ab_y[h][:, :HEAD_DIM], s0[h]) for h in hs]
        o_ref[0, 0] = jnp.concatenate([rs[h] + akv[h] - ab_y[h][:, HEAD_DIM:] for h in hs], axis=1)
    sg = [_mm(s0[h], g_mat[h]) for h in hs]
    for h in hs:
        s_ref[h] = sg[h] + vk[h] - gh[h][HEAD_DIM:]

    @pl.when(c == n_chunks - 1)
    def _():
        sfin_ref[0, 0] = s_ref[...]


def _rwkv_scan(kk, v, r, lw, b, kd, s0, *, name):
    bsz, l, d = kk.shape
    dh = HEAD_DIM
    nh = d // dh
    L = RWKV_CHUNK
    nch = l // L
    emit = r is not None

    def chunk(di, c):
        return c + di * (nch - 1 - 2 * c)

    one = pl.BlockSpec((1, L, d), lambda di, bi, c: (bi, chunk(di, c), 0))
    two = pl.BlockSpec((1, 1, L, d), lambda di, bi, c: (di, bi, chunk(di, c), 0))
    st = pl.BlockSpec((1, 1, nh, dh, dh), lambda di, bi, c: (di, bi, 0, 0, 0))
    in_specs = [one, one] + ([one] if emit else []) + [two, two, two, st]
    args = [kk, v] + ([r] if emit else []) + [lw, b, kd, s0]
    s_shape = jax.ShapeDtypeStruct((2, bsz, nh, dh, dh), F32)
    if emit:
        out_specs = [two, st]
        out_shape = [jax.ShapeDtypeStruct((2, bsz, l, d), F32), s_shape]
    else:
        out_specs = [st]
        out_shape = [s_shape]
    return pl.pallas_call(
        functools.partial(_scan_body, emit=emit, n_chunks=nch),
        grid=(2, bsz, nch),
        in_specs=in_specs,
        out_specs=out_specs,
        out_shape=out_shape,
        scratch_shapes=[pltpu.VMEM((nh, dh, dh), F32)],
        compiler_params=_cp(("parallel", "parallel", "arbitrary")),
        name=name,
    )(*args)


def _rwkv_out_body(o_ref, r_ref, v_ref, kd_ref, g_ref, rk_ref, lnw_ref, lnb_ref, w_ref, x_ref, m_ref, out_ref):
    o = o_ref[0, 0] + o_ref[1, 0]
    mean = _head_sums(o) * (1.0 / HEAD_DIM)
    cen = o - mean
    var = _head_sums(cen * cen) * (1.0 / HEAD_DIM)
    on = cen * lax.rsqrt(var + GN_EPS) * lnw_ref[...] + lnb_ref[...]
    bonus = _head_sums(r_ref[0] * (kd_ref[0, 0] + kd_ref[1, 0]) * rk_ref[...]) * v_ref[0]
    a = (on + bonus) * g_ref[0]
    out_ref[0] = x_ref[0] + m_ref[0][2:3] * _dot(_bf(a), w_ref[...])


def _rwkv_output(o, r, v, kd, g, r_k, ln_w, ln_b, w_o, x, mod, *, name):
    b, l, d = x.shape
    tm = _row_tile(l, 256)
    row = pl.BlockSpec((1, tm, d), lambda bi, i: (bi, i, 0))
    row2 = pl.BlockSpec((2, 1, tm, d), lambda bi, i: (0, bi, i, 0))
    small = pl.BlockSpec((1, d), lambda bi, i: (0, 0))
    return pl.pallas_call(
        _rwkv_out_body,
        grid=(b, l // tm),
        in_specs=[row2, row, row, row2, row, small, small, small,
                  pl.BlockSpec((d, d), lambda bi, i: (0, 0)), row, _mod_spec(mod)],
        out_specs=row,
        out_shape=jax.ShapeDtypeStruct((b, l, d), F32),
        input_output_aliases={9: 0},
        compiler_params=_cp(("parallel", "parallel")),
        name=name,
    )(o, r, v, kd, g, r_k.reshape(1, d), ln_w.reshape(1, d), ln_b.reshape(1, d), _bf(w_o), x, mod)


def _router_body(*refs, lat_tiles):
    if lat_tiles is None:
        x_ref, g_ref, m_ref, whi_ref, wlo_ref, b_ref, h_ref, sel_ref, selt_ref, cnt_ref, run_ref = refs
        xt, m = x_ref[0], m_ref[0]
    else:
        (x_ref, xc_ref, g_ref, m_ref, mc_ref, whi_ref, wlo_ref, b_ref, h_ref, sel_ref, selt_ref, cnt_ref,
         run_ref) = refs
        is_ctx = pl.program_id(1) >= lat_tiles
        xt = jnp.where(is_ctx, xc_ref[0], x_ref[0])
        m = jnp.where(is_ctx, mc_ref[0], m_ref[0])

    @pl.when((pl.program_id(0) == 0) & (pl.program_id(1) == 0))
    def _():
        run_ref[...] = jnp.zeros_like(run_ref)

    h = _norm_mod(xt, g_ref[...], m[4:5], m[3:4])
    hi = _bf(h)
    h_ref[0] = h
    lo_part = _bf(h - hi.astype(F32))
    lg = _dot(hi, whi_ref[...]) + _dot(hi, wlo_ref[...]) + _dot(lo_part, whi_ref[...]) + b_ref[...]
    lane = lax.broadcasted_iota(jnp.int32, lg.shape, 1)
    big = jnp.int32(1 << 20)
    g_logit = jnp.where(lane < N_GROUPS, lg, -jnp.inf)
    g_max = jnp.max(g_logit, axis=1, keepdims=True)
    gsel = jnp.min(jnp.where(g_logit == g_max, lane, big), axis=1, keepdims=True)
    p_grp = 1.0 / jnp.sum(jnp.exp(g_logit - g_max), axis=1, keepdims=True)
    lo = N_GROUPS + gsel * EXPERTS_PER_GROUP
    e_logit = jnp.where((lane >= lo) & (lane < lo + EXPERTS_PER_GROUP), lg, -jnp.inf)
    e_max = jnp.max(e_logit, axis=1, keepdims=True)
    i1 = jnp.min(jnp.where(e_logit == e_max, lane, big), axis=1, keepdims=True)
    rest = jnp.where(lane == i1, -jnp.inf, e_logit)
    e2 = jnp.max(rest, axis=1, keepdims=True)
    i2 = jnp.min(jnp.where(rest == e2, lane, big), axis=1, keepdims=True)
    q2 = jnp.exp(e2 - e_max)
    w1 = p_grp / (1.0 + q2)
    w2 = p_grp * q2 / (1.0 + q2)
    pick1, pick2 = lane == i1, lane == i2
    both = jnp.where(pick1 | pick2, 1.0, 0.0)
    tm = lg.shape[0]
    earlier = lax.broadcasted_iota(jnp.int32, (tm, tm), 1) < lax.broadcasted_iota(jnp.int32, (tm, tm), 0)
    ahead = _dot(_bf(earlier.astype(F32)), _bf(both)) + run_ref[...]
    r1 = jnp.sum(jnp.where(pick1, ahead, 0.0), axis=1, keepdims=True)
    r2 = jnp.sum(jnp.where(pick2, ahead, 0.0), axis=1, keepdims=True)
    run_ref[...] = run_ref[...] + jnp.sum(both, axis=0, keepdims=True)
    cnt_ref[...] = run_ref[...]
    cols = ((i1 - N_GROUPS).astype(F32), (i2 - N_GROUPS).astype(F32), w1, w2, r1, r2)
    sel = jnp.zeros_like(lg)
    for j, val in enumerate(cols):
        sel = jnp.where(lane == j, val, sel)
    sel_ref[0] = sel
    selt_ref[0] = sel.T[:SEL_ROWS, :]


def _router(x, xc, g, mod, modc, w_hi, w_lo, b_r, *, name):
    b, s, d = x.shape
    c = 0 if xc is None else xc.shape[1]
    tm = _row_tile(s, 256)
    assert c % tm == 0
    lat_tiles = s // tm
    l = s + c
    wide = pl.BlockSpec((1, LANES), lambda bi, i: (0, 0))
    weights = [pl.BlockSpec((d, LANES), lambda bi, i: (0, 0))] * 2 + [wide]
    gspec = pl.BlockSpec((1, d), lambda bi, i: (0, 0))
    if xc is None:
        in_specs = [pl.BlockSpec((1, tm, d), lambda bi, i: (bi, i, 0)), gspec, _mod_spec(mod)] + weights
        args = [x, g.reshape(1, d), mod, w_hi, w_lo, b_r]
    else:
        in_specs = [
            pl.BlockSpec((1, tm, d), lambda bi, i: (bi, jnp.minimum(i, lat_tiles - 1), 0)),
            pl.BlockSpec((1, tm, d), lambda bi, i: (bi, jnp.maximum(i - lat_tiles, 0), 0)),
            gspec, _mod_spec(mod), _mod_spec(modc)] + weights
        args = [x, xc, g.reshape(1, d), mod, modc, w_hi, w_lo, b_r]
    return pl.pallas_call(
        functools.partial(_router_body, lat_tiles=None if xc is None else lat_tiles),
        grid=(b, l // tm),
        in_specs=in_specs,
        out_specs=[pl.BlockSpec((1, tm, d), lambda bi, i: (bi, i, 0)),
                   pl.BlockSpec((1, tm, LANES), lambda bi, i: (bi, i, 0)),
                   pl.BlockSpec((1, SEL_ROWS, tm), lambda bi, i: (bi, 0, i)),
                   wide],
        out_shape=[jax.ShapeDtypeStruct((b, l, d), F32), jax.ShapeDtypeStruct((b, l, LANES), F32),
                   jax.ShapeDtypeStruct((b, SEL_ROWS, l), F32), jax.ShapeDtypeStruct((1, LANES), F32)],
        scratch_shapes=[pltpu.VMEM((1, LANES), F32)],
        compiler_params=_cp(("arbitrary", "arbitrary")),
        name=name,
    )(*args)


def _expert_body(be_ref, nu_ref, x_ref, wu_ref, wd_ref, o_ref, wub_ref, wdb_ref):
    i = pl.program_id(0)
    ff = wd_ref.shape[2]

    @pl.when((i == 0) | (be_ref[i] != be_ref[jnp.maximum(i - 1, 0)]))
    def _():
        wub_ref[...] = _bf(wu_ref[0, 0])
        wdb_ref[...] = _bf(wd_ref[0, 0])

    @pl.when(i < nu_ref[0])
    def _():
        u = _dot(_bf(x_ref[...]), wub_ref[...])
        gate = u[:, :ff]
        act = gate * _sigmoid(gate) * u[:, ff:]
        o_ref[...] = _dot(_bf(act), wdb_ref[...])

    @pl.when(i >= nu_ref[0])
    def _():
        o_ref[...] = jnp.zeros_like(o_ref)


def _experts(xb, blk_expert, n_used, w_up, w_down, layer):
    rows, d = xb.shape
    nb = rows // MOE_BLOCK
    ff2 = w_up.shape[3]
    ff = w_down.shape[2]
    return pl.pallas_call(
        _expert_body,
        grid_spec=pltpu.PrefetchScalarGridSpec(
            num_scalar_prefetch=2,
            grid=(nb,),
            in_specs=[
                pl.BlockSpec((MOE_BLOCK, d), lambda i, be, nu: (i, 0)),
                pl.BlockSpec((1, 1, d, ff2), lambda i, be, nu: (layer, be[i], 0, 0)),
                pl.BlockSpec((1, 1, ff, d), lambda i, be, nu: (layer, be[i], 0, 0)),
            ],
            out_specs=pl.BlockSpec((MOE_BLOCK, d), lambda i, be, nu: (i, 0)),
            scratch_shapes=[pltpu.VMEM((d, ff2), BF16), pltpu.VMEM((ff, d), BF16)],
        ),
        out_shape=jax.ShapeDtypeStruct((rows, d), F32),
        compiler_params=_cp(("arbitrary",)),
        name="moe_experts",
    )(blk_expert, n_used, xb, w_up, w_down)


def _moe_res_body(*refs, final):
    if final:
        x_ref, y0_ref, y1_ref, sel_ref, m_ref, fg_ref, o_ref = refs
    else:
        x_ref, y0_ref, y1_ref, sel_ref, m_ref, o_ref = refs
    sel = sel_ref[0]
    y = y0_ref[0] * sel[:, 2:3] + y1_ref[0] * sel[:, 3:4]
    x = x_ref[0] + m_ref[0][5:6] * y
    if final:
        ms = jnp.mean(x * x, axis=-1, keepdims=True)
        x = x * lax.rsqrt(ms + NORM_EPS) * fg_ref[...]
    o_ref[0] = x


def _moe_res(x, y0, y1, sel, mod, final_g=None, *, sel_row0=0, name):
    b, l, d = x.shape
    tm = _row_tile(l, 512)
    assert sel_row0 % tm == 0
    off = sel_row0 // tm
    row = pl.BlockSpec((1, tm, d), lambda bi, i: (bi, i, 0))
    in_specs = [row, row, row, pl.BlockSpec((1, tm, LANES), lambda bi, i: (bi, off + i, 0)), _mod_spec(mod)]
    args = [x, y0, y1, sel, mod]
    if final_g is not None:
        in_specs.append(pl.BlockSpec((1, d), lambda bi, i: (0, 0)))
        args.append(final_g.reshape(1, d))
    return pl.pallas_call(
        functools.partial(_moe_res_body, final=final_g is not None),
        grid=(b, l // tm),
        in_specs=in_specs,
        out_specs=row,
        out_shape=jax.ShapeDtypeStruct((b, l, d), F32),
        input_output_aliases={0: 0},
        compiler_params=_cp(("parallel", "parallel")),
        name=name,
    )(*args)


def _dispatch_tables(eids, counts):
    l = eids.shape[2]
    a = eids.size
    bi = lax.broadcasted_iota(jnp.int32, eids.shape, 0)
    ji = lax.broadcasted_iota(jnp.int32, eids.shape, 1)
    li = lax.broadcasted_iota(jnp.int32, eids.shape, 2)
    tok = bi * l + li
    by_expert = jnp.argsort((eids * a + tok * 2 + ji).reshape(a))
    tok_sorted = tok.reshape(a)[by_expert]
    start = jnp.cumsum(counts) - counts
    nblk = (counts + MOE_BLOCK - 1) // MOE_BLOCK
    blk_end = jnp.cumsum(nblk)
    blk_start = blk_end - nblk
    nb = -(-a // MOE_BLOCK) + N_EXPERTS
    blk = jnp.arange(nb, dtype=jnp.int32)
    blk_expert = jnp.minimum(jnp.sum((blk[:, None] >= blk_end[None, :]).astype(jnp.int32), axis=1), N_EXPERTS - 1)
    local = jnp.arange(MOE_BLOCK, dtype=jnp.int32)[None, :] + ((blk - blk_start[blk_expert]) * MOE_BLOCK)[:, None]
    sorted_idx = jnp.clip(start[blk_expert][:, None] + local, 0, a - 1)
    spread = (jnp.arange(nb * MOE_BLOCK, dtype=jnp.int32) % (a // 2)).reshape(nb, MOE_BLOCK)
    src = jnp.where(local < counts[blk_expert][:, None], tok_sorted[sorted_idx], spread).reshape(nb * MOE_BLOCK)
    return src, blk_start, blk_expert, blk_end[-1:].astype(jnp.int32)


def _row_positions(eids, ranks, blk_start):
    first = jnp.sum(jnp.where(eids[..., None] == jnp.arange(N_EXPERTS, dtype=jnp.int32), blk_start, 0), axis=-1)
    return first * MOE_BLOCK + ranks


def _hier_moe(x, xc, mod, modc, g2, wg, bg, we, be, w_up, w_down, li, final_g=None):
    b, s, d = x.shape
    w_r = jnp.zeros((d, LANES), F32).at[:, :N_GROUPS].set(wg).at[:, N_GROUPS:N_GROUPS + N_EXPERTS].set(we)
    b_r = jnp.zeros((1, LANES), F32).at[0, :N_GROUPS].set(bg).at[0, N_GROUPS:N_GROUPS + N_EXPERTS].set(be)
    w_hi = _bf(w_r)
    w_lo = _bf(w_r - w_hi.astype(F32))
    h, sel, sel_t, cnt = _router(x, xc, g2, mod, modc, w_hi, w_lo, b_r, name=f"router{li}")
    n_tok = h.shape[0] * h.shape[1]
    counts = cnt[0, N_GROUPS:N_GROUPS + N_EXPERTS].astype(jnp.int32)
    eids = sel_t[:, 0:2, :].astype(jnp.int32)
    src, blk_start, blk_expert, n_used = _dispatch_tables(eids, counts)
    yb = _experts(h.reshape(n_tok, d)[src], blk_expert, n_used, w_up, w_down, li)
    pos = _row_positions(eids, sel_t[:, 4:6, :].astype(jnp.int32), blk_start)
    x = _moe_res(x, yb[pos[:, 0, :s]], yb[pos[:, 1, :s]], sel, mod, final_g, name=f"moe_res{li}")
    if xc is not None:
        xc = _moe_res(xc, yb[pos[:, 0, s:]], yb[pos[:, 1, s:]], sel, modc, sel_row0=s, name=f"moe_res_ctx{li}")
    return x, xc


def _rope_tables(seq):
    rows = seq // GRID_W
    row = jnp.repeat(jnp.arange(rows), GRID_W).astype(F32)
    col = jnp.tile(jnp.arange(GRID_W), rows).astype(F32)
    nf = HEAD_DIM // 4
    inv = ROPE_BASE ** (-jnp.arange(nf, dtype=F32) / nf)
    ar, ac = row[:, None] * inv, col[:, None] * inv
    ang = jnp.concatenate([ar, ar, ac, ac] * (LANES // HEAD_DIM), axis=-1)
    return jnp.cos(ang), jnp.sin(ang)


def kernel(x, c, ctx, c_ctx, mod_w, mod_b, norm_g, final_g, win_w_qkv, win_sink, win_w_o, diff_w_qkv, diff_lambda, diff_subln_g, diff_w_o, pool_w_group, pool_b_group, pool_scale, rwkv_mu, rwkv_w_rkv, rwkv_w0, rwkv_w_a1, rwkv_w_a2, rwkv_a0, rwkv_a_a1, rwkv_a_a2, rwkv_g1, rwkv_g2, rwkv_k_k, rwkv_k_a, rwkv_r_k, rwkv_ln_w, rwkv_ln_b, rwkv_w_o, moe_wg, moe_bg, moe_we, moe_be, moe_w_up, moe_w_down):
    b, s, d = x.shape
    depth = mod_w.shape[0]
    n_mixers = 4
    cos, sin = _rope_tables(s)
    rows = -(-(b + 1) // 8) * 8
    c_all = jnp.zeros((rows, d), F32).at[:b].set(c).at[b].set(c_ctx)
    mods = _mod_all(c_all, mod_w, mod_b).reshape(depth, rows, 6, d)
    xc = ctx
    for i in range(depth):
        m, occ = i % n_mixers, i // n_mixers
        last = i == depth - 1
        mod = mods[i, :b]
        modc = mods[i, b:b + 1]
        g1 = norm_g[i, 0]
        if m == 0:
            nq = win_sink.shape[1] * HEAD_DIM
            nk = WIN_KV_HEADS * HEAD_DIM
            w = _bf(win_w_qkv[occ])
            qkv = _proj(x, g1, mod, w, cos, sin, n_rope=nq + nk, n_q=nq, scale_i=1, shift_i=0, name="win_qkv")
            qkvc = _proj(xc, g1, modc, w, None, None, n_rope=0, n_q=nq, scale_i=1, shift_i=0, name="win_qkv_ctx")
            o = _win_attn(qkv, qkvc, win_sink[occ], local=True, name="win_attn")
            wo = _bf(win_w_o[occ])
            x = _res_proj(o, wo, x, mod, gate_i=2, inplace=i > 0, name="win_out")
            if not last:
                oc = _win_attn(None, qkvc, win_sink[occ], local=False, name="win_attn_ctx")
                xc = _res_proj(oc, wo, xc, modc, gate_i=2, inplace=i > 0, name="win_out_ctx")
        elif m == 1:
            lambda_init = 0.8 - 0.6 * math.exp(-0.3 * i)
            w = _bf(diff_w_qkv[occ])
            qkv = _proj(x, g1, mod, w, cos, sin, n_rope=2 * d, n_q=d, scale_i=1, shift_i=0, name="diff_qkv")
            qkvc = _proj(xc, g1, modc, w, None, None, n_rope=0, n_q=d, scale_i=1, shift_i=0, name="diff_qkv_ctx")
            o = _diff_attn(qkv, qkvc, diff_lambda[occ], diff_subln_g[occ], local=True,
                           lambda_init=lambda_init, name="diff_attn")
            wo = _bf(diff_w_o[occ])
            x = _res_proj(o, wo, x, mod, gate_i=2, inplace=i > 0, name="diff_out")
            if not last:
                oc = _diff_attn(None, qkvc, diff_lambda[occ], diff_subln_g[occ], local=False,
                                lambda_init=lambda_init, name="diff_attn_ctx")
                xc = _res_proj(oc, wo, xc, modc, gate_i=2, inplace=i > 0, name="diff_out_ctx")
        elif m == 2:
            x = _pool_mix(x, g1, mod, pool_w_group[occ], pool_b_group[occ], pool_scale[occ], name="pool")
            if not last:
                xc = _pool_mix(xc, g1, modc, pool_w_group[occ], pool_b_group[occ], pool_scale[occ], name="pool_ctx")
        else:
            p = dict(mu=rwkv_mu[occ], w_rkv=rwkv_w_rkv[occ], w0=rwkv_w0[occ], w_a1=rwkv_w_a1[occ],
                     w_a2=rwkv_w_a2[occ], a0=rwkv_a0[occ], a_a1=rwkv_a_a1[occ], a_a2=rwkv_a_a2[occ],
                     g1=rwkv_g1[occ], g2=rwkv_g2[occ], k_k=rwkv_k_k[occ], k_a=rwkv_k_a[occ])
            assert last, "the context stream's RWKV output path is only needed for non-final layers"
            r, v, kk, lw, bb, kd, g = _rwkv_features(x, g1, mod, p, with_out=True, name="rwkv_feat")
            vc, kkc, lwc, bc, kdc = _rwkv_features(xc, g1, modc, p, with_out=False, name="rwkv_feat_ctx")
            nh = d // HEAD_DIM
            s0 = jnp.zeros((2, b, nh, HEAD_DIM, HEAD_DIM), F32)
            (s_ctx,) = _rwkv_scan(kkc, vc, None, lwc, bc, kdc, s0, name="rwkv_scan_ctx")
            o, _ = _rwkv_scan(kk, v, r, lw, bb, kd, s_ctx, name="rwkv_scan")
            x = _rwkv_output(o, r, v, kd, g, rwkv_r_k[occ], rwkv_ln_w[occ], rwkv_ln_b[occ], rwkv_w_o[occ],
                             x, mod, name="rwkv_out")
        x, xc = _hier_moe(x, None if last else xc, mod, modc, norm_g[i, 1], moe_wg[i], moe_bg[i], moe_we[i],
                          moe_be[i], moe_w_up, moe_w_down, i, final_g if last else None)
    return x
```
